```python
import math
import jax, jax.numpy as jnp
from jax import lax
import numpy as np

D_MODEL = 1024
BATCH = 4
SEQ = 4096
DEPTH = 4

kernel_name = 'hybrid_ssd_dilated_diffattn_encoder'

N_MIXERS = 3
N_SSD_LAYERS = (DEPTH + 2) // 3
N_DIL_LAYERS = (DEPTH + 1) // 3
N_DIFF_LAYERS = DEPTH // 3

RMS_EPS = 1e-6
D_FF = 4 * D_MODEL

SSD_EXPAND = 2
SSD_D_INNER = SSD_EXPAND * D_MODEL
SSD_HEAD_DIM = 64
SSD_HEADS = SSD_D_INNER // SSD_HEAD_DIM
SSD_GROUPS = 8
SSD_HEADS_PER_GROUP = SSD_HEADS // SSD_GROUPS
SSD_D_STATE = 128
SSD_CONV_WIDTH = 5
SSD_CHUNK = 128
SSD_CONV_CH = SSD_D_INNER + 2 * SSD_GROUPS * SSD_D_STATE
SSD_IN_COLS = SSD_D_INNER + SSD_CONV_CH + 2 * SSD_HEADS

DIL_CONFIGS = ((128, 1), (512, 4), (2048, 16))
DIL_GROUPS = len(DIL_CONFIGS)
DIL_HEADS = 16
DIL_HEAD_DIM = D_MODEL // DIL_HEADS
DIL_WIDTH = DIL_HEADS * DIL_HEAD_DIM
DIL_QKV_COLS = 3 * DIL_GROUPS * DIL_WIDTH

DIFF_HEADS = 8
DIFF_HEAD_DIM = D_MODEL // (2 * DIFF_HEADS)
DIFF_V_DIM = 2 * DIFF_HEAD_DIM
DIFF_QKV_COLS = 3 * D_MODEL
DIFF_QBLK = 128

REL_BUCKETS = 32
REL_MAX_DIST = 1024
REL_BIAS_HEADS = 16
NEG_INF = -1e30


def rmsnorm(x, g):
    xf = x.astype(jnp.float32)
    y = xf * lax.rsqrt(jnp.mean(xf * xf, axis=-1, keepdims=True) + RMS_EPS)
    return (y * g.astype(jnp.float32)).astype(x.dtype)


def rel_bucket(rel):
    half = REL_BUCKETS // 2
    max_exact = half // 2
    ret = jnp.where(rel > 0, half, 0)
    n = jnp.abs(rel)
    nf = jnp.maximum(n, 1).astype(jnp.float32)
    large = max_exact + (jnp.log(nf / max_exact) / math.log(REL_MAX_DIST / max_exact)
                         * (half - max_exact)).astype(jnp.int32)
    large = jnp.minimum(large, half - 1)
    return ret + jnp.where(n < max_exact, n, large)


def rel_bias_lookup(table, rel):
    return table.astype(jnp.float32)[rel_bucket(rel)]


def ssd_chunk_scan(x, dt, a, b_in, c_in):
    bsz, s, g, r, p = x.shape
    n = b_in.shape[-1]
    L = SSD_CHUNK
    nc = s // L
    xdt = x.astype(jnp.float32) * dt[..., None]
    adt = dt * a

    def chunks(t):
        return jnp.swapaxes(t.reshape((bsz, nc, L) + t.shape[2:]), 0, 1)

    tri = jnp.tril(jnp.ones((L, L), dtype=bool))[None, :, :, None, None]

    def step(state, inp):
        xc, ac, bc, cc = inp
        acs = jnp.cumsum(ac, axis=1)
        seg = acs[:, :, None] - acs[:, None, :]
        lmat = jnp.exp(jnp.where(tri, seg, -jnp.inf))
        cb = jnp.einsum('blgn,bsgn->blsg', cc, bc)
        y_diag = jnp.einsum('blsgr,bsgrp->blgrp', cb[..., None] * lmat, xc)
        y_off = jnp.einsum('blgn,bgrpn->blgrp', cc, state) * jnp.exp(acs)[..., None]
        decay = jnp.exp(acs[:, -1:] - acs)
        new_state = (state * jnp.exp(acs[:, -1])[..., None, None]
                     + jnp.einsum('bsgn,bsgr,bsgrp->bgrpn', bc, decay, xc))
        return new_state, y_diag + y_off

    state0 = jnp.zeros((bsz, g, r, p, n), jnp.float32)
    _, ys = lax.scan(step, state0, (chunks(xdt), chunks(adt),
                                    chunks(b_in.astype(jnp.float32)),
                                    chunks(c_in.astype(jnp.float32))))
    return jnp.swapaxes(ys, 0, 1).reshape(bsz, s, g, r, p)


def ssd_mixer(h, in_w, conv_w, conv_b, dt_bias, a_log, d_skip, norm_g, out_w):
    bsz, s, _ = h.shape
    G, R, P, N = SSD_GROUPS, SSD_HEADS_PER_GROUP, SSD_HEAD_DIM, SSD_D_STATE
    zxbcdt = h @ in_w
    z, xbc, dt = jnp.split(zxbcdt, [SSD_D_INNER, SSD_D_INNER + SSD_CONV_CH], axis=-1)
    pad = SSD_CONV_WIDTH // 2
    xbc = lax.conv_general_dilated(xbc, conv_w[:, None, :], (1,), [(pad, pad)],
                                   dimension_numbers=('NWC', 'WIO', 'NWC'),
                                   feature_group_count=SSD_CONV_CH)
    xbc = jax.nn.silu(xbc + conv_b)
    xs, bs, cs = jnp.split(xbc, [SSD_D_INNER, SSD_D_INNER + G * N], axis=-1)
    xs = xs.reshape(bsz, s, G, R, P)
    bs = bs.reshape(bsz, s, G, N)
    cs = cs.reshape(bsz, s, G, N)
    dt = jax.nn.softplus(dt.astype(jnp.float32).reshape(bsz, s, 2, G, R)
                         + dt_bias.astype(jnp.float32).reshape(2, G, R))
    a = -jnp.exp(a_log.astype(jnp.float32)).reshape(2, G, R)
    flip = lambda t: jnp.flip(t, axis=1)
    y_fwd = ssd_chunk_scan(xs, dt[:, :, 0], a[0], bs, cs)
    y_bwd = flip(ssd_chunk_scan(flip(xs), flip(dt[:, :, 1]), a[1], flip(bs), flip(cs)))
    y = y_fwd + y_bwd + d_skip.astype(jnp.float32).reshape(G, R)[..., None] * xs.astype(jnp.float32)
    y = y.reshape(bsz, s, SSD_D_INNER) * jax.nn.silu(z.astype(jnp.float32))
    y = rmsnorm(y, norm_g).astype(h.dtype)
    return y @ out_w


def dilated_group(q, k, v, window, dilation, table):
    bsz, s, nh, hd = q.shape
    d = dilation
    n = s // d
    w = window // (2 * d)
    blk = w
    nb = -(-n // blk)
    n_p = nb * blk

    def to_sub(t):
        return t.reshape(bsz, n, d, nh, hd).transpose(0, 2, 1, 3, 4).astype(jnp.float32)

    qs = jnp.pad(to_sub(q), ((0, 0), (0, 0), (0, n_p - n), (0, 0), (0, 0)))
    qs = qs.reshape(bsz, d, nb, blk, nh, hd)

    def band(t):
        tp = jnp.pad(to_sub(t), ((0, 0), (0, 0), (blk, n_p - n + blk), (0, 0), (0, 0)))
        tp = tp.reshape(bsz, d, nb + 2, blk, nh, hd)
        return jnp.concatenate([tp[:, :, :-2], tp[:, :, 1:-1], tp[:, :, 2:]], axis=3)

    kb, vb = band(k), band(v)
    scores = jnp.einsum('brjqhe,brjkhe->brjhqk', qs, kb) / math.sqrt(hd)
    delta = jnp.arange(3 * blk)[None, :] - blk - jnp.arange(blk)[:, None]
    key_idx = jnp.arange(nb)[:, None] * blk - blk + jnp.arange(3 * blk)[None, :]
    valid = (jnp.abs(delta) <= w)[None] & ((key_idx >= 0) & (key_idx < n))[:, None, :]
    bias = jnp.moveaxis(rel_bias_lookup(table, delta * d), -1, 0)
    scores = jnp.where(valid[:, None], scores + bias, NEG_INF)
    m = jnp.max(scores, axis=-1, keepdims=True)
    p = jnp.exp(scores - m)
    den = jnp.sum(p, axis=-1, keepdims=True)
    o = jnp.einsum('brjhqk,brjkhe->brjqhe', p, vb) / jnp.swapaxes(den, 3, 4)
    lse = jnp.swapaxes((m + jnp.log(den))[..., 0], 3, 4)
    o = o.reshape(bsz, d, n_p, nh, hd)[:, :, :n].transpose(0, 2, 1, 3, 4).reshape(bsz, s, nh, hd)
    lse = lse.reshape(bsz, d, n_p, nh)[:, :, :n].transpose(0, 2, 1, 3).reshape(bsz, s, nh)
    return o, lse


def dilated_mixer(h, qkv_w, out_w, table):
    bsz, s, _ = h.shape
    qkv = (h @ qkv_w).reshape(bsz, s, 3, DIL_GROUPS, DIL_HEADS, DIL_HEAD_DIM)
    outs, lses = [], []
    for gi, (win, dil) in enumerate(DIL_CONFIGS):
        o, l = dilated_group(qkv[:, :, 0, gi], qkv[:, :, 1, gi], qkv[:, :, 2, gi], win, dil, table)
        outs.append(o)
        lses.append(l)
    wts = jax.nn.softmax(jnp.stack(lses), axis=0)
    o = jnp.sum(wts[..., None] * jnp.stack(outs), axis=0)
    return o.reshape(bsz, s, DIL_WIDTH).astype(h.dtype) @ out_w


def diff_mixer(h, qkv_w, lam, subln_g, out_w, table, layer_idx):
    bsz, s, _ = h.shape
    H, E, V = DIFF_HEADS, DIFF_HEAD_DIM, DIFF_V_DIM
    q, k, v = jnp.split(h @ qkv_w, 3, axis=-1)
    q = q.reshape(bsz, s, H, 2, E).astype(jnp.float32) / math.sqrt(E)
    k = k.reshape(bsz, s, H, 2, E).astype(jnp.float32)
    v = v.reshape(bsz, s, H, V).astype(jnp.float32)
    lam_init = 0.8 - 0.6 * math.exp(-0.3 * layer_idx)
    lf = lam.astype(jnp.float32)
    lam_full = jnp.exp(jnp.sum(lf[0] * lf[1])) - jnp.exp(jnp.sum(lf[2] * lf[3])) + lam_init
    nq = s // DIFF_QBLK
    q_blocks = jnp.swapaxes(q.reshape(bsz, nq, DIFF_QBLK, H, 2, E), 0, 1)
    starts = jnp.arange(nq) * DIFF_QBLK
    key_pos = jnp.arange(s)

    def block(args):
        qb, st = args
        rel = key_pos[None, :] - (st + jnp.arange(DIFF_QBLK))[:, None]
        bias = rel_bias_lookup(table, rel).reshape(DIFF_QBLK, s, H, 2).transpose(2, 3, 0, 1)
        logits = jnp.einsum('bqhie,bkhie->bhiqk', qb, k) + bias
        a = jax.nn.softmax(logits, axis=-1)
        attn = a[:, :, 0] - lam_full * a[:, :, 1]
        return jnp.einsum('bhqk,bkhe->bqhe', attn, v)

    o = lax.map(block, (q_blocks, starts))
    o = jnp.swapaxes(o, 0, 1).reshape(bsz, s, H, V)
    o = rmsnorm(o, subln_g) * (1.0 - lam_init)
    return o.reshape(bsz, s, D_MODEL).astype(h.dtype) @ out_w


def sqrelu_mlp(h, w1, w2):
    return jnp.square(jax.nn.relu(h @ w1)) @ w2


def setup_inputs(seed: int = 0) -> dict:
    key = jax.random.key(seed)
    ks = jax.random.split(key, 24)
    nrm = lambda k, shape, scale: jax.random.normal(k, shape, jnp.float32) * scale
    dt0 = jnp.exp(jax.random.uniform(ks[9], (N_SSD_LAYERS, 2, SSD_HEADS), jnp.float32,
                                     minval=math.log(1e-3), maxval=math.log(1e-1)))
    return {
        'x': nrm(ks[0], (BATCH, SEQ, D_MODEL), 1.0),
        'rel_bias': nrm(ks[1], (REL_BUCKETS, REL_BIAS_HEADS), 0.2),
        'norm_mix_g': 1.0 + nrm(ks[2], (DEPTH, D_MODEL), 0.02),
        'norm_mlp_g': 1.0 + nrm(ks[3], (DEPTH, D_MODEL), 0.02),
        'mlp_w1': nrm(ks[4], (DEPTH, D_MODEL, D_FF), D_MODEL ** -0.5),
        'mlp_w2': nrm(ks[5], (DEPTH, D_FF, D_MODEL), D_FF ** -0.5),
        'ssd_in_w': nrm(ks[6], (N_SSD_LAYERS, D_MODEL, SSD_IN_COLS), D_MODEL ** -0.5),
        'ssd_conv_w': nrm(ks[7], (N_SSD_LAYERS, SSD_CONV_WIDTH, SSD_CONV_CH), SSD_CONV_WIDTH ** -0.5),
        'ssd_conv_b': nrm(ks[8], (N_SSD_LAYERS, SSD_CONV_CH), 0.02),
        'ssd_dt_bias': dt0 + jnp.log(-jnp.expm1(-dt0)),
        'ssd_a_log': jnp.log(jax.random.uniform(ks[10], (N_SSD_LAYERS, 2, SSD_HEADS), jnp.float32,
                                                minval=1.0, maxval=16.0)),
        'ssd_d': 1.0 + nrm(ks[11], (N_SSD_LAYERS, SSD_HEADS), 0.1),
        'ssd_norm_g': 1.0 + nrm(ks[12], (N_SSD_LAYERS, SSD_D_INNER), 0.02),
        'ssd_out_w': nrm(ks[13], (N_SSD_LAYERS, SSD_D_INNER, D_MODEL), SSD_D_INNER ** -0.5),
        'dil_qkv_w': nrm(ks[14], (N_DIL_LAYERS, D_MODEL, DIL_QKV_COLS), D_MODEL ** -0.5),
        'dil_out_w': nrm(ks[15], (N_DIL_LAYERS, DIL_WIDTH, D_MODEL), DIL_WIDTH ** -0.5),
        'diff_qkv_w': nrm(ks[16], (N_DIFF_LAYERS, D_MODEL, DIFF_QKV_COLS), D_MODEL ** -0.5),
        'diff_lambda': nrm(ks[17], (N_DIFF_LAYERS, 4, DIFF_HEAD_DIM), 0.1),
        'diff_subln_g': 1.0 + nrm(ks[18], (N_DIFF_LAYERS, DIFF_V_DIM), 0.02),
        'diff_out_w': nrm(ks[19], (N_DIFF_LAYERS, D_MODEL, D_MODEL), D_MODEL ** -0.5),
        'final_norm_g': 1.0 + nrm(ks[20], (D_MODEL,), 0.02),
    }


def reference(x, rel_bias, norm_mix_g, norm_mlp_g, mlp_w1, mlp_w2,
              ssd_in_w, ssd_conv_w, ssd_conv_b, ssd_dt_bias, ssd_a_log, ssd_d,
              ssd_norm_g, ssd_out_w, dil_qkv_w, dil_out_w,
              diff_qkv_w, diff_lambda, diff_subln_g, diff_out_w, final_norm_g):
    for i in range(DEPTH):
        kind = i % N_MIXERS
        j = i // N_MIXERS
        h = rmsnorm(x, norm_mix_g[i])
        if kind == 0:
            h = ssd_mixer(h, ssd_in_w[j], ssd_conv_w[j], ssd_conv_b[j], ssd_dt_bias[j],
                          ssd_a_log[j], ssd_d[j], ssd_norm_g[j], ssd_out_w[j])
        elif kind == 1:
            h = dilated_mixer(h, dil_qkv_w[j], dil_out_w[j], rel_bias)
        else:
            h = diff_mixer(h, diff_qkv_w[j], diff_lambda[j], diff_subln_g[j], diff_out_w[j],
                           rel_bias, i)
        x = x + h
        x = x + sqrelu_mlp(rmsnorm(x, norm_mlp_g[i]), mlp_w1[i], mlp_w2[i])
    return rmsnorm(x, final_norm_g)
```

```python
import functools
import math

import numpy as np
import jax
import jax.numpy as jnp
from jax import lax
from jax.experimental import pallas as pl
from jax.experimental.pallas import tpu as pltpu

F32 = jnp.float32
BF16 = jnp.bfloat16
HIGHEST = lax.Precision.HIGHEST

RMS_EPS = 1e-6
NEG_INF = -1e30

V7X_VMEM_BYTES = 64 * 1024 * 1024
VMEM_LIMIT_BYTES = V7X_VMEM_BYTES - 8 * 1024 * 1024
LANES = 128

SSD_HEAD_DIM = 64
SSD_GROUPS = 8
SSD_HEADS_PER_GROUP = 4
SSD_D_STATE = 128
SSD_CONV_WIDTH = 5
SSD_CHUNK = 128
SSD_GROUP_WIDTH = SSD_HEADS_PER_GROUP * SSD_HEAD_DIM

DIL_CONFIGS = ((128, 1), (512, 4), (2048, 16))
DIL_HEADS = 16
DIL_HEAD_DIM = 64

DIFF_HEADS = 8
DIFF_HEAD_DIM = 64
DIFF_V_DIM = 128
DIFF_TILE = 128

REL_BUCKETS = 32
REL_MAX_DIST = 1024
REL_BIAS_HEADS = 16

NT_DIMS = (((1,), (1,)), ((), ()))
TN_DIMS = (((0,), (0,)), ((), ()))


def _params(*semantics):
    return pltpu.CompilerParams(dimension_semantics=semantics,
                                vmem_limit_bytes=VMEM_LIMIT_BYTES)


def _rms(x, g):
    ms = jnp.mean(x * x, axis=-1, keepdims=True)
    return x * lax.rsqrt(ms + RMS_EPS) * g


def _silu(x):
    return x * (1.0 / (1.0 + jnp.exp(-x)))


def _softplus(x):
    return jnp.maximum(x, 0.0) + jnp.log1p(jnp.exp(-jnp.abs(x)))


def _mm_kernel(*refs, norm, res):
    refs = list(refs)
    x_ref, w_ref = refs[0], refs[1]
    pos = 2
    g_ref = r_ref = None
    if norm:
        g_ref = refs[pos]
        pos += 1
    if res:
        r_ref = refs[pos]
        pos += 1
    o_ref, xs_ref = refs[pos], refs[pos + 1]

    @pl.when(pl.program_id(1) == 0)
    def _():
        x = x_ref[...].astype(F32)
        if norm:
            x = _rms(x, g_ref[...])
        xs_ref[...] = x.astype(BF16)

    acc = jnp.dot(xs_ref[...], w_ref[...].astype(BF16), preferred_element_type=F32)
    if res:
        acc = r_ref[...] + acc
    o_ref[...] = acc.astype(o_ref.dtype)


def _mm(x, w, *, g=None, res=None, n_cols=None, out_dtype=F32, tm=1024, tn=512, name="mm"):
    t, k = x.shape
    n = w.shape[1] if n_cols is None else n_cols
    tn = min(tn, n)
    tm = min(tm, t)
    assert t % tm == 0 and n % tn == 0 and w.shape[0] == k
    in_specs = [pl.BlockSpec((tm, k), lambda i, j: (i, 0)),
                pl.BlockSpec((k, tn), lambda i, j: (0, j))]
    args = [x, w]
    if g is not None:
        in_specs.append(pl.BlockSpec((1, k), lambda i, j: (0, 0)))
        args.append(g.reshape(1, k))
    if res is not None:
        in_specs.append(pl.BlockSpec((tm, tn), lambda i, j: (i, j)))
        args.append(res)
    return pl.pallas_call(
        functools.partial(_mm_kernel, norm=g is not None, res=res is not None),
        out_shape=jax.ShapeDtypeStruct((t, n), out_dtype),
        grid=(t // tm, n // tn),
        in_specs=in_specs,
        out_specs=pl.BlockSpec((tm, tn), lambda i, j: (i, j)),
        scratch_shapes=[pltpu.VMEM((tm, k), BF16)],
        compiler_params=_params("parallel", "arbitrary"),
        name=name,
    )(*args)


def _mlp_kernel(*refs, final):
    if final:
        x_ref, g_ref, w1_ref, w2_ref, fg_ref, o_ref, xn_ref, acc_ref = refs
    else:
        x_ref, g_ref, w1_ref, w2_ref, o_ref, xn_ref, acc_ref = refs
        fg_ref = None
    f = pl.program_id(1)

    @pl.when(f == 0)
    def _():
        x = x_ref[...]
        xn_ref[...] = _rms(x, g_ref[...]).astype(BF16)
        acc_ref[...] = x

    h = jnp.dot(xn_ref[...], w1_ref[...].astype(BF16), preferred_element_type=F32)
    h = jnp.square(jnp.maximum(h, 0.0)).astype(BF16)
    acc_ref[...] += jnp.dot(h, w2_ref[...].astype(BF16), preferred_element_type=F32)

    @pl.when(f == pl.num_programs(1) - 1)
    def _():
        y = acc_ref[...]
        if final:
            y = _rms(y, fg_ref[...])
        o_ref[...] = y


def _mlp(x, g, w1, w2, final_g=None, *, tm=1024, tf=512):
    t, d = x.shape
    ff = w1.shape[1]
    assert t % tm == 0 and ff % tf == 0
    final = final_g is not None
    in_specs = [pl.BlockSpec((tm, d), lambda i, f: (i, 0)),
                pl.BlockSpec((1, d), lambda i, f: (0, 0)),
                pl.BlockSpec((d, tf), lambda i, f: (0, f)),
                pl.BlockSpec((tf, d), lambda i, f: (f, 0))]
    args = [x, g.reshape(1, d), w1, w2]
    if final:
        in_specs.append(pl.BlockSpec((1, d), lambda i, f: (0, 0)))
        args.append(final_g.reshape(1, d))
    return pl.pallas_call(
        functools.partial(_mlp_kernel, final=final),
        out_shape=jax.ShapeDtypeStruct((t, d), F32),
        grid=(t // tm, ff // tf),
        in_specs=in_specs,
        out_specs=pl.BlockSpec((tm, d), lambda i, f: (i, 0)),
        scratch_shapes=[pltpu.VMEM((tm, d), BF16), pltpu.VMEM((tm, d), F32)],
        compiler_params=_params("parallel", "arbitrary"),
        name="mlp",
    )(*args)


def _rel_bucket_np(rel):
    half = REL_BUCKETS // 2
    max_exact = half // 2
    rel = np.asarray(rel, np.int32)
    ret = np.where(rel > 0, half, 0)
    n = np.abs(rel)
    nf = np.maximum(n, 1).astype(np.float32)
    scale = np.float32(math.log(REL_MAX_DIST / max_exact))
    large = max_exact + (np.log(nf / np.float32(max_exact)) / scale
                         * np.float32(half - max_exact)).astype(np.int32)
    large = np.minimum(large, half - 1)
    return (ret + np.where(n < max_exact, n, large)).astype(np.int32)


def _bias_tiles_kernel(tbl_ref, idx_ref, o_ref):
    h = pl.program_id(0)
    idx = idx_ref[...]
    acc = jnp.zeros(idx.shape, F32)
    for b in range(REL_BUCKETS):
        acc = jnp.where(idx == b, tbl_ref[b, h], acc)
    o_ref[...] = acc


def _bias_tiles(table, bucket_idx):
    n, r, c = bucket_idx.shape
    return pl.pallas_call(
        _bias_tiles_kernel,
        out_shape=jax.ShapeDtypeStruct((REL_BIAS_HEADS, n, r, c), F32),
        grid=(REL_BIAS_HEADS,),
        in_specs=[pl.BlockSpec(memory_space=pltpu.SMEM),
                  pl.BlockSpec((n, r, c), lambda h: (0, 0, 0))],
        out_specs=pl.BlockSpec((None, n, r, c), lambda h: (h, 0, 0, 0)),
        compiler_params=_params("parallel"),
        name="bias_tiles",
    )(table.astype(F32), jnp.asarray(bucket_idx))


def _conv_kernel(x_ref, w_ref, b_ref, o_ref, xp_ref, *, rows):
    s, tc = x_ref.shape
    pad = 8
    half = SSD_CONV_WIDTH // 2
    xp_ref[0:pad, :] = jnp.zeros((pad, tc), F32)
    xp_ref[s + pad:s + 2 * pad, :] = jnp.zeros((pad, tc), F32)
    xp_ref[pad:s + pad, :] = x_ref[...].astype(F32)
    w = w_ref[...]
    b = b_ref[...]
    for ch in range(s // rows):
        r0 = ch * rows
        acc = b
        for k in range(SSD_CONV_WIDTH):
            start = r0 + pad - half + k
            acc = acc + w[k:k + 1, :] * xp_ref[start:start + rows, :]
        o_ref[r0:r0 + rows, :] = _silu(acc).astype(o_ref.dtype)


def _ssd_conv(zx, conv_w, conv_b, *, d_inner, tc=256, rows=512):
    bsz, s, _ = zx.shape
    c = conv_w.shape[1]
    off = d_inner // tc
    return pl.pallas_call(
        functools.partial(_conv_kernel, rows=rows),
        out_shape=jax.ShapeDtypeStruct((bsz, s, c), BF16),
        grid=(bsz, c // tc),
        in_specs=[pl.BlockSpec((None, s, tc), lambda b, j: (b, 0, off + j)),
                  pl.BlockSpec((SSD_CONV_WIDTH, tc), lambda b, j: (0, j)),
                  pl.BlockSpec((1, tc), lambda b, j: (0, j))],
        out_specs=pl.BlockSpec((None, s, tc), lambda b, j: (b, 0, j)),
        scratch_shapes=[pltpu.VMEM((s + 16, tc), F32)],
        compiler_params=_params("parallel", "parallel"),
        name="ssd_conv",
    )(zx, conv_w, conv_b.reshape(1, c))


def _expand_heads(v, head_of_lane):
    r_heads = v.shape[1]
    out = v[:, r_heads - 1:r_heads]
    for r in range(r_heads - 2, -1, -1):
        out = jnp.where(head_of_lane == r, v[:, r:r + 1], out)
    return out


def _ssd_direction(x_ref, b_ref, c_ref, dc_ref, dr_ref, bias_c, alog_c, bias_r, alog_r,
                   s_ref, rev):
    L = SSD_CHUNK
    R = SSD_HEADS_PER_GROUP
    row = lax.broadcasted_iota(jnp.int32, (L, L), 0)
    col = lax.broadcasted_iota(jnp.int32, (L, L), 1)
    head_of_lane = lax.broadcasted_iota(jnp.int32, (1, SSD_GROUP_WIDTH), 1) // SSD_HEAD_DIM

    x = x_ref[...].astype(F32)
    bm = b_ref[...]
    cm = c_ref[...]
    dtc = _softplus(dc_ref[...] + bias_c)
    dtr = _softplus(dr_ref[...] + bias_r)
    adt_c = dtc * (-jnp.exp(alog_c))
    adt_r = dtr * (-jnp.exp(alog_r))
    tri = (col >= row) if rev else (col <= row)
    tri_t = (row >= col) if rev else (row <= col)
    acs_c = jnp.dot(tri.astype(F32), adt_c, precision=HIGHEST, preferred_element_type=F32)
    acs_r = jnp.dot(adt_r, tri_t.astype(F32), precision=HIGHEST, preferred_element_type=F32)

    cb = lax.dot_general(cm, bm, NT_DIMS, preferred_element_type=F32)
    ws = []
    for r in range(R):
        seg = acs_c[:, r:r + 1] - acs_r[r:r + 1, :]
        lmat = jnp.exp(jnp.where(tri, seg, -jnp.inf))
        ws.append((cb * lmat).astype(BF16))
    wcat = jnp.concatenate(ws, axis=1)
    xdt = x * _expand_heads(dtc, head_of_lane)
    xstack = jnp.concatenate(
        [jnp.where(head_of_lane == r, xdt, 0.0).astype(BF16) for r in range(R)], axis=0)
    y = jnp.dot(wcat, xstack, preferred_element_type=F32)

    state = s_ref[...]
    y_off = jnp.dot(cm, state.astype(BF16), preferred_element_type=F32)
    y = y + y_off * _expand_heads(jnp.exp(acs_c), head_of_lane)

    last = acs_c[0:1, :] if rev else acs_c[L - 1:L, :]
    decay = jnp.exp(last - acs_c)
    xw = (xdt * _expand_heads(decay, head_of_lane)).astype(BF16)
    upd = lax.dot_general(bm, xw, TN_DIMS, preferred_element_type=F32)
    s_ref[...] = state * _expand_heads(jnp.exp(last), head_of_lane) + upd
    return y, x


def _ssd_scan_kernel(xf_ref, bf_ref, cf_ref, dcf_ref, drf_ref,
                     xb_ref, bb_ref, cb_ref, dcb_ref, drb_ref,
                     pc_ref, pr_ref, dsk_ref, yf_ref, yb_ref, sf_ref, sb_ref):
    @pl.when(pl.program_id(2) == 0)
    def _():
        sf_ref[...] = jnp.zeros(sf_ref.shape, F32)
        sb_ref[...] = jnp.zeros(sb_ref.shape, F32)

    pc = pc_ref[...]
    pr = pr_ref[...]
    y, x = _ssd_direction(xf_ref, bf_ref, cf_ref, dcf_ref, drf_ref,
                          pc[0:1, :], pc[1:2, :], pr[:, 0:1], pr[:, 1:2], sf_ref, False)
    yf_ref[...] = (y + dsk_ref[...] * x).astype(yf_ref.dtype)
    y, _ = _ssd_direction(xb_ref, bb_ref, cb_ref, dcb_ref, drb_ref,
                          pc[2:3, :], pc[3:4, :], pr[:, 2:3], pr[:, 3:4], sb_ref, True)
    yb_ref[...] = y.astype(yb_ref.dtype)


def _ssd_scan(xbc, dt_raw, dt_bias, a_log, d_skip):
    bsz, s, _ = xbc.shape
    G, R, P, N, L = SSD_GROUPS, SSD_HEADS_PER_GROUP, SSD_HEAD_DIM, SSD_D_STATE, SSD_CHUNK
    d_inner = G * R * P
    nc = s // L
    xw = R * P
    b_off = d_inner // N
    c_off = b_off + G
    dt5 = dt_raw.reshape(bsz, s, 2, G, R)
    dtc = jnp.transpose(dt5, (2, 0, 3, 1, 4))
    dtr = jnp.transpose(dt5, (2, 0, 3, 4, 1))
    pc = jnp.stack([dt_bias[0].reshape(G, R), a_log[0].reshape(G, R),
                    dt_bias[1].reshape(G, R), a_log[1].reshape(G, R)], axis=1).astype(F32)
    pr = jnp.transpose(pc, (0, 2, 1))
    dsk = jnp.repeat(d_skip.astype(F32).reshape(G, 1, R), P, axis=2)

    def fwd(c):
        return c

    def bwd(c):
        return nc - 1 - c

    def dir_specs(d, cm):
        return [
            pl.BlockSpec((None, L, xw), lambda b, g, c: (b, cm(c), g)),
            pl.BlockSpec((None, L, N), lambda b, g, c: (b, cm(c), b_off + g)),
            pl.BlockSpec((None, L, N), lambda b, g, c: (b, cm(c), c_off + g)),
            pl.BlockSpec((None, None, None, L, R), lambda b, g, c: (d, b, g, cm(c), 0)),
            pl.BlockSpec((None, None, None, R, L), lambda b, g, c: (d, b, g, 0, cm(c))),
        ]

    in_specs = dir_specs(0, fwd) + dir_specs(1, bwd) + [
        pl.BlockSpec((None, 4, R), lambda b, g, c: (g, 0, 0)),
        pl.BlockSpec((None, R, 4), lambda b, g, c: (g, 0, 0)),
        pl.BlockSpec((None, 1, xw), lambda b, g, c: (g, 0, 0)),
    ]
    out_sd = jax.ShapeDtypeStruct((bsz, s, d_inner), BF16)
    return pl.pallas_call(
        _ssd_scan_kernel,
        out_shape=(out_sd, out_sd),
        grid=(bsz, G, nc),
        in_specs=in_specs,
        out_specs=(pl.BlockSpec((None, L, xw), lambda b, g, c: (b, c, g)),
                   pl.BlockSpec((None, L, xw), lambda b, g, c: (b, nc - 1 - c, g))),
        scratch_shapes=[pltpu.VMEM((N, xw), F32), pltpu.VMEM((N, xw), F32)],
        compiler_params=_params("parallel", "parallel", "arbitrary"),
        name="ssd_scan",
    )(xbc, xbc, xbc, dtc, dtr, xbc, xbc, xbc, dtc, dtr, pc, pr, dsk)


def _ssd_out_kernel(yf_ref, yb_ref, z_ref, g_ref, w_ref, x_ref, o_ref, wbf_ref):
    @pl.when(pl.program_id(0) == 0)
    def _():
        wbf_ref[...] = w_ref[...].astype(BF16)

    y = (yf_ref[...].astype(F32) + yb_ref[...].astype(F32)) * _silu(z_ref[...].astype(F32))
    yn = _rms(y, g_ref[...]).astype(BF16)
    o_ref[...] = x_ref[...] + jnp.dot(yn, wbf_ref[...], preferred_element_type=F32)


def _ssd_out(yf, yb, zx, norm_g, out_w, x, *, tm=512):
    t, d_inner = yf.shape
    d = out_w.shape[1]
    assert t % tm == 0
    return pl.pallas_call(
        _ssd_out_kernel,
        out_shape=jax.ShapeDtypeStruct((t, d), F32),
        grid=(t // tm,),
        in_specs=[pl.BlockSpec((tm, d_inner), lambda i: (i, 0)),
                  pl.BlockSpec((tm, d_inner), lambda i: (i, 0)),
                  pl.BlockSpec((tm, d_inner), lambda i: (i, 0)),
                  pl.BlockSpec((1, d_inner), lambda i: (0, 0)),
                  pl.BlockSpec((d_inner, d), lambda i: (0, 0)),
                  pl.BlockSpec((tm, d), lambda i: (i, 0))],
        out_specs=pl.BlockSpec((tm, d), lambda i: (i, 0)),
        scratch_shapes=[pltpu.VMEM((d_inner, d), BF16)],
        compiler_params=_params("arbitrary"),
        name="ssd_out",
    )(yf, yb, zx, norm_g.reshape(1, d_inner), out_w, x)


def _ssd_layer(x, norm_g, in_w, conv_w, conv_b, dt_bias, a_log, d_skip, ssd_norm_g, out_w, bsz, s):
    d_inner = ssd_norm_g.shape[0]
    conv_ch = conv_w.shape[1]
    zx_cols = d_inner + conv_ch
    zx = _mm(x, in_w, g=norm_g, n_cols=zx_cols, out_dtype=BF16, name="ssd_in")
    dt_raw = _mm(x, in_w[:, zx_cols:], g=norm_g, out_dtype=F32, name="ssd_dt")
    xbc = _ssd_conv(zx.reshape(bsz, s, zx_cols), conv_w, conv_b, d_inner=d_inner)
    yf, yb = _ssd_scan(xbc, dt_raw.reshape(bsz, s, -1), dt_bias, a_log, d_skip)
    t = bsz * s
    return _ssd_out(yf.reshape(t, d_inner), yb.reshape(t, d_inner), zx, ssd_norm_g, out_w, x)


def _dil_attn_kernel(q_ref, kp_ref, kc_ref, kn_ref, vp_ref, vc_ref, vn_ref, bias_ref,
                     o_ref, lse_ref, *, nb, blk):
    j = pl.program_id(2)
    qi = lax.broadcasted_iota(jnp.int32, (blk, 3 * blk), 0)
    ci = lax.broadcasted_iota(jnp.int32, (blk, 3 * blk), 1)
    delta = ci - blk - qi
    lo = jnp.where(j > 0, 0, blk)
    hi = jnp.where(j < nb - 1, 3 * blk, 2 * blk)
    valid = (jnp.abs(delta) <= blk) & (ci >= lo) & (ci < hi)
    scale = 1.0 / math.sqrt(DIL_HEAD_DIM)
    for h in range(DIL_HEADS):
        sl = slice(h * DIL_HEAD_DIM, (h + 1) * DIL_HEAD_DIM)
        q = q_ref[:, sl]
        k = jnp.concatenate([kp_ref[:, sl], kc_ref[:, sl], kn_ref[:, sl]], axis=0)
        v = jnp.concatenate([vp_ref[:, sl], vc_ref[:, sl], vn_ref[:, sl]], axis=0)
        s = lax.dot_general(q, k, NT_DIMS, preferred_element_type=F32) * scale + bias_ref[h]
        s = jnp.where(valid, s, NEG_INF)
        m = jnp.max(s, axis=-1, keepdims=True)
        p = jnp.exp(s - m)
        den = jnp.sum(p, axis=-1, keepdims=True)
        o = jnp.dot(p.astype(BF16), v, preferred_element_type=F32) / den
        o_ref[:, sl] = o.astype(o_ref.dtype)
        lse_ref[:, h:h + 1] = m + jnp.log(den)


def _dil_group(qkv, bias, gi, window, dilation, bsz, s):
    n_groups = len(DIL_CONFIGS)
    width = DIL_HEADS * DIL_HEAD_DIM
    d = dilation
    n = s // d
    blk = window // (2 * d)
    assert n % blk == 0 and blk % 8 == 0
    nb = n // blk
    per_pos = 3 * n_groups
    qkv3 = qkv.reshape(bsz, n, d * per_pos * width)

    def spec(which, shift):
        def index(b, r, j):
            jj = jnp.clip(j + shift, 0, nb - 1)
            return (b, jj, r * per_pos + which * n_groups + gi)
        return pl.BlockSpec((None, blk, width), index)

    o, lse = pl.pallas_call(
        functools.partial(_dil_attn_kernel, nb=nb, blk=blk),
        out_shape=(jax.ShapeDtypeStruct((bsz, n, d * width), BF16),
                   jax.ShapeDtypeStruct((bsz, d, nb, blk, DIL_HEADS), F32)),
        grid=(bsz, d, nb),
        in_specs=[spec(0, 0), spec(1, -1), spec(1, 0), spec(1, 1),
                  spec(2, -1), spec(2, 0), spec(2, 1),
                  pl.BlockSpec((DIL_HEADS, blk, 3 * blk), lambda b, r, j: (0, 0, 0))],
        out_specs=(pl.BlockSpec((None, blk, width), lambda b, r, j: (b, j, r)),
                   pl.BlockSpec((None, None, None, blk, DIL_HEADS), lambda b, r, j: (b, r, j, 0, 0))),
        compiler_params=_params("parallel", "parallel", "arbitrary"),
        name=f"dil_attn_d{d}",
    )(qkv3, qkv3, qkv3, qkv3, qkv3, qkv3, qkv3, bias)
    lse = jnp.transpose(lse.reshape(bsz, d, n, DIL_HEADS), (0, 2, 1, 3))
    return o.reshape(bsz * s, width), lse.reshape(bsz * s, DIL_HEADS)


def _dil_out_kernel(o0_ref, o1_ref, o2_ref, l0_ref, l1_ref, l2_ref, w_ref, x_ref, o_ref, wbf_ref):
    @pl.when(pl.program_id(0) == 0)
    def _():
        wbf_ref[...] = w_ref[...].astype(BF16)

    width = o0_ref.shape[1]
    lses = [l0_ref[...], l1_ref[...], l2_ref[...]]
    m = jnp.maximum(jnp.maximum(lses[0], lses[1]), lses[2])
    es = [jnp.exp(l - m) for l in lses]
    inv = 1.0 / (es[0] + es[1] + es[2])
    head_row = lax.broadcasted_iota(jnp.int32, (DIL_HEADS, width), 0)
    head_col = lax.broadcasted_iota(jnp.int32, (DIL_HEADS, width), 1) // DIL_HEAD_DIM
    expand = (head_row == head_col).astype(F32)
    y = None
    for e, o_ref_g in zip(es, (o0_ref, o1_ref, o2_ref)):
        wts = jnp.dot(e * inv, expand, precision=HIGHEST, preferred_element_type=F32)
        term = wts * o_ref_g[...].astype(F32)
        y = term if y is None else y + term
    o_ref[...] = x_ref[...] + jnp.dot(y.astype(BF16), wbf_ref[...], preferred_element_type=F32)


def _dil_out(os_, lses, out_w, x, *, tm=512):
    t, width = os_[0].shape
    d = out_w.shape[1]
    row = lambda i: (i, 0)
    return pl.pallas_call(
        _dil_out_kernel,
        out_shape=jax.ShapeDtypeStruct((t, d), F32),
        grid=(t // tm,),
        in_specs=[pl.BlockSpec((tm, width), row)] * 3 + [pl.BlockSpec((tm, DIL_HEADS), row)] * 3
                 + [pl.BlockSpec((width, d), lambda i: (0, 0)), pl.BlockSpec((tm, d), row)],
        out_specs=pl.BlockSpec((tm, d), row),
        scratch_shapes=[pltpu.VMEM((width, d), BF16)],
        compiler_params=_params("arbitrary"),
        name="dil_out",
    )(*os_, *lses, out_w, x)


def _dil_layer(x, norm_g, qkv_w, out_w, table, bsz, s):
    qkv = _mm(x, qkv_w, g=norm_g, out_dtype=BF16, name="dil_qkv")
    os_, lses = [], []
    for gi, (win, dil) in enumerate(DIL_CONFIGS):
        blk = win // (2 * dil)
        delta = np.arange(3 * blk)[None, :] - blk - np.arange(blk)[:, None]
        bias = _bias_tiles(table, _rel_bucket_np(delta * dil)[None])[:, 0]
        o, lse = _dil_group(qkv, bias, gi, win, dil, bsz, s)
        os_.append(o)
        lses.append(lse)
    return _dil_out(os_, lses, out_w, x)


DIFF_BANK_REACH = 6


def _diff_attn_kernel(q_ref, k_ref, v_ref, bank_ref, lam_ref, g_ref, o_ref,
                      m_ref, l_ref, acc_ref, *, tq, tk, seq, lam_init):
    qi = pl.program_id(2)
    tile = DIFF_TILE
    lane = lax.broadcasted_iota(jnp.int32, (tq, 2 * DIFF_HEAD_DIM), 1)
    q = (q_ref[...].astype(F32) * (1.0 / math.sqrt(DIFF_HEAD_DIM))).astype(BF16)
    zero = jnp.zeros_like(q)
    qs = (jnp.where(lane < DIFF_HEAD_DIM, q, zero), jnp.where(lane >= DIFF_HEAD_DIM, q, zero))
    m_ref[...] = jnp.full(m_ref.shape, -jnp.inf, F32)
    l_ref[...] = jnp.zeros(l_ref.shape, F32)
    acc_ref[...] = jnp.zeros(acc_ref.shape, F32)

    def body(kc, carry):
        start = pl.multiple_of(kc * tk, tk)
        k = k_ref[pl.ds(start, tk), :]
        v = v_ref[pl.ds(start, tk), :]
        for i in range(2):
            s = lax.dot_general(qs[i], k, NT_DIMS, preferred_element_type=F32)
            rows = []
            for a in range(tq // tile):
                cols = []
                for b in range(tk // tile):
                    dlt = (kc * (tk // tile) + b) - (qi * (tq // tile) + a)
                    idx = jnp.clip(dlt, -DIFF_BANK_REACH, DIFF_BANK_REACH) + DIFF_BANK_REACH
                    cols.append(s[a * tile:(a + 1) * tile, b * tile:(b + 1) * tile]
                                + bank_ref[i, idx])
                rows.append(jnp.concatenate(cols, axis=1))
            s = jnp.concatenate(rows, axis=0)
            m_old = m_ref[i]
            m_new = jnp.maximum(m_old, jnp.max(s, axis=-1, keepdims=True))
            alpha = jnp.exp(m_old - m_new)
            p = jnp.exp(s - m_new)
            l_ref[i] = alpha * l_ref[i] + jnp.sum(p, axis=-1, keepdims=True)
            acc_ref[i] = alpha * acc_ref[i] + jnp.dot(p.astype(BF16), v, preferred_element_type=F32)
            m_ref[i] = m_new
        return carry

    lax.fori_loop(0, seq // tk, body, 0)

    lf = lam_ref[...]
    lam_full = (jnp.exp(jnp.sum(lf[0:1] * lf[1:2], axis=-1, keepdims=True))
                - jnp.exp(jnp.sum(lf[2:3] * lf[3:4], axis=-1, keepdims=True)) + lam_init)
    o = acc_ref[0] / l_ref[0] - lam_full * (acc_ref[1] / l_ref[1])
    o = _rms(o, g_ref[...]) * (1.0 - lam_init)
    o_ref[...] = o.astype(o_ref.dtype)


def _diff_attn(qkv, bank, lam, subln_g, lam_init, bsz, s, *, tq=256, tk=512):
    hw = 2 * DIFF_HEAD_DIM
    H = DIFF_HEADS
    assert s % tq == 0 and s % tk == 0 and tq % DIFF_TILE == 0 and tk % DIFF_TILE == 0
    n_bank = bank.shape[1]
    return pl.pallas_call(
        functools.partial(_diff_attn_kernel, tq=tq, tk=tk, seq=s, lam_init=lam_init),
        out_shape=jax.ShapeDtypeStruct((bsz, s, H * DIFF_V_DIM), BF16),
        grid=(bsz, H, s // tq),
        in_specs=[pl.BlockSpec((None, tq, hw), lambda b, h, i: (b, i, h)),
                  pl.BlockSpec((None, s, hw), lambda b, h, i: (b, 0, H + h)),
                  pl.BlockSpec((None, s, DIFF_V_DIM), lambda b, h, i: (b, 0, 2 * H + h)),
                  pl.BlockSpec((2, n_bank, DIFF_TILE, DIFF_TILE), lambda b, h, i: (h, 0, 0, 0)),
                  pl.BlockSpec(lam.shape, lambda b, h, i: (0, 0)),
                  pl.BlockSpec((1, DIFF_V_DIM), lambda b, h, i: (0, 0))],
        out_specs=pl.BlockSpec((None, tq, DIFF_V_DIM), lambda b, h, i: (b, i, h)),
        scratch_shapes=[pltpu.VMEM((2, tq, 1), F32), pltpu.VMEM((2, tq, 1), F32),
                        pltpu.VMEM((2, tq, DIFF_V_DIM), F32)],
        compiler_params=_params("parallel", "parallel", "arbitrary"),
        name="diff_attn",
    )(qkv, qkv, qkv, bank, lam.astype(F32), subln_g.reshape(1, DIFF_V_DIM).astype(F32))


def _diff_bank_buckets():
    reach = DIFF_BANK_REACH
    local = np.arange(DIFF_TILE)[None, :] - np.arange(DIFF_TILE)[:, None]
    tiles = np.stack([_rel_bucket_np(local + (t - reach) * DIFF_TILE) for t in range(2 * reach + 1)])
    assert (tiles[0] == tiles[0, 0, 0]).all() and (tiles[-1] == tiles[-1, 0, 0]).all()
    return tiles


def _diff_layer(x, norm_g, qkv_w, lam, subln_g, out_w, table, layer_idx, bsz, s):
    d = x.shape[1]
    lam_init = 0.8 - 0.6 * math.exp(-0.3 * layer_idx)
    qkv = _mm(x, qkv_w, g=norm_g, out_dtype=BF16, name="diff_qkv")
    bank = _bias_tiles(table, _diff_bank_buckets())
    o = _diff_attn(qkv.reshape(bsz, s, 3 * d), bank, lam, subln_g, lam_init, bsz, s)
    return _mm(o.reshape(bsz * s, d), out_w, res=x, name="diff_out")


def kernel(x, rel_bias, norm_mix_g, norm_mlp_g, mlp_w1, mlp_w2, ssd_in_w, ssd_conv_w, ssd_conv_b, ssd_dt_bias, ssd_a_log, ssd_d, ssd_norm_g, ssd_out_w, dil_qkv_w, dil_out_w, diff_qkv_w, diff_lambda, diff_subln_g, diff_out_w, final_norm_g):
    bsz, s, d = x.shape
    depth = norm_mix_g.shape[0]
    h = x.reshape(bsz * s, d)
    for i in range(depth):
        kind, j = i % 3, i // 3
        if kind == 0:
            h = _ssd_layer(h, norm_mix_g[i], ssd_in_w[j], ssd_conv_w[j], ssd_conv_b[j],
                           ssd_dt_bias[j], ssd_a_log[j], ssd_d[j], ssd_norm_g[j], ssd_out_w[j], bsz, s)
        elif kind == 1:
            h = _dil_layer(h, norm_mix_g[i], dil_qkv_w[j], dil_out_w[j], rel_bias, bsz, s)
        else:
            h = _diff_layer(h, norm_mix_g[i], diff_qkv_w[j], diff_lambda[j], diff_subln_g[j],
                            diff_out_w[j], rel_bias, i, bsz, s)
        h = _mlp(h, norm_mlp_g[i], mlp_w1[i], mlp_w2[i],
                 final_norm_g if i == depth - 1 else None)
    return h.reshape(bsz, s, d)
```

```python
import functools
import math

import numpy as np
import jax
import jax.numpy as jnp
from jax import lax
from jax.experimental import pallas as pl
from jax.experimental.pallas import tpu as pltpu

F32 = jnp.float32
BF16 = jnp.bfloat16
HIGHEST = lax.Precision.HIGHEST

RMS_EPS = 1e-6
NEG_INF = -1e30
LOG2_E = math.log2(math.e)

V7X_VMEM_BYTES = 64 * 1024 * 1024
VMEM_LIMIT_BYTES = V7X_VMEM_BYTES - 8 * 1024 * 1024

SSD_HEAD_DIM = 64
SSD_GROUPS = 8
SSD_HEADS_PER_GROUP = 4
SSD_D_STATE = 128
SSD_CONV_WIDTH = 5
SSD_CHUNK = 128
SSD_GROUP_WIDTH = SSD_HEADS_PER_GROUP * SSD_HEAD_DIM

DIL_CONFIGS = ((128, 1), (512, 4), (2048, 16))
DIL_HEADS = 16
DIL_HEAD_DIM = 64

DIFF_HEADS = 8
DIFF_HEAD_DIM = 64
DIFF_V_DIM = 128
DIFF_TILE = 128
DIFF_BANK_REACH = 6
DIFF_ONES_ROWS = 16

REL_BUCKETS = 32
REL_MAX_DIST = 1024
REL_BIAS_HEADS = 16
MASKED_BUCKET = REL_BUCKETS

NT_DIMS = (((1,), (1,)), ((), ()))
TN_DIMS = (((0,), (0,)), ((), ()))


def _params(*semantics):
    return pltpu.CompilerParams(dimension_semantics=semantics,
                                vmem_limit_bytes=VMEM_LIMIT_BYTES)


def _rms(x, g):
    ms = jnp.mean(x * x, axis=-1, keepdims=True)
    return x * lax.rsqrt(ms + RMS_EPS) * g


def _silu(x):
    return x * (1.0 / (1.0 + jnp.exp(-x)))


def _softplus(x):
    return jnp.maximum(x, 0.0) + jnp.log1p(jnp.exp(-jnp.abs(x)))


def _residue_major_perm(pb, d, inverse=False):
    i = lax.broadcasted_iota(jnp.int32, (pb, pb), 0)
    j = lax.broadcasted_iota(jnp.int32, (pb, pb), 1)
    per = pb // d
    if inverse:
        i, j = j, i
    return (j == (i % per) * d + i // per).astype(BF16)


def _permute_rows(x, pb, d, inverse=False):
    if d == 1:
        return x
    pm = _residue_major_perm(pb, d, inverse)
    blocks = [jnp.dot(pm, x[r0:r0 + pb], preferred_element_type=F32).astype(BF16)
              for r0 in range(0, x.shape[0], pb)]
    return blocks[0] if len(blocks) == 1 else jnp.concatenate(blocks, axis=0)


def _mm_kernel(*refs, norm, res, perm):
    refs = list(refs)
    x_ref, w_ref = refs[0], refs[1]
    pos = 2
    g_ref = r_ref = None
    if norm:
        g_ref = refs[pos]
        pos += 1
    if res:
        r_ref = refs[pos]
        pos += 1
    o_ref, xs_ref = refs[pos], refs[pos + 1]

    @pl.when(pl.program_id(1) == 0)
    def _():
        x = x_ref[...].astype(F32)
        if norm:
            x = _rms(x, g_ref[...])
        x = x.astype(BF16)
        if perm is not None:
            x = _permute_rows(x, *perm)
        xs_ref[...] = x

    acc = jnp.dot(xs_ref[...], w_ref[...].astype(BF16), preferred_element_type=F32)
    if res:
        acc = r_ref[...] + acc
    o_ref[...] = acc.astype(o_ref.dtype)


def _mm(x, w, *, g=None, res=None, n_cols=None, w_col_block=None, perm=None, out_dtype=F32,
        tm=1024, tn=512, name="mm"):
    t, k = x.shape
    n = w.shape[1] if n_cols is None else n_cols
    tn = min(tn, n)
    tm = min(tm, t)
    assert t % tm == 0 and n % tn == 0 and w.shape[0] == k
    if w_col_block is None:
        w_col_block = lambda j: j
    if perm is not None:
        assert tm % perm[0] == 0 and perm[0] % perm[1] == 0
    in_specs = [pl.BlockSpec((tm, k), lambda i, j: (i, 0)),
                pl.BlockSpec((k, tn), lambda i, j: (0, w_col_block(j)))]
    args = [x, w]
    if g is not None:
        in_specs.append(pl.BlockSpec((1, k), lambda i, j: (0, 0)))
        args.append(g.reshape(1, k))
    if res is not None:
        in_specs.append(pl.BlockSpec((tm, tn), lambda i, j: (i, j)))
        args.append(res)
    return pl.pallas_call(
        functools.partial(_mm_kernel, norm=g is not None, res=res is not None, perm=perm),
        out_shape=jax.ShapeDtypeStruct((t, n), out_dtype),
        grid=(t // tm, n // tn),
        in_specs=in_specs,
        out_specs=pl.BlockSpec((tm, tn), lambda i, j: (i, j)),
        scratch_shapes=[pltpu.VMEM((tm, k), BF16)],
        compiler_params=_params("parallel", "arbitrary"),
        name=name,
    )(*args)


def _mlp_kernel(*refs, final):
    if final:
        x_ref, g_ref, w1_ref, w2_ref, fg_ref, o_ref, xn_ref, acc_ref = refs
    else:
        x_ref, g_ref, w1_ref, w2_ref, o_ref, xn_ref, acc_ref = refs
        fg_ref = None
    f = pl.program_id(1)

    @pl.when(f == 0)
    def _():
        x = x_ref[...]
        xn_ref[...] = _rms(x, g_ref[...]).astype(BF16)
        acc_ref[...] = x

    h = jnp.dot(xn_ref[...], w1_ref[...].astype(BF16), preferred_element_type=F32)
    h = jnp.square(jnp.maximum(h, 0.0)).astype(BF16)
    acc_ref[...] += jnp.dot(h, w2_ref[...].astype(BF16), preferred_element_type=F32)

    @pl.when(f == pl.num_programs(1) - 1)
    def _():
        y = acc_ref[...]
        if final:
            y = _rms(y, fg_ref[...])
        o_ref[...] = y


def _mlp(x, g, w1, w2, final_g=None, *, tm=1024, tf=512):
    t, d = x.shape
    ff = w1.shape[1]
    assert t % tm == 0 and ff % tf == 0
    final = final_g is not None
    in_specs = [pl.BlockSpec((tm, d), lambda i, f: (i, 0)),
                pl.BlockSpec((1, d), lambda i, f: (0, 0)),
                pl.BlockSpec((d, tf), lambda i, f: (0, f)),
                pl.BlockSpec((tf, d), lambda i, f: (f, 0))]
    args = [x, g.reshape(1, d), w1, w2]
    if final:
        in_specs.append(pl.BlockSpec((1, d), lambda i, f: (0, 0)))
        args.append(final_g.reshape(1, d))
    return pl.pallas_call(
        functools.partial(_mlp_kernel, final=final),
        out_shape=jax.ShapeDtypeStruct((t, d), F32),
        grid=(t // tm, ff // tf),
        in_specs=in_specs,
        out_specs=pl.BlockSpec((tm, d), lambda i, f: (i, 0)),
        scratch_shapes=[pltpu.VMEM((tm, d), BF16), pltpu.VMEM((tm, d), F32)],
        compiler_params=_params("parallel", "arbitrary"),
        name="mlp",
    )(*args)


def _rel_bucket_np(rel):
    half = REL_BUCKETS // 2
    max_exact = half // 2
    rel = np.asarray(rel, np.int32)
    ret = np.where(rel > 0, half, 0)
    n = np.abs(rel)
    nf = np.maximum(n, 1).astype(np.float32)
    scale = np.float32(math.log(REL_MAX_DIST / max_exact))
    large = max_exact + (np.log(nf / np.float32(max_exact)) / scale
                         * np.float32(half - max_exact)).astype(np.int32)
    large = np.minimum(large, half - 1)
    return (ret + np.where(n < max_exact, n, large)).astype(np.int32)


def _bias_tiles_kernel(tbl_ref, idx_ref, o_ref, *, scale):
    h = pl.program_id(0)
    idx = idx_ref[...]
    acc = jnp.zeros(idx.shape, F32)
    for b in range(REL_BUCKETS):
        acc = jnp.where(idx == b, tbl_ref[b, h] * scale, acc)
    o_ref[...] = jnp.where(idx == MASKED_BUCKET, NEG_INF, acc)


def _bias_tiles(table, bucket_idx, scale=1.0):
    n, r, c = bucket_idx.shape
    return pl.pallas_call(
        functools.partial(_bias_tiles_kernel, scale=scale),
        out_shape=jax.ShapeDtypeStruct((REL_BIAS_HEADS, n, r, c), F32),
        grid=(REL_BIAS_HEADS,),
        in_specs=[pl.BlockSpec(memory_space=pltpu.SMEM),
                  pl.BlockSpec((n, r, c), lambda h: (0, 0, 0))],
        out_specs=pl.BlockSpec((None, n, r, c), lambda h: (h, 0, 0, 0)),
        compiler_params=_params("parallel"),
        name="bias_tiles",
    )(table.astype(F32), jnp.asarray(bucket_idx))


def _conv_kernel(x_ref, w_ref, b_ref, o_ref, xp_ref, *, rows):
    s, tc = x_ref.shape
    pad = 8
    half = SSD_CONV_WIDTH // 2
    xp_ref[0:pad, :] = jnp.zeros((pad, tc), F32)
    xp_ref[s + pad:s + 2 * pad, :] = jnp.zeros((pad, tc), F32)
    xp_ref[pad:s + pad, :] = x_ref[...].astype(F32)
    w = w_ref[...]
    b = b_ref[...]
    for ch in range(s // rows):
        r0 = ch * rows
        acc = b
        for k in range(SSD_CONV_WIDTH):
            start = r0 + pad - half + k
            acc = acc + w[k:k + 1, :] * xp_ref[start:start + rows, :]
        o_ref[r0:r0 + rows, :] = _silu(acc).astype(o_ref.dtype)


def _ssd_conv(zx, conv_w, conv_b, *, d_inner, tc=256, rows=512):
    bsz, s, _ = zx.shape
    c = conv_w.shape[1]
    off = d_inner // tc
    return pl.pallas_call(
        functools.partial(_conv_kernel, rows=rows),
        out_shape=jax.ShapeDtypeStruct((bsz, s, c), BF16),
        grid=(bsz, c // tc),
        in_specs=[pl.BlockSpec((None, s, tc), lambda b, j: (b, 0, off + j)),
                  pl.BlockSpec((SSD_CONV_WIDTH, tc), lambda b, j: (0, j)),
                  pl.BlockSpec((1, tc), lambda b, j: (0, j))],
        out_specs=pl.BlockSpec((None, s, tc), lambda b, j: (b, 0, j)),
        scratch_shapes=[pltpu.VMEM((s + 16, tc), F32)],
        compiler_params=_params("parallel", "parallel"),
        name="ssd_conv",
    )(zx, conv_w, conv_b.reshape(1, c))


def _expand_heads(v, head_of_lane):
    r_heads = v.shape[1]
    out = v[:, r_heads - 1:r_heads]
    for r in range(r_heads - 2, -1, -1):
        out = jnp.where(head_of_lane == r, v[:, r:r + 1], out)
    return out


def _ssd_scan_kernel(xf_ref, bf_ref, cf_ref, dcf_ref, drf_ref,
                     xb_ref, bb_ref, cb_ref, dcb_ref, drb_ref,
                     pc_ref, pr_ref, dsk_ref, yf_ref, yb_ref, sf_ref, sb_ref):
    @pl.when(pl.program_id(2) == 0)
    def _():
        sf_ref[...] = jnp.zeros(sf_ref.shape, F32)
        sb_ref[...] = jnp.zeros(sb_ref.shape, F32)

    L, R, P = SSD_CHUNK, SSD_HEADS_PER_GROUP, SSD_HEAD_DIM
    row = lax.broadcasted_iota(jnp.int32, (L, L), 0)
    col = lax.broadcasted_iota(jnp.int32, (L, L), 1)
    head_of_lane = lax.broadcasted_iota(jnp.int32, (1, SSD_GROUP_WIDTH), 1) // P
    head_of_lane_x = lax.broadcasted_iota(jnp.int32, (L, SSD_GROUP_WIDTH), 1) // P
    pc = pc_ref[...]
    pr = pr_ref[...]
    dirs = (
        dict(x=xf_ref, b=bf_ref, c=cf_ref, dc=dcf_ref, dr=drf_ref, s=sf_ref, rev=False,
             bias_c=pc[0:1, :], alog_c=pc[1:2, :], bias_r=pr[:, 0:1], alog_r=pr[:, 1:2]),
        dict(x=xb_ref, b=bb_ref, c=cb_ref, dc=dcb_ref, dr=drb_ref, s=sb_ref, rev=True,
             bias_c=pc[2:3, :], alog_c=pc[3:4, :], bias_r=pr[:, 2:3], alog_r=pr[:, 3:4]),
    )

    for d in dirs:
        rev = d["rev"]
        d["tri"] = (col >= row) if rev else (col <= row)
        tri_t = (row >= col) if rev else (row <= col)
        dtc = _softplus(d["dc"][...] + d["bias_c"])
        d["dtr"] = _softplus(d["dr"][...] + d["bias_r"])
        adt_c = dtc * (-jnp.exp(d["alog_c"]))
        adt_r = d["dtr"] * (-jnp.exp(d["alog_r"]))
        d["acs_c"] = jnp.dot(d["tri"].astype(F32), adt_c, precision=HIGHEST,
                             preferred_element_type=F32)
        d["acs_r"] = jnp.dot(adt_r, tri_t.astype(F32), precision=HIGHEST,
                             preferred_element_type=F32)
        bm = d["b"][...]
        d["cm"] = d["c"][...]
        d["cb"] = lax.dot_general(d["cm"], bm, NT_DIMS, preferred_element_type=F32)
        d["bt"] = bm.astype(F32).T
        d["state"] = d["s"][...]
        d["y_off"] = jnp.dot(d["cm"], d["state"].astype(BF16), preferred_element_type=F32)
        xb16 = d["x"][...]
        d["xstack"] = jnp.concatenate(
            [jnp.where(head_of_lane_x == r, xb16, jnp.zeros_like(xb16)) for r in range(R)], axis=0)

    for d in dirs:
        acs_c, acs_r, dtr = d["acs_c"], d["acs_r"], d["dtr"]
        last_r = acs_r[:, 0:1] if d["rev"] else acs_r[:, L - 1:L]
        key_scale = dtr * jnp.exp(last_r - acs_r)
        ws, bts = [], []
        for r in range(R):
            seg = acs_c[:, r:r + 1] - acs_r[r:r + 1, :]
            lmat = jnp.exp(jnp.where(d["tri"], seg, -jnp.inf))
            ws.append((d["cb"] * lmat * dtr[r:r + 1, :]).astype(BF16))
            bts.append((d["bt"] * key_scale[r:r + 1, :]).astype(BF16))
        d["wcat"] = jnp.concatenate(ws, axis=1)
        d["btcat"] = jnp.concatenate(bts, axis=1)

    for d in dirs:
        d["y"] = jnp.dot(d["wcat"], d["xstack"], preferred_element_type=F32)
        d["upd"] = jnp.dot(d["btcat"], d["xstack"], preferred_element_type=F32)

    for d in dirs:
        acs_c = d["acs_c"]
        last_c = acs_c[0:1, :] if d["rev"] else acs_c[L - 1:L, :]
        y = d["y"] + d["y_off"] * _expand_heads(jnp.exp(acs_c), head_of_lane)
        d["s"][...] = d["state"] * _expand_heads(jnp.exp(last_c), head_of_lane) + d["upd"]
        if d["rev"]:
            yb_ref[...] = y.astype(yb_ref.dtype)
        else:
            yf_ref[...] = (y + dsk_ref[...] * d["x"][...].astype(F32)).astype(yf_ref.dtype)


def _ssd_scan(xbc, dt_raw, dt_bias, a_log, d_skip):
    bsz, s, _ = xbc.shape
    G, R, P, N, L = SSD_GROUPS, SSD_HEADS_PER_GROUP, SSD_HEAD_DIM, SSD_D_STATE, SSD_CHUNK
    d_inner = G * R * P
    nc = s // L
    xw = R * P
    b_off = d_inner // N
    c_off = b_off + G
    dt5 = dt_raw.reshape(bsz, s, 2, G, R)
    dtc = jnp.transpose(dt5, (2, 0, 3, 1, 4))
    dtr = jnp.transpose(dt5, (2, 0, 3, 4, 1))
    pc = jnp.stack([dt_bias[0].reshape(G, R), a_log[0].reshape(G, R),
                    dt_bias[1].reshape(G, R), a_log[1].reshape(G, R)], axis=1).astype(F32)
    pr = jnp.transpose(pc, (0, 2, 1))
    dsk = jnp.repeat(d_skip.astype(F32).reshape(G, 1, R), P, axis=2)

    def fwd(c):
        return c

    def bwd(c):
        return nc - 1 - c

    def dir_specs(d, cm):
        return [
            pl.BlockSpec((None, L, xw), lambda b, g, c: (b, cm(c), g)),
            pl.BlockSpec((None, L, N), lambda b, g, c: (b, cm(c), b_off + g)),
            pl.BlockSpec((None, L, N), lambda b, g, c: (b, cm(c), c_off + g)),
            pl.BlockSpec((None, None, None, L, R), lambda b, g, c: (d, b, g, cm(c), 0)),
            pl.BlockSpec((None, None, None, R, L), lambda b, g, c: (d, b, g, 0, cm(c))),
        ]

    in_specs = dir_specs(0, fwd) + dir_specs(1, bwd) + [
        pl.BlockSpec((None, 4, R), lambda b, g, c: (g, 0, 0)),
        pl.BlockSpec((None, R, 4), lambda b, g, c: (g, 0, 0)),
        pl.BlockSpec((None, 1, xw), lambda b, g, c: (g, 0, 0)),
    ]
    out_sd = jax.ShapeDtypeStruct((bsz, s, d_inner), BF16)
    return pl.pallas_call(
        _ssd_scan_kernel,
        out_shape=(out_sd, out_sd),
        grid=(bsz, G, nc),
        in_specs=in_specs,
        out_specs=(pl.BlockSpec((None, L, xw), lambda b, g, c: (b, c, g)),
                   pl.BlockSpec((None, L, xw), lambda b, g, c: (b, nc - 1 - c, g))),
        scratch_shapes=[pltpu.VMEM((N, xw), F32), pltpu.VMEM((N, xw), F32)],
        compiler_params=_params("parallel", "parallel", "arbitrary"),
        name="ssd_scan",
    )(xbc, xbc, xbc, dtc, dtr, xbc, xbc, xbc, dtc, dtr, pc, pr, dsk)


def _ssd_out_kernel(yf_ref, yb_ref, z_ref, g_ref, w_ref, x_ref, o_ref, wbf_ref):
    @pl.when(pl.program_id(0) == 0)
    def _():
        wbf_ref[...] = w_ref[...].astype(BF16)

    y = (yf_ref[...].astype(F32) + yb_ref[...].astype(F32)) * _silu(z_ref[...].astype(F32))
    yn = _rms(y, g_ref[...]).astype(BF16)
    o_ref[...] = x_ref[...] + jnp.dot(yn, wbf_ref[...], preferred_element_type=F32)


def _ssd_out(yf, yb, zx, norm_g, out_w, x, *, tm=512):
    t, d_inner = yf.shape
    d = out_w.shape[1]
    assert t % tm == 0
    return pl.pallas_call(
        _ssd_out_kernel,
        out_shape=jax.ShapeDtypeStruct((t, d), F32),
        grid=(t // tm,),
        in_specs=[pl.BlockSpec((tm, d_inner), lambda i: (i, 0)),
                  pl.BlockSpec((tm, d_inner), lambda i: (i, 0)),
                  pl.BlockSpec((tm, d_inner), lambda i: (i, 0)),
                  pl.BlockSpec((1, d_inner), lambda i: (0, 0)),
                  pl.BlockSpec((d_inner, d), lambda i: (0, 0)),
                  pl.BlockSpec((tm, d), lambda i: (i, 0))],
        out_specs=pl.BlockSpec((tm, d), lambda i: (i, 0)),
        scratch_shapes=[pltpu.VMEM((d_inner, d), BF16)],
        compiler_params=_params("arbitrary"),
        name="ssd_out",
    )(yf, yb, zx, norm_g.reshape(1, d_inner), out_w, x)


def _ssd_layer(x, norm_g, in_w, conv_w, conv_b, dt_bias, a_log, d_skip, ssd_norm_g, out_w, bsz, s):
    d_inner = ssd_norm_g.shape[0]
    conv_ch = conv_w.shape[1]
    zx_cols = d_inner + conv_ch
    zx = _mm(x, in_w, g=norm_g, n_cols=zx_cols, out_dtype=BF16, name="ssd_in")
    dt_raw = _mm(x, in_w[:, zx_cols:], g=norm_g, out_dtype=F32, name="ssd_dt")
    xbc = _ssd_conv(zx.reshape(bsz, s, zx_cols), conv_w, conv_b, d_inner=d_inner)
    yf, yb = _ssd_scan(xbc, dt_raw.reshape(bsz, s, -1), dt_bias, a_log, d_skip)
    t = bsz * s
    return _ssd_out(yf.reshape(t, d_inner), yb.reshape(t, d_inner), zx, ssd_norm_g, out_w, x)


def _dil_attn_kernel(q_ref, kp_ref, kc_ref, kn_ref, vp_ref, vc_ref, vn_ref, bias_ref,
                     o_ref, lse_ref, *, nb, blk):
    j = pl.program_id(1)
    hd = DIL_HEAD_DIM
    variant = (j == 0).astype(jnp.int32) + 2 * (j == nb - 1).astype(jnp.int32)
    lane = lax.broadcasted_iota(jnp.int32, (blk, 2 * hd), 1)
    scale = 1.0 / math.sqrt(hd)
    pairs = DIL_HEADS // 2
    sls = [slice(a * 2 * hd, (a + 1) * 2 * hd) for a in range(pairs)]
    scores = []
    for a, sl in enumerate(sls):
        q = (q_ref[:, sl].astype(F32) * scale).astype(BF16)
        zero = jnp.zeros_like(q)
        qbd = jnp.concatenate([jnp.where(lane < hd, q, zero), jnp.where(lane >= hd, q, zero)], axis=0)
        k = jnp.concatenate([kp_ref[:, sl], kc_ref[:, sl], kn_ref[:, sl]], axis=0)
        scores.append(lax.dot_general(k, qbd, NT_DIMS, preferred_element_type=F32)
                      + bias_ref[variant, a])
    probs = []
    for a, s in enumerate(scores):
        m = jnp.max(s, axis=0, keepdims=True)
        p = jnp.exp(s - m)
        den = jnp.sum(p, axis=0, keepdims=True)
        probs.append((p * (1.0 / den)).astype(BF16))
        lse_ref[a:a + 1, :] = m + jnp.log(den)
    for a, (sl, pn) in enumerate(zip(sls, probs)):
        v = jnp.concatenate([vp_ref[:, sl], vc_ref[:, sl], vn_ref[:, sl]], axis=0)
        r = lax.dot_general(pn, v, TN_DIMS, preferred_element_type=F32)
        o_ref[:, sl] = jnp.where(lane < hd, r[0:blk], r[blk:2 * blk]).astype(o_ref.dtype)


def _dil_bias(table, window, dilation):
    blk = window // (2 * dilation)
    delta = np.arange(3 * blk)[:, None] - blk - np.arange(blk)[None, :]
    bucket = _rel_bucket_np(delta * dilation)
    band = np.abs(delta) <= blk
    key_blk = np.arange(3 * blk)[:, None] // blk
    idx = []
    for variant in range(4):
        ok = band & ~((key_blk == 0) & bool(variant & 1)) & ~((key_blk == 2) & bool(variant & 2))
        idx.append(np.where(ok, bucket, MASKED_BUCKET))
    tiles = _bias_tiles(table, np.stack(idx).astype(np.int32))
    tiles = tiles.reshape(DIL_HEADS // 2, 2, 4, 3 * blk, blk)
    return jnp.transpose(tiles, (2, 0, 3, 1, 4)).reshape(4, DIL_HEADS // 2, 3 * blk, 2 * blk)


def _dil_group(qkv, bias, window, dilation, bsz, s):
    width = DIL_HEADS * DIL_HEAD_DIM
    d = dilation
    n = s // d
    blk = window // (2 * d)
    assert n % blk == 0 and blk % 8 == 0
    nb = n // blk
    pairs = DIL_HEADS // 2

    def spec(which, shift):
        def index(b, jj, r):
            jn = jnp.clip(jj + shift, 0, nb - 1)
            return ((b * nb + jn) * d + r, which)
        return pl.BlockSpec((blk, width), index)

    o, lse = pl.pallas_call(
        functools.partial(_dil_attn_kernel, nb=nb, blk=blk),
        out_shape=(jax.ShapeDtypeStruct((bsz * s, width), BF16),
                   jax.ShapeDtypeStruct((bsz, nb, d, pairs, 2 * blk), F32)),
        grid=(bsz, nb, d),
        in_specs=[spec(0, 0), spec(1, -1), spec(1, 0), spec(1, 1),
                  spec(2, -1), spec(2, 0), spec(2, 1),
                  pl.BlockSpec((4, pairs, 3 * blk, 2 * blk), lambda b, jj, r: (0, 0, 0, 0))],
        out_specs=(pl.BlockSpec((blk, width), lambda b, jj, r: ((b * nb + jj) * d + r, 0)),
                   pl.BlockSpec((None, None, None, pairs, 2 * blk), lambda b, jj, r: (b, jj, r, 0, 0))),
        compiler_params=_params("parallel", "parallel", "parallel"),
        name=f"dil_attn_d{d}",
    )(qkv, qkv, qkv, qkv, qkv, qkv, qkv, bias)
    lse = jnp.transpose(lse.reshape(bsz, nb, d, pairs, 2, blk), (0, 1, 5, 2, 3, 4))
    return o, lse.reshape(bsz * s, DIL_HEADS)


def _dil_out_kernel(o0_ref, o1_ref, o2_ref, l0_ref, l1_ref, l2_ref, w_ref, x_ref, o_ref, wbf_ref,
                    *, perms):
    @pl.when(pl.program_id(0) == 0)
    def _():
        wbf_ref[...] = w_ref[...].astype(BF16)

    width = o0_ref.shape[1]
    lses = [l0_ref[...], l1_ref[...], l2_ref[...]]
    m = jnp.maximum(jnp.maximum(lses[0], lses[1]), lses[2])
    es = [jnp.exp(l - m) for l in lses]
    inv = 1.0 / (es[0] + es[1] + es[2])
    head_row = lax.broadcasted_iota(jnp.int32, (DIL_HEADS, width), 0)
    head_col = lax.broadcasted_iota(jnp.int32, (DIL_HEADS, width), 1) // DIL_HEAD_DIM
    expand = (head_row == head_col).astype(BF16)
    y = None
    for e, o_ref_g, perm in zip(es, (o0_ref, o1_ref, o2_ref), perms):
        w_g = e * inv
        w_hi = w_g.astype(BF16)
        w_lo = (w_g - w_hi.astype(F32)).astype(BF16)
        wts = (jnp.dot(w_hi, expand, preferred_element_type=F32)
               + jnp.dot(w_lo, expand, preferred_element_type=F32))
        o_nat = _permute_rows(o_ref_g[...], *perm, inverse=True)
        term = wts * o_nat.astype(F32)
        y = term if y is None else y + term
    o_ref[...] = x_ref[...] + jnp.dot(y.astype(BF16), wbf_ref[...], preferred_element_type=F32)


def _dil_out(os_, lses, perms, out_w, x, *, tm=1024):
    t, width = os_[0].shape
    d = out_w.shape[1]
    assert t % tm == 0 and all(tm % pb == 0 for pb, _ in perms)
    row = lambda i: (i, 0)
    return pl.pallas_call(
        functools.partial(_dil_out_kernel, perms=perms),
        out_shape=jax.ShapeDtypeStruct((t, d), F32),
        grid=(t // tm,),
        in_specs=[pl.BlockSpec((tm, width), row)] * 3 + [pl.BlockSpec((tm, DIL_HEADS), row)] * 3
                 + [pl.BlockSpec((width, d), lambda i: (0, 0)), pl.BlockSpec((tm, d), row)],
        out_specs=pl.BlockSpec((tm, d), row),
        scratch_shapes=[pltpu.VMEM((width, d), BF16)],
        compiler_params=_params("arbitrary"),
        name="dil_out",
    )(*os_, *lses, out_w, x)


def _dil_layer(x, norm_g, qkv_w, out_w, table, bsz, s):
    n_groups = len(DIL_CONFIGS)
    width = DIL_HEADS * DIL_HEAD_DIM
    tn = 512
    per_w = width // tn
    os_, lses, perms = [], [], []
    for gi, (win, dil) in enumerate(DIL_CONFIGS):
        perm = ((win // (2 * dil)) * dil, dil)
        qkv = _mm(x, qkv_w, g=norm_g, n_cols=3 * width, perm=perm, out_dtype=BF16, tn=tn,
                  w_col_block=lambda j, gi=gi: ((j // per_w) * n_groups + gi) * per_w + j % per_w,
                  name=f"dil_qkv_d{dil}")
        o, lse = _dil_group(qkv, _dil_bias(table, win, dil), win, dil, bsz, s)
        os_.append(o)
        lses.append(lse)
        perms.append(perm)
    return _dil_out(os_, lses, tuple(perms), out_w, x)


def _diff_attn_kernel(q_ref, k_ref, v_ref, bank_ref, lam_ref, g_ref, o_ref,
                      vt_ref, qt_ref, s_ref, m_ref, acc_ref, *, tq, tk, seq, lam_init):
    qi = pl.program_id(2)
    tile = DIFF_TILE
    hd = DIFF_HEAD_DIM

    @pl.when(qi == 0)
    def _():
        for c in range(seq // tk):
            vt_ref[0:DIFF_V_DIM, c * tk:(c + 1) * tk] = (
                v_ref[c * tk:(c + 1) * tk, :].astype(F32).T.astype(BF16))
        vt_ref[DIFF_V_DIM:, :] = jnp.ones((DIFF_ONES_ROWS, seq), BF16)

    qt = (q_ref[...].astype(F32) * (LOG2_E / math.sqrt(hd))).T
    row = lax.broadcasted_iota(jnp.int32, qt.shape, 0)
    qt_ref[:, 0:tq] = jnp.where(row < hd, qt, 0.0).astype(BF16)
    qt_ref[:, tq:2 * tq] = jnp.where(row >= hd, qt, 0.0).astype(BF16)
    m_ref[...] = jnp.full(m_ref.shape, -jnp.inf, F32)
    acc_ref[...] = jnp.zeros(acc_ref.shape, F32)

    n_chunks = seq // tk

    def scores(kc, slot):
        start = pl.multiple_of(kc * tk, tk)
        s_ref[slot] = jnp.dot(k_ref[pl.ds(start, tk), :], qt_ref[...],
                              preferred_element_type=F32)

    def consume(kc, slot):
        start = pl.multiple_of(kc * tk, tk)
        vt = vt_ref[:, pl.ds(start, tk)]
        for i in range(2):
            rows = []
            for b in range(tk // tile):
                cols = []
                for a in range(tq // tile):
                    dlt = (kc * (tk // tile) + b) - (qi * (tq // tile) + a)
                    idx = jnp.clip(dlt, -DIFF_BANK_REACH, DIFF_BANK_REACH) + DIFF_BANK_REACH
                    lo = i * tq + a * tile
                    cols.append(s_ref[slot, b * tile:(b + 1) * tile, lo:lo + tile]
                                + bank_ref[i, idx])
                rows.append(jnp.concatenate(cols, axis=1))
            s = jnp.concatenate(rows, axis=0)
            m_old = m_ref[i]
            m_new = jnp.maximum(m_old, jnp.max(s, axis=0, keepdims=True))
            alpha = jnp.exp2(m_old - m_new)
            p = jnp.exp2(s - m_new)
            acc_ref[i] = alpha * acc_ref[i] + jnp.dot(vt, p.astype(BF16), preferred_element_type=F32)
            m_ref[i] = m_new

    scores(0, 0)

    def body(j, carry):
        for slot in range(2):
            kc = 2 * j + slot
            scores(jnp.minimum(kc + 1, n_chunks - 1), 1 - slot)
            consume(kc, slot)
        return carry

    lax.fori_loop(0, n_chunks // 2, body, 0)

    lf = lam_ref[...]
    lam_full = (jnp.exp(jnp.sum(lf[0:1] * lf[1:2], axis=-1, keepdims=True))
                - jnp.exp(jnp.sum(lf[2:3] * lf[3:4], axis=-1, keepdims=True)) + lam_init)
    nv = DIFF_V_DIM
    ot = (acc_ref[0, 0:nv] / acc_ref[0, nv:nv + 1]
          - lam_full * (acc_ref[1, 0:nv] / acc_ref[1, nv:nv + 1]))
    o = _rms(ot.T, g_ref[...]) * (1.0 - lam_init)
    o_ref[...] = o.astype(o_ref.dtype)


def _diff_attn(qkv, bank, lam, subln_g, lam_init, bsz, s, *, tq=512, tk=512):
    hw = 2 * DIFF_HEAD_DIM
    H = DIFF_HEADS
    assert s % tq == 0 and s % (2 * tk) == 0 and tq % DIFF_TILE == 0 and tk % DIFF_TILE == 0
    n_bank = bank.shape[1]
    return pl.pallas_call(
        functools.partial(_diff_attn_kernel, tq=tq, tk=tk, seq=s, lam_init=lam_init),
        out_shape=jax.ShapeDtypeStruct((bsz, s, H * DIFF_V_DIM), BF16),
        grid=(bsz, H, s // tq),
        in_specs=[pl.BlockSpec((None, tq, hw), lambda b, h, i: (b, i, h)),
                  pl.BlockSpec((None, s, hw), lambda b, h, i: (b, 0, H + h)),
                  pl.BlockSpec((None, s, DIFF_V_DIM), lambda b, h, i: (b, 0, 2 * H + h)),
                  pl.BlockSpec((2, n_bank, DIFF_TILE, DIFF_TILE), lambda b, h, i: (h, 0, 0, 0)),
                  pl.BlockSpec(lam.shape, lambda b, h, i: (0, 0)),
                  pl.BlockSpec((1, DIFF_V_DIM), lambda b, h, i: (0, 0))],
        out_specs=pl.BlockSpec((None, tq, DIFF_V_DIM), lambda b, h, i: (b, i, h)),
        scratch_shapes=[pltpu.VMEM((DIFF_V_DIM + DIFF_ONES_ROWS, s), BF16),
                        pltpu.VMEM((hw, 2 * tq), BF16),
                        pltpu.VMEM((2, tk, 2 * tq), F32),
                        pltpu.VMEM((2, 1, tq), F32),
                        pltpu.VMEM((2, DIFF_V_DIM + DIFF_ONES_ROWS, tq), F32)],
        compiler_params=_params("parallel", "parallel", "arbitrary"),
        name="diff_attn",
    )(qkv, qkv, qkv, bank, lam.astype(F32), subln_g.reshape(1, DIFF_V_DIM).astype(F32))


def _diff_bank_buckets():
    reach = DIFF_BANK_REACH
    local = np.arange(DIFF_TILE)[:, None] - np.arange(DIFF_TILE)[None, :]
    tiles = np.stack([_rel_bucket_np(local + (t - reach) * DIFF_TILE) for t in range(2 * reach + 1)])
    assert (tiles[0] == tiles[0, 0, 0]).all() and (tiles[-1] == tiles[-1, 0, 0]).all()
    return tiles


def _diff_layer(x, norm_g, qkv_w, lam, subln_g, out_w, table, layer_idx, bsz, s):
    d = x.shape[1]
    lam_init = 0.8 - 0.6 * math.exp(-0.3 * layer_idx)
    qkv = _mm(x, qkv_w, g=norm_g, out_dtype=BF16, name="diff_qkv")
    bank = _bias_tiles(table, _diff_bank_buckets(), scale=LOG2_E)
    o = _diff_attn(qkv.reshape(bsz, s, 3 * d), bank, lam, subln_g, lam_init, bsz, s)
    return _mm(o.reshape(bsz * s, d), out_w, res=x, name="diff_out")


def kernel(x, rel_bias, norm_mix_g, norm_mlp_g, mlp_w1, mlp_w2, ssd_in_w, ssd_conv_w, ssd_conv_b, ssd_dt_bias, ssd_a_log, ssd_d, ssd_norm_g, ssd_out_w, dil_qkv_w, dil_out_w, diff_qkv_w, diff_lambda, diff_subln_g, diff_out_w, final_norm_g):
    bsz, s, d = x.shape
    depth = norm_mix_g.shape[0]
    h = x.reshape(bsz * s, d)
    for i in range(depth):
        kind, j = i % 3, i // 3
        if kind == 0:
            h = _ssd_layer(h, norm_mix_g[i], ssd_in_w[j], ssd_conv_w[j], ssd_conv_b[j],
                           ssd_dt_bias[j], ssd_a_log[j], ssd_d[j], ssd_norm_g[j], ssd_out_w[j], bsz, s)
        elif kind == 1:
            h = _dil_layer(h, norm_mix_g[i], dil_qkv_w[j], dil_out_w[j], rel_bias, bsz, s)
        else:
            h = _diff_layer(h, norm_mix_g[i], diff_qkv_w[j], diff_lambda[j], diff_subln_g[j],
                            diff_out_w[j], rel_bias, i, bsz, s)
        h = _mlp(h, norm_mlp_g[i], mlp_w1[i], mlp_w2[i],
                 final_norm_g if i == depth - 1 else None)
    return h.reshape(bsz, s, d)
```

```python
import functools
import math

import numpy as np
import jax
import jax.numpy as jnp
from jax import lax
from jax.experimental import pallas as pl
from jax.experimental.pallas import tpu as pltpu

F32 = jnp.float32
BF16 = jnp.bfloat16
HIGHEST = lax.Precision.HIGHEST

RMS_EPS = 1e-6
NEG_INF = -1e30
LOG2_E = math.log2(math.e)

V7X_VMEM_BYTES = 64 * 1024 * 1024
VMEM_LIMIT_BYTES = V7X_VMEM_BYTES - 8 * 1024 * 1024

SSD_HEAD_DIM = 64
SSD_GROUPS = 8
SSD_HEADS_PER_GROUP = 4
SSD_D_STATE = 128
SSD_CONV_WIDTH = 5
SSD_CHUNK = 128
SSD_GROUP_WIDTH = SSD_HEADS_PER_GROUP * SSD_HEAD_DIM

DIL_CONFIGS = ((128, 1), (512, 4), (2048, 16))
DIL_HEADS = 16
DIL_HEAD_DIM = 64

DIFF_HEADS = 8
DIFF_HEAD_DIM = 64
DIFF_V_DIM = 128
DIFF_TILE = 128
DIFF_BANK_REACH = 6
DIFF_ONES_ROWS = 16

REL_BUCKETS = 32
REL_MAX_DIST = 1024
REL_BIAS_HEADS = 16
MASKED_BUCKET = REL_BUCKETS

NT_DIMS = (((1,), (1,)), ((), ()))
TN_DIMS = (((0,), (0,)), ((), ()))


def _params(*semantics):
    return pltpu.CompilerParams(dimension_semantics=semantics,
                                vmem_limit_bytes=VMEM_LIMIT_BYTES)


def _rms(x, g):
    ms = jnp.mean(x * x, axis=-1, keepdims=True)
    return x * lax.rsqrt(ms + RMS_EPS) * g


def _silu(x):
    return x * (1.0 / (1.0 + jnp.exp(-x)))


def _softplus(x):
    return jnp.maximum(x, 0.0) + jnp.log1p(jnp.exp(-jnp.abs(x)))


def _residue_major_perm(pb, d, inverse=False):
    i = lax.broadcasted_iota(jnp.int32, (pb, pb), 0)
    j = lax.broadcasted_iota(jnp.int32, (pb, pb), 1)
    per = pb // d
    if inverse:
        i, j = j, i
    return (j == (i % per) * d + i // per).astype(BF16)


def _permute_rows(x, pb, d, inverse=False):
    if d == 1:
        return x
    pm = _residue_major_perm(pb, d, inverse)
    blocks = [jnp.dot(pm, x[r0:r0 + pb], preferred_element_type=F32).astype(BF16)
              for r0 in range(0, x.shape[0], pb)]
    return blocks[0] if len(blocks) == 1 else jnp.concatenate(blocks, axis=0)


def _mm_kernel(*refs, norm, res, perm, side):
    refs = list(refs)
    x_ref, w_ref = refs[0], refs[1]
    pos = 2
    g_ref = r_ref = ws_ref = os_ref = None
    if norm:
        g_ref = refs[pos]
        pos += 1
    if res:
        r_ref = refs[pos]
        pos += 1
    if side:
        ws_ref = refs[pos]
        pos += 1
    o_ref = refs[pos]
    pos += 1
    if side:
        os_ref = refs[pos]
        pos += 1
    xs_ref = refs[pos]

    @pl.when(pl.program_id(1) == 0)
    def _():
        x = x_ref[...].astype(F32)
        if norm:
            x = _rms(x, g_ref[...])
        x = x.astype(BF16)
        if perm is not None:
            x = _permute_rows(x, *perm)
        xs_ref[...] = x
        if side:
            os_ref[...] = lax.dot_general(ws_ref[...], x, NT_DIMS, preferred_element_type=F32)

    acc = jnp.dot(xs_ref[...], w_ref[...], preferred_element_type=F32)
    if res:
        acc = r_ref[...] + acc
    o_ref[...] = acc.astype(o_ref.dtype)


def _mm(x, w, *, g=None, res=None, n_cols=None, w_col_block=None, perm=None, side_wt=None,
        out_dtype=F32, tm=1024, tn=1024, name="mm"):
    t, k = x.shape
    n = w.shape[1] if n_cols is None else n_cols
    tn = min(tn, n)
    tm = min(tm, t)
    assert t % tm == 0 and n % tn == 0 and w.shape[0] == k and w.dtype == BF16
    if w_col_block is None:
        w_col_block = lambda j: j
    if perm is not None:
        assert tm % perm[0] == 0 and perm[0] % perm[1] == 0
    in_specs = [pl.BlockSpec((tm, k), lambda i, j: (i, 0)),
                pl.BlockSpec((k, tn), lambda i, j: (0, w_col_block(j)))]
    args = [x, w]
    if g is not None:
        in_specs.append(pl.BlockSpec((1, k), lambda i, j: (0, 0)))
        args.append(g.reshape(1, k))
    if res is not None:
        in_specs.append(pl.BlockSpec((tm, tn), lambda i, j: (i, j)))
        args.append(res)
    out_shape = jax.ShapeDtypeStruct((t, n), out_dtype)
    out_specs = pl.BlockSpec((tm, tn), lambda i, j: (i, j))
    if side_wt is not None:
        m = side_wt.shape[0]
        in_specs.append(pl.BlockSpec((m, k), lambda i, j: (0, 0)))
        args.append(side_wt)
        out_shape = (out_shape, jax.ShapeDtypeStruct((m, t), F32))
        out_specs = (out_specs, pl.BlockSpec((m, tm), lambda i, j: (0, i)))
    return pl.pallas_call(
        functools.partial(_mm_kernel, norm=g is not None, res=res is not None, perm=perm,
                          side=side_wt is not None),
        out_shape=out_shape,
        grid=(t // tm, n // tn),
        in_specs=in_specs,
        out_specs=out_specs,
        scratch_shapes=[pltpu.VMEM((tm, k), BF16)],
        compiler_params=_params("parallel", "arbitrary"),
        name=name,
    )(*args)


def _mlp_kernel(*refs, final):
    if final:
        x_ref, g_ref, w1_ref, w2_ref, fg_ref, o_ref, xn_ref, acc_ref = refs
    else:
        x_ref, g_ref, w1_ref, w2_ref, o_ref, xn_ref, acc_ref = refs
        fg_ref = None
    f = pl.program_id(1)

    @pl.when(f == 0)
    def _():
        x = x_ref[...]
        xn_ref[...] = _rms(x, g_ref[...]).astype(BF16)
        acc_ref[...] = x

    h = jnp.dot(xn_ref[...], w1_ref[...], preferred_element_type=F32)
    h = jnp.square(jnp.maximum(h, 0.0)).astype(BF16)
    acc_ref[...] += jnp.dot(h, w2_ref[...], preferred_element_type=F32)

    @pl.when(f == pl.num_programs(1) - 1)
    def _():
        y = acc_ref[...]
        if final:
            y = _rms(y, fg_ref[...])
        o_ref[...] = y


def _mlp(x, g, w1, w2, final_g=None, *, tm=1024, tf=1024):
    t, d = x.shape
    ff = w1.shape[1]
    assert t % tm == 0 and ff % tf == 0 and w1.dtype == BF16 and w2.dtype == BF16
    final = final_g is not None
    in_specs = [pl.BlockSpec((tm, d), lambda i, f: (i, 0)),
                pl.BlockSpec((1, d), lambda i, f: (0, 0)),
                pl.BlockSpec((d, tf), lambda i, f: (0, f)),
                pl.BlockSpec((tf, d), lambda i, f: (f, 0))]
    args = [x, g.reshape(1, d), w1, w2]
    if final:
        in_specs.append(pl.BlockSpec((1, d), lambda i, f: (0, 0)))
        args.append(final_g.reshape(1, d))
    return pl.pallas_call(
        functools.partial(_mlp_kernel, final=final),
        out_shape=jax.ShapeDtypeStruct((t, d), F32),
        grid=(t // tm, ff // tf),
        in_specs=in_specs,
        out_specs=pl.BlockSpec((tm, d), lambda i, f: (i, 0)),
        scratch_shapes=[pltpu.VMEM((tm, d), BF16), pltpu.VMEM((tm, d), F32)],
        compiler_params=_params("parallel", "arbitrary"),
        name="mlp",
    )(*args)


def _rel_bucket_np(rel):
    half = REL_BUCKETS // 2
    max_exact = half // 2
    rel = np.asarray(rel, np.int32)
    ret = np.where(rel > 0, half, 0)
    n = np.abs(rel)
    nf = np.maximum(n, 1).astype(np.float32)
    scale = np.float32(math.log(REL_MAX_DIST / max_exact))
    large = max_exact + (np.log(nf / np.float32(max_exact)) / scale
                         * np.float32(half - max_exact)).astype(np.int32)
    large = np.minimum(large, half - 1)
    return (ret + np.where(n < max_exact, n, large)).astype(np.int32)


def _bias_tiles_kernel(tbl_ref, idx_ref, o_ref, *, scale):
    h = pl.program_id(0)
    idx = idx_ref[...]
    acc = jnp.zeros(idx.shape, F32)
    for b in range(REL_BUCKETS):
        acc = jnp.where(idx == b, tbl_ref[b, h] * scale, acc)
    o_ref[...] = jnp.where(idx == MASKED_BUCKET, NEG_INF, acc)


def _bias_tiles(table, bucket_idx, scale=1.0):
    n, r, c = bucket_idx.shape
    return pl.pallas_call(
        functools.partial(_bias_tiles_kernel, scale=scale),
        out_shape=jax.ShapeDtypeStruct((REL_BIAS_HEADS, n, r, c), F32),
        grid=(REL_BIAS_HEADS,),
        in_specs=[pl.BlockSpec(memory_space=pltpu.SMEM),
                  pl.BlockSpec((n, r, c), lambda h: (0, 0, 0))],
        out_specs=pl.BlockSpec((None, n, r, c), lambda h: (h, 0, 0, 0)),
        compiler_params=_params("parallel"),
        name="bias_tiles",
    )(table.astype(F32), jnp.asarray(bucket_idx))


def _conv_kernel(x_ref, w_ref, b_ref, o_ref, xp_ref, *, rows):
    s, tc = x_ref.shape
    pad = 8
    half = SSD_CONV_WIDTH // 2
    xp_ref[0:pad, :] = jnp.zeros((pad, tc), F32)
    xp_ref[s + pad:s + 2 * pad, :] = jnp.zeros((pad, tc), F32)
    xp_ref[pad:s + pad, :] = x_ref[...].astype(F32)
    w = w_ref[...]
    b = b_ref[...]
    for ch in range(s // rows):
        r0 = ch * rows
        acc = b
        for k in range(SSD_CONV_WIDTH):
            start = r0 + pad - half + k
            acc = acc + w[k:k + 1, :] * xp_ref[start:start + rows, :]
        o_ref[r0:r0 + rows, :] = _silu(acc).astype(o_ref.dtype)


def _ssd_conv(zx, conv_w, conv_b, *, d_inner, tc=256, rows=512):
    bsz, s, _ = zx.shape
    c = conv_w.shape[1]
    off = d_inner // tc
    return pl.pallas_call(
        functools.partial(_conv_kernel, rows=rows),
        out_shape=jax.ShapeDtypeStruct((bsz, s, c), BF16),
        grid=(bsz, c // tc),
        in_specs=[pl.BlockSpec((None, s, tc), lambda b, j: (b, 0, off + j)),
                  pl.BlockSpec((SSD_CONV_WIDTH, tc), lambda b, j: (0, j)),
                  pl.BlockSpec((1, tc), lambda b, j: (0, j))],
        out_specs=pl.BlockSpec((None, s, tc), lambda b, j: (b, 0, j)),
        scratch_shapes=[pltpu.VMEM((s + 16, tc), F32)],
        compiler_params=_params("parallel", "parallel"),
        name="ssd_conv",
    )(zx, conv_w, conv_b.reshape(1, c))


def _expand_heads(v, head_of_lane):
    r_heads = v.shape[1]
    out = v[:, r_heads - 1:r_heads]
    for r in range(r_heads - 2, -1, -1):
        out = jnp.where(head_of_lane == r, v[:, r:r + 1], out)
    return out


def _split3(x):
    hi = x.astype(BF16)
    r1 = x - hi.astype(F32)
    mid = r1.astype(BF16)
    lo = (r1 - mid.astype(F32)).astype(BF16)
    return hi, mid, lo


def _ssd_scan_kernel(xf_ref, bf_ref, cf_ref, drf_ref, xb_ref, bb_ref, cb_ref, drb_ref,
                     pr_ref, dsk_ref, yf_ref, yb_ref, sf_ref, sb_ref, *, groups):
    @pl.when(pl.program_id(2) == 0)
    def _():
        sf_ref[...] = jnp.zeros(sf_ref.shape, F32)
        sb_ref[...] = jnp.zeros(sb_ref.shape, F32)

    L, R, P, N = SSD_CHUNK, SSD_HEADS_PER_GROUP, SSD_HEAD_DIM, SSD_D_STATE
    W = SSD_GROUP_WIDTH
    row = lax.broadcasted_iota(jnp.int32, (L, L), 0)
    col = lax.broadcasted_iota(jnp.int32, (L, L), 1)
    head_of_lane = lax.broadcasted_iota(jnp.int32, (1, W), 1) // P
    head_of_lane_x = lax.broadcasted_iota(jnp.int32, (L, W), 1) // P
    head_masks = [(head_of_lane_x == r).astype(BF16) for r in range(R)]
    tris = {False: col <= row, True: col >= row}
    tri_b = {rev: t.astype(BF16) for rev, t in tris.items()}
    tri_tb = {False: (row <= col).astype(BF16), True: (row >= col).astype(BF16)}

    items = []
    for g in range(groups):
        pr = pr_ref[g]
        xs, ns = slice(g * W, (g + 1) * W), slice(g * N, (g + 1) * N)
        items.append(dict(g=g, xs=xs, ns=ns, x=xf_ref, b=bf_ref, c=cf_ref, dr=drf_ref, s=sf_ref,
                          y=yf_ref, rev=False, bias_r=pr[:, 0:1], alog_r=pr[:, 1:2]))
        items.append(dict(g=g, xs=xs, ns=ns, x=xb_ref, b=bb_ref, c=cb_ref, dr=drb_ref, s=sb_ref,
                          y=yb_ref, rev=True, bias_r=pr[:, 2:3], alog_r=pr[:, 3:4]))

    for d in items:
        rev = d["rev"]
        d["dtr"] = _softplus(d["dr"][d["g"] * R:(d["g"] + 1) * R, :] + d["bias_r"])
        adt_r = d["dtr"] * (-LOG2_E * jnp.exp(d["alog_r"]))
        pieces = _split3(adt_r)
        d["acs_c"] = sum(lax.dot_general(tri_b[rev], piece, NT_DIMS, preferred_element_type=F32)
                         for piece in pieces)
        d["acs_r"] = sum(jnp.dot(piece, tri_tb[rev], preferred_element_type=F32)
                         for piece in pieces)
        bm = d["b"][:, d["ns"]]
        cm = d["c"][:, d["ns"]]
        d["cb"] = lax.dot_general(cm, bm, NT_DIMS, preferred_element_type=F32)
        d["bt"] = bm.astype(F32).T

    for d in items:
        acs_c, acs_r, dtr = d["acs_c"], d["acs_r"], d["dtr"]
        last_r = acs_r[:, 0:1] if d["rev"] else acs_r[:, L - 1:L]
        key_scale = dtr * jnp.exp2(last_r - acs_r)
        ws, bts, growth = [], [], []
        for r in range(R):
            acs_b = jnp.broadcast_to(acs_c[:, r:r + 1], (L, L))
            seg = acs_b - acs_r[r:r + 1, :]
            lmat = jnp.exp2(jnp.where(tris[d["rev"]], seg, -jnp.inf))
            ws.append((d["cb"] * lmat * dtr[r:r + 1, :]).astype(BF16))
            bts.append((d["bt"] * key_scale[r:r + 1, :]).astype(BF16))
            growth.append(jnp.exp2(acs_b))
        d["wcat"] = jnp.concatenate(ws, axis=1)
        d["btcat"] = jnp.concatenate(bts, axis=1)
        lane_lo = lax.broadcasted_iota(jnp.int32, (L, L), 1) < P
        d["growth"] = jnp.concatenate(
            [jnp.where(lane_lo, growth[r], growth[r + 1]) for r in range(0, R, 2)], axis=1)

    for d in items:
        xb16 = d["x"][:, d["xs"]]
        xstack = jnp.concatenate([xb16 * mask for mask in head_masks], axis=0)
        d["yd"] = jnp.dot(d["wcat"], xstack, preferred_element_type=F32)
        d["upd"] = jnp.dot(d["btcat"], xstack, preferred_element_type=F32)
        d["state"] = d["s"][d["g"]]
        d["y_off"] = jnp.dot(d["c"][:, d["ns"]], d["state"].astype(BF16), preferred_element_type=F32)

    for d in items:
        acs_c = d["acs_c"]
        last_c = acs_c[0:1, :] if d["rev"] else acs_c[L - 1:L, :]
        y = d["yd"] + d["y_off"] * d["growth"]
        d["s"][d["g"]] = d["state"] * _expand_heads(jnp.exp2(last_c), head_of_lane) + d["upd"]
        if not d["rev"]:
            y = y + dsk_ref[d["g"]] * d["x"][:, d["xs"]].astype(F32)
        d["y"][:, d["xs"]] = y.astype(d["y"].dtype)


def _ssd_scan(xbc, dt_t, dt_bias, a_log, d_skip, *, groups=4):
    bsz, s, _ = xbc.shape
    G, R, P, N, L = SSD_GROUPS, SSD_HEADS_PER_GROUP, SSD_HEAD_DIM, SSD_D_STATE, SSD_CHUNK
    assert G % groups == 0 and L == 2 * P
    d_inner = G * R * P
    nc = s // L
    xw = groups * R * P
    nw = groups * N
    b_off = d_inner // nw
    c_off = b_off + G // groups
    gblocks = G // groups
    pr = jnp.stack([dt_bias[0].reshape(G, R), a_log[0].reshape(G, R),
                    dt_bias[1].reshape(G, R), a_log[1].reshape(G, R)], axis=2).astype(F32)
    dsk = jnp.repeat(d_skip.astype(F32).reshape(G, 1, R), P, axis=2)

    def fwd(c):
        return c

    def bwd(c):
        return nc - 1 - c

    def dir_specs(d, cm):
        return [
            pl.BlockSpec((None, L, xw), lambda b, g, c: (b, cm(c), g)),
            pl.BlockSpec((None, L, nw), lambda b, g, c: (b, cm(c), b_off + g)),
            pl.BlockSpec((None, L, nw), lambda b, g, c: (b, cm(c), c_off + g)),
            pl.BlockSpec((groups * R, L), lambda b, g, c: (d * gblocks + g, b * nc + cm(c))),
        ]

    in_specs = dir_specs(0, fwd) + dir_specs(1, bwd) + [
        pl.BlockSpec((groups, R, 4), lambda b, g, c: (g, 0, 0)),
        pl.BlockSpec((groups, 1, R * P), lambda b, g, c: (g, 0, 0)),
    ]
    out_sd = jax.ShapeDtypeStruct((bsz, s, d_inner), BF16)
    return pl.pallas_call(
        functools.partial(_ssd_scan_kernel, groups=groups),
        out_shape=(out_sd, out_sd),
        grid=(bsz, G // groups, nc),
        in_specs=in_specs,
        out_specs=(pl.BlockSpec((None, L, xw), lambda b, g, c: (b, c, g)),
                   pl.BlockSpec((None, L, xw), lambda b, g, c: (b, nc - 1 - c, g))),
        scratch_shapes=[pltpu.VMEM((groups, N, R * P), F32), pltpu.VMEM((groups, N, R * P), F32)],
        compiler_params=_params("parallel", "parallel", "arbitrary"),
        name="ssd_scan",
    )(xbc, xbc, xbc, dt_t, xbc, xbc, xbc, dt_t, pr, dsk)


def _ssd_out_kernel(yf_ref, yb_ref, z_ref, g_ref, w_ref, x_ref, o_ref):
    y = (yf_ref[...].astype(F32) + yb_ref[...].astype(F32)) * _silu(z_ref[...].astype(F32))
    yn = _rms(y, g_ref[...]).astype(BF16)
    o_ref[...] = x_ref[...] + jnp.dot(yn, w_ref[...], preferred_element_type=F32)


def _ssd_out(yf, yb, zx, norm_g, out_w, x, *, tm=512):
    t, d_inner = yf.shape
    d = out_w.shape[1]
    assert t % tm == 0 and out_w.dtype == BF16
    return pl.pallas_call(
        _ssd_out_kernel,
        out_shape=jax.ShapeDtypeStruct((t, d), F32),
        grid=(t // tm,),
        in_specs=[pl.BlockSpec((tm, d_inner), lambda i: (i, 0)),
                  pl.BlockSpec((tm, d_inner), lambda i: (i, 0)),
                  pl.BlockSpec((tm, d_inner), lambda i: (i, 0)),
                  pl.BlockSpec((1, d_inner), lambda i: (0, 0)),
                  pl.BlockSpec((d_inner, d), lambda i: (0, 0)),
                  pl.BlockSpec((tm, d), lambda i: (i, 0))],
        out_specs=pl.BlockSpec((tm, d), lambda i: (i, 0)),
        compiler_params=_params("parallel"),
        name="ssd_out",
    )(yf, yb, zx, norm_g.reshape(1, d_inner), out_w, x)


def _ssd_layer(x, norm_g, in_w, conv_w, conv_b, dt_bias, a_log, d_skip, ssd_norm_g, out_w, bsz, s):
    d_inner = ssd_norm_g.shape[0]
    conv_ch = conv_w.shape[1]
    zx_cols = d_inner + conv_ch
    zx, dt_t = _mm(x, in_w, g=norm_g, n_cols=zx_cols, side_wt=in_w[:, zx_cols:].T, out_dtype=BF16,
                   name="ssd_in")
    xbc = _ssd_conv(zx.reshape(bsz, s, zx_cols), conv_w, conv_b, d_inner=d_inner)
    yf, yb = _ssd_scan(xbc, dt_t, dt_bias, a_log, d_skip)
    t = bsz * s
    return _ssd_out(yf.reshape(t, d_inner), yb.reshape(t, d_inner), zx, ssd_norm_g, out_w, x)


def _dil_attn_kernel(q_ref, kp_ref, kc_ref, kn_ref, vp_ref, vc_ref, vn_ref, bias_ref,
                     o_ref, lse_ref, *, nb, blk):
    j = pl.program_id(1)
    hd = DIL_HEAD_DIM
    variant = (j == 0).astype(jnp.int32) + 2 * (j == nb - 1).astype(jnp.int32)
    lane = lax.broadcasted_iota(jnp.int32, (blk, 2 * hd), 1)
    scale = 1.0 / math.sqrt(hd)
    pairs = DIL_HEADS // 2
    sls = [slice(a * 2 * hd, (a + 1) * 2 * hd) for a in range(pairs)]
    scores = []
    for a, sl in enumerate(sls):
        q = (q_ref[:, sl].astype(F32) * scale).astype(BF16)
        zero = jnp.zeros_like(q)
        qbd = jnp.concatenate([jnp.where(lane < hd, q, zero), jnp.where(lane >= hd, q, zero)], axis=0)
        k = jnp.concatenate([kp_ref[:, sl], kc_ref[:, sl], kn_ref[:, sl]], axis=0)
        scores.append(lax.dot_general(k, qbd, NT_DIMS, preferred_element_type=F32)
                      + bias_ref[variant, a])
    probs = []
    for a, s in enumerate(scores):
        m = jnp.max(s, axis=0, keepdims=True)
        p = jnp.exp(s - m)
        den = jnp.sum(p, axis=0, keepdims=True)
        probs.append((p * (1.0 / den)).astype(BF16))
        lse_ref[a:a + 1, :] = m + jnp.log(den)
    for a, (sl, pn) in enumerate(zip(sls, probs)):
        v = jnp.concatenate([vp_ref[:, sl], vc_ref[:, sl], vn_ref[:, sl]], axis=0)
        r = lax.dot_general(pn, v, TN_DIMS, preferred_element_type=F32)
        o_ref[:, sl] = jnp.where(lane < hd, r[0:blk], r[blk:2 * blk]).astype(o_ref.dtype)


def _dil_bias(table, window, dilation):
    blk = window // (2 * dilation)
    delta = np.arange(3 * blk)[:, None] - blk - np.arange(blk)[None, :]
    bucket = _rel_bucket_np(delta * dilation)
    band = np.abs(delta) <= blk
    key_blk = np.arange(3 * blk)[:, None] // blk
    idx = []
    for variant in range(4):
        ok = band & ~((key_blk == 0) & bool(variant & 1)) & ~((key_blk == 2) & bool(variant & 2))
        idx.append(np.where(ok, bucket, MASKED_BUCKET))
    tiles = _bias_tiles(table, np.stack(idx).astype(np.int32))
    tiles = tiles.reshape(DIL_HEADS // 2, 2, 4, 3 * blk, blk)
    return jnp.transpose(tiles, (2, 0, 3, 1, 4)).reshape(4, DIL_HEADS // 2, 3 * blk, 2 * blk)


def _dil_group(qkv, bias, window, dilation, bsz, s):
    width = DIL_HEADS * DIL_HEAD_DIM
    d = dilation
    n = s // d
    blk = window // (2 * d)
    assert n % blk == 0 and blk % 8 == 0
    nb = n // blk
    pairs = DIL_HEADS // 2

    def spec(which, shift):
        def index(b, jj, r):
            jn = jnp.clip(jj + shift, 0, nb - 1)
            return ((b * nb + jn) * d + r, which)
        return pl.BlockSpec((blk, width), index)

    o, lse = pl.pallas_call(
        functools.partial(_dil_attn_kernel, nb=nb, blk=blk),
        out_shape=(jax.ShapeDtypeStruct((bsz * s, width), BF16),
                   jax.ShapeDtypeStruct((bsz, nb, d, pairs, 2 * blk), F32)),
        grid=(bsz, nb, d),
        in_specs=[spec(0, 0), spec(1, -1), spec(1, 0), spec(1, 1),
                  spec(2, -1), spec(2, 0), spec(2, 1),
                  pl.BlockSpec((4, pairs, 3 * blk, 2 * blk), lambda b, jj, r: (0, 0, 0, 0))],
        out_specs=(pl.BlockSpec((blk, width), lambda b, jj, r: ((b * nb + jj) * d + r, 0)),
                   pl.BlockSpec((None, None, None, pairs, 2 * blk), lambda b, jj, r: (b, jj, r, 0, 0))),
        compiler_params=_params("parallel", "parallel", "parallel"),
        name=f"dil_attn_d{d}",
    )(qkv, qkv, qkv, qkv, qkv, qkv, qkv, bias)
    lse = jnp.transpose(lse.reshape(bsz, nb, d, pairs, 2, blk), (0, 1, 5, 2, 3, 4))
    return o, lse.reshape(bsz * s, DIL_HEADS)


def _dil_out_kernel(o0_ref, o1_ref, o2_ref, l0_ref, l1_ref, l2_ref, w_ref, x_ref, o_ref, *, perms):
    width = o0_ref.shape[1]
    lses = [l0_ref[...], l1_ref[...], l2_ref[...]]
    m = jnp.maximum(jnp.maximum(lses[0], lses[1]), lses[2])
    es = [jnp.exp(l - m) for l in lses]
    inv = 1.0 / (es[0] + es[1] + es[2])
    head_row = lax.broadcasted_iota(jnp.int32, (DIL_HEADS, width), 0)
    head_col = lax.broadcasted_iota(jnp.int32, (DIL_HEADS, width), 1) // DIL_HEAD_DIM
    expand = (head_row == head_col).astype(BF16)
    y = None
    for e, o_ref_g, perm in zip(es, (o0_ref, o1_ref, o2_ref), perms):
        w_g = e * inv
        w_hi = w_g.astype(BF16)
        w_lo = (w_g - w_hi.astype(F32)).astype(BF16)
        wts = (jnp.dot(w_hi, expand, preferred_element_type=F32)
               + jnp.dot(w_lo, expand, preferred_element_type=F32))
        o_nat = _permute_rows(o_ref_g[...], *perm, inverse=True)
        term = wts * o_nat.astype(F32)
        y = term if y is None else y + term
    o_ref[...] = x_ref[...] + jnp.dot(y.astype(BF16), w_ref[...], preferred_element_type=F32)


def _dil_out(os_, lses, perms, out_w, x, *, tm=1024):
    t, width = os_[0].shape
    d = out_w.shape[1]
    assert t % tm == 0 and all(tm % pb == 0 for pb, _ in perms) and out_w.dtype == BF16
    row = lambda i: (i, 0)
    return pl.pallas_call(
        functools.partial(_dil_out_kernel, perms=perms),
        out_shape=jax.ShapeDtypeStruct((t, d), F32),
        grid=(t // tm,),
        in_specs=[pl.BlockSpec((tm, width), row)] * 3 + [pl.BlockSpec((tm, DIL_HEADS), row)] * 3
                 + [pl.BlockSpec((width, d), lambda i: (0, 0)), pl.BlockSpec((tm, d), row)],
        out_specs=pl.BlockSpec((tm, d), row),
        compiler_params=_params("parallel"),
        name="dil_out",
    )(*os_, *lses, out_w, x)


def _dil_layer(x, norm_g, qkv_w, out_w, table, bsz, s):
    n_groups = len(DIL_CONFIGS)
    width = DIL_HEADS * DIL_HEAD_DIM
    tn = width
    per_w = width // tn
    os_, lses, perms = [], [], []
    for gi, (win, dil) in enumerate(DIL_CONFIGS):
        perm = ((win // (2 * dil)) * dil, dil)
        qkv = _mm(x, qkv_w, g=norm_g, n_cols=3 * width, perm=perm, out_dtype=BF16, tn=tn,
                  w_col_block=lambda j, gi=gi: ((j // per_w) * n_groups + gi) * per_w + j % per_w,
                  name=f"dil_qkv_d{dil}")
        o, lse = _dil_group(qkv, _dil_bias(table, win, dil), win, dil, bsz, s)
        os_.append(o)
        lses.append(lse)
        perms.append(perm)
    return _dil_out(os_, lses, tuple(perms), out_w, x)


def _diff_attn_kernel(q_ref, k_ref, v_ref, bank_ref, lam_ref, g_ref, o_ref,
                      vt_ref, qt_ref, s_ref, m_ref, acc_ref, *, tq, tk, seq, lam_init):
    qi = pl.program_id(2)
    tile = DIFF_TILE
    hd = DIFF_HEAD_DIM

    @pl.when(qi == 0)
    def _():
        for c in range(seq // tk):
            vt_ref[0:DIFF_V_DIM, c * tk:(c + 1) * tk] = (
                v_ref[c * tk:(c + 1) * tk, :].astype(F32).T.astype(BF16))
        vt_ref[DIFF_V_DIM:, :] = jnp.ones((DIFF_ONES_ROWS, seq), BF16)

    qt = (q_ref[...].astype(F32) * (LOG2_E / math.sqrt(hd))).T
    row = lax.broadcasted_iota(jnp.int32, qt.shape, 0)
    qt_ref[:, 0:tq] = jnp.where(row < hd, qt, 0.0).astype(BF16)
    qt_ref[:, tq:2 * tq] = jnp.where(row >= hd, qt, 0.0).astype(BF16)
    m_ref[...] = jnp.full(m_ref.shape, -jnp.inf, F32)
    acc_ref[...] = jnp.zeros(acc_ref.shape, F32)

    n_chunks = seq // tk

    def scores(kc, slot):
        start = pl.multiple_of(kc * tk, tk)
        s_ref[slot] = jnp.dot(k_ref[pl.ds(start, tk), :], qt_ref[...],
                              preferred_element_type=F32)

    def consume(kc, slot):
        start = pl.multiple_of(kc * tk, tk)
        vt = vt_ref[:, pl.ds(start, tk)]
        for i in range(2):
            rows = []
            for b in range(tk // tile):
                cols = []
                for a in range(tq // tile):
                    dlt = (kc * (tk // tile) + b) - (qi * (tq // tile) + a)
                    idx = jnp.clip(dlt, -DIFF_BANK_REACH, DIFF_BANK_REACH) + DIFF_BANK_REACH
                    lo = i * tq + a * tile
                    cols.append(s_ref[slot, b * tile:(b + 1) * tile, lo:lo + tile]
                                + bank_ref[i, idx])
                rows.append(jnp.concatenate(cols, axis=1))
            s = jnp.concatenate(rows, axis=0)
            m_old = m_ref[i]
            m_new = jnp.maximum(m_old, jnp.max(s, axis=0, keepdims=True))
            alpha = jnp.exp2(m_old - m_new)
            p = jnp.exp2(s - m_new)
            acc_ref[i] = alpha * acc_ref[i] + jnp.dot(vt, p.astype(BF16), preferred_element_type=F32)
            m_ref[i] = m_new

    scores(0, 0)

    def body(j, carry):
        for slot in range(2):
            kc = 2 * j + slot
            scores(jnp.minimum(kc + 1, n_chunks - 1), 1 - slot)
            consume(kc, slot)
        return carry

    lax.fori_loop(0, n_chunks // 2, body, 0)

    lf = lam_ref[...]
    lam_full = (jnp.exp(jnp.sum(lf[0:1] * lf[1:2], axis=-1, keepdims=True))
                - jnp.exp(jnp.sum(lf[2:3] * lf[3:4], axis=-1, keepdims=True)) + lam_init)
    nv = DIFF_V_DIM
    ot = (acc_ref[0, 0:nv] / acc_ref[0, nv:nv + 1]
          - lam_full * (acc_ref[1, 0:nv] / acc_ref[1, nv:nv + 1]))
    o = _rms(ot.T, g_ref[...]) * (1.0 - lam_init)
    o_ref[...] = o.astype(o_ref.dtype)


def _diff_attn(qkv, bank, lam, subln_g, lam_init, bsz, s, *, tq=512, tk=512):
    hw = 2 * DIFF_HEAD_DIM
    H = DIFF_HEADS
    assert s % tq == 0 and s % (2 * tk) == 0 and tq % DIFF_TILE == 0 and tk % DIFF_TILE == 0
    n_bank = bank.shape[1]
    return pl.pallas_call(
        functools.partial(_diff_attn_kernel, tq=tq, tk=tk, seq=s, lam_init=lam_init),
        out_shape=jax.ShapeDtypeStruct((bsz, s, H * DIFF_V_DIM), BF16),
        grid=(bsz, H, s // tq),
        in_specs=[pl.BlockSpec((None, tq, hw), lambda b, h, i: (b, i, h)),
                  pl.BlockSpec((None, s, hw), lambda b, h, i: (b, 0, H + h)),
                  pl.BlockSpec((None, s, DIFF_V_DIM), lambda b, h, i: (b, 0, 2 * H + h)),
                  pl.BlockSpec((2, n_bank, DIFF_TILE, DIFF_TILE), lambda b, h, i: (h, 0, 0, 0)),
                  pl.BlockSpec(lam.shape, lambda b, h, i: (0, 0)),
                  pl.BlockSpec((1, DIFF_V_DIM), lambda b, h, i: (0, 0))],
        out_specs=pl.BlockSpec((None, tq, DIFF_V_DIM), lambda b, h, i: (b, i, h)),
        scratch_shapes=[pltpu.VMEM((DIFF_V_DIM + DIFF_ONES_ROWS, s), BF16),
                        pltpu.VMEM((hw, 2 * tq), BF16),
                        pltpu.VMEM((2, tk, 2 * tq), F32),
                        pltpu.VMEM((2, 1, tq), F32),
                        pltpu.VMEM((2, DIFF_V_DIM + DIFF_ONES_ROWS, tq), F32)],
        compiler_params=_params("parallel", "parallel", "arbitrary"),
        name="diff_attn",
    )(qkv, qkv, qkv, bank, lam.astype(F32), subln_g.reshape(1, DIFF_V_DIM).astype(F32))


def _diff_bank_buckets():
    reach = DIFF_BANK_REACH
    local = np.arange(DIFF_TILE)[:, None] - np.arange(DIFF_TILE)[None, :]
    tiles = np.stack([_rel_bucket_np(local + (t - reach) * DIFF_TILE) for t in range(2 * reach + 1)])
    assert (tiles[0] == tiles[0, 0, 0]).all() and (tiles[-1] == tiles[-1, 0, 0]).all()
    return tiles


def _diff_layer(x, norm_g, qkv_w, lam, subln_g, out_w, table, layer_idx, bsz, s):
    d = x.shape[1]
    lam_init = 0.8 - 0.6 * math.exp(-0.3 * layer_idx)
    qkv = _mm(x, qkv_w, g=norm_g, out_dtype=BF16, name="diff_qkv")
    bank = _bias_tiles(table, _diff_bank_buckets(), scale=LOG2_E)
    o = _diff_attn(qkv.reshape(bsz, s, 3 * d), bank, lam, subln_g, lam_init, bsz, s)
    return _mm(o.reshape(bsz * s, d), out_w, res=x, name="diff_out")


def kernel(x, rel_bias, norm_mix_g, norm_mlp_g, mlp_w1, mlp_w2, ssd_in_w, ssd_conv_w, ssd_conv_b, ssd_dt_bias, ssd_a_log, ssd_d, ssd_norm_g, ssd_out_w, dil_qkv_w, dil_out_w, diff_qkv_w, diff_lambda, diff_subln_g, diff_out_w, final_norm_g):
    bsz, s, d = x.shape
    depth = norm_mix_g.shape[0]
    (mlp_w1, mlp_w2, ssd_in_w, ssd_out_w, dil_qkv_w, dil_out_w, diff_qkv_w, diff_out_w) = (
        w.astype(BF16) for w in (mlp_w1, mlp_w2, ssd_in_w, ssd_out_w, dil_qkv_w, dil_out_w,
                                 diff_qkv_w, diff_out_w))
    h = x.reshape(bsz * s, d)
    for i in range(depth):
        kind, j = i % 3, i // 3
        if kind == 0:
            h = _ssd_layer(h, norm_mix_g[i], ssd_in_w[j], ssd_conv_w[j], ssd_conv_b[j],
                           ssd_dt_bias[j], ssd_a_log[j], ssd_d[j], ssd_norm_g[j], ssd_out_w[j], bsz, s)
        elif kind == 1:
            h = _dil_layer(h, norm_mix_g[i], dil_qkv_w[j], dil_out_w[j], rel_bias, bsz, s)
        else:
            h = _diff_layer(h, norm_mix_g[i], diff_qkv_w[j], diff_lambda[j], diff_subln_g[j],
                            diff_out_w[j], rel_bias, i, bsz, s)
        h = _mlp(h, norm_mlp_g[i], mlp_w1[i], mlp_w2[i],
                 final_norm_g if i == depth - 1 else None)
    return h.reshape(bsz, s, d)
```

```python
import functools
import math

import numpy as np
import jax
import jax.numpy as jnp
from jax import lax
from jax.experimental import pallas as pl
from jax.experimental.pallas import tpu as pltpu

F32 = jnp.float32
BF16 = jnp.bfloat16

RMS_EPS = 1e-6
NEG_INF = -1e30
LOG2_E = math.log2(math.e)

V7X_VMEM_BYTES = 64 * 1024 * 1024
VMEM_LIMIT_BYTES = V7X_VMEM_BYTES - 8 * 1024 * 1024

SSD_HEAD_DIM = 64
SSD_GROUPS = 8
SSD_HEADS_PER_GROUP = 4
SSD_D_STATE = 128
SSD_CONV_WIDTH = 5
SSD_CHUNK = 128
SSD_GROUP_WIDTH = SSD_HEADS_PER_GROUP * SSD_HEAD_DIM

DIL_CONFIGS = ((128, 1), (512, 4), (2048, 16))
DIL_HEADS = 16
DIL_HEAD_DIM = 64

DIFF_HEADS = 8
DIFF_HEAD_DIM = 64
DIFF_V_DIM = 128
DIFF_TILE = 128
DIFF_BANK_REACH = 6
DIFF_ONES_ROWS = 16

REL_BUCKETS = 32
REL_MAX_DIST = 1024
REL_BIAS_HEADS = 16
MASKED_BUCKET = REL_BUCKETS

NT_DIMS = (((1,), (1,)), ((), ()))
TN_DIMS = (((0,), (0,)), ((), ()))


def _params(*semantics):
    return pltpu.CompilerParams(dimension_semantics=semantics,
                                vmem_limit_bytes=VMEM_LIMIT_BYTES)


def _rms(x, g):
    ms = jnp.mean(x * x, axis=-1, keepdims=True)
    return x * lax.rsqrt(ms + RMS_EPS) * g


def _silu(x):
    return x * (1.0 / (1.0 + jnp.exp(-x)))


def _softplus(x):
    return jnp.maximum(x, 0.0) + jnp.log1p(jnp.exp(-jnp.abs(x)))


def _residue_major_perm(pb, d, inverse=False):
    i = lax.broadcasted_iota(jnp.int32, (pb, pb), 0)
    j = lax.broadcasted_iota(jnp.int32, (pb, pb), 1)
    per = pb // d
    if inverse:
        i, j = j, i
    return (j == (i % per) * d + i // per).astype(BF16)


def _permute_rows(x, pb, d, inverse=False):
    if d == 1:
        return x
    pm = _residue_major_perm(pb, d, inverse)
    blocks = [jnp.dot(pm, x[r0:r0 + pb], preferred_element_type=F32).astype(BF16)
              for r0 in range(0, x.shape[0], pb)]
    return blocks[0] if len(blocks) == 1 else jnp.concatenate(blocks, axis=0)


def _mm_kernel(*refs, norm, res, perm, side):
    refs = list(refs)
    x_ref, w_ref = refs[0], refs[1]
    pos = 2
    g_ref = r_ref = ws_ref = os_ref = None
    if norm:
        g_ref = refs[pos]
        pos += 1
    if res:
        r_ref = refs[pos]
        pos += 1
    if side:
        ws_ref = refs[pos]
        pos += 1
    o_ref = refs[pos]
    pos += 1
    if side:
        os_ref = refs[pos]
        pos += 1
    xs_ref = refs[pos]

    @pl.when(pl.program_id(1) == 0)
    def _():
        x = x_ref[...].astype(F32)
        if norm:
            x = _rms(x, g_ref[...])
        x = x.astype(BF16)
        if perm is not None:
            x = _permute_rows(x, *perm)
        xs_ref[...] = x
        if side:
            os_ref[...] = lax.dot_general(ws_ref[...], x, NT_DIMS, preferred_element_type=F32)

    acc = jnp.dot(xs_ref[...], w_ref[...], preferred_element_type=F32)
    if res:
        acc = r_ref[...] + acc
    o_ref[...] = acc.astype(o_ref.dtype)


def _mm(x, w, *, g=None, res=None, n_cols=None, w_col_block=None, perm=None, side_wt=None,
        out_dtype=F32, tm=1024, tn=1024, name="mm"):
    t, k = x.shape
    n = w.shape[1] if n_cols is None else n_cols
    tn = min(tn, n)
    tm = min(tm, t)
    assert t % tm == 0 and n % tn == 0 and w.shape[0] == k and w.dtype == BF16
    if w_col_block is None:
        w_col_block = lambda j: j
    if perm is not None:
        assert tm % perm[0] == 0 and perm[0] % perm[1] == 0
    in_specs = [pl.BlockSpec((tm, k), lambda i, j: (i, 0)),
                pl.BlockSpec((k, tn), lambda i, j: (0, w_col_block(j)))]
    args = [x, w]
    if g is not None:
        in_specs.append(pl.BlockSpec((1, k), lambda i, j: (0, 0)))
        args.append(g.reshape(1, k))
    if res is not None:
        in_specs.append(pl.BlockSpec((tm, tn), lambda i, j: (i, j)))
        args.append(res)
    out_shape = jax.ShapeDtypeStruct((t, n), out_dtype)
    out_specs = pl.BlockSpec((tm, tn), lambda i, j: (i, j))
    if side_wt is not None:
        m = side_wt.shape[0]
        in_specs.append(pl.BlockSpec((m, k), lambda i, j: (0, 0)))
        args.append(side_wt)
        out_shape = (out_shape, jax.ShapeDtypeStruct((m, t), F32))
        out_specs = (out_specs, pl.BlockSpec((m, tm), lambda i, j: (0, i)))
    return pl.pallas_call(
        functools.partial(_mm_kernel, norm=g is not None, res=res is not None, perm=perm,
                          side=side_wt is not None),
        out_shape=out_shape,
        grid=(t // tm, n // tn),
        in_specs=in_specs,
        out_specs=out_specs,
        scratch_shapes=[pltpu.VMEM((tm, k), BF16)],
        compiler_params=_params("parallel", "arbitrary"),
        name=name,
    )(*args)


def _mlp_kernel(*refs, final):
    if final:
        x_ref, g_ref, w1_ref, w2_ref, fg_ref, o_ref, xn_ref, acc_ref = refs
    else:
        x_ref, g_ref, w1_ref, w2_ref, o_ref, xn_ref, acc_ref = refs
        fg_ref = None
    f = pl.program_id(1)

    @pl.when(f == 0)
    def _():
        x = x_ref[...]
        xn_ref[...] = _rms(x, g_ref[...]).astype(BF16)
        acc_ref[...] = x

    h = jnp.dot(xn_ref[...], w1_ref[...], preferred_element_type=F32)
    h = jnp.square(jnp.maximum(h, 0.0)).astype(BF16)
    acc_ref[...] += jnp.dot(h, w2_ref[...], preferred_element_type=F32)

    @pl.when(f == pl.num_programs(1) - 1)
    def _():
        y = acc_ref[...]
        if final:
            y = _rms(y, fg_ref[...])
        o_ref[...] = y


def _mlp(x, g, w1, w2, final_g=None, *, tm=1024, tf=1024):
    t, d = x.shape
    ff = w1.shape[1]
    assert t % tm == 0 and ff % tf == 0 and w1.dtype == BF16 and w2.dtype == BF16
    final = final_g is not None
    in_specs = [pl.BlockSpec((tm, d), lambda i, f: (i, 0)),
                pl.BlockSpec((1, d), lambda i, f: (0, 0)),
                pl.BlockSpec((d, tf), lambda i, f: (0, f)),
                pl.BlockSpec((tf, d), lambda i, f: (f, 0))]
    args = [x, g.reshape(1, d), w1, w2]
    if final:
        in_specs.append(pl.BlockSpec((1, d), lambda i, f: (0, 0)))
        args.append(final_g.reshape(1, d))
    return pl.pallas_call(
        functools.partial(_mlp_kernel, final=final),
        out_shape=jax.ShapeDtypeStruct((t, d), F32),
        grid=(t // tm, ff // tf),
        in_specs=in_specs,
        out_specs=pl.BlockSpec((tm, d), lambda i, f: (i, 0)),
        scratch_shapes=[pltpu.VMEM((tm, d), BF16), pltpu.VMEM((tm, d), F32)],
        compiler_params=_params("parallel", "arbitrary"),
        name="mlp",
    )(*args)


def _rel_bucket_np(rel):
    half = REL_BUCKETS // 2
    max_exact = half // 2
    rel = np.asarray(rel, np.int32)
    ret = np.where(rel > 0, half, 0)
    n = np.abs(rel)
    nf = np.maximum(n, 1).astype(np.float32)
    scale = np.float32(math.log(REL_MAX_DIST / max_exact))
    large = max_exact + (np.log(nf / np.float32(max_exact)) / scale
                         * np.float32(half - max_exact)).astype(np.int32)
    large = np.minimum(large, half - 1)
    return (ret + np.where(n < max_exact, n, large)).astype(np.int32)


def _bias_tiles_kernel(tbl_ref, idx_ref, o_ref, *, scale):
    h = pl.program_id(0)
    idx = idx_ref[...]
    acc = jnp.zeros(idx.shape, F32)
    for b in range(REL_BUCKETS):
        acc = jnp.where(idx == b, tbl_ref[b, h] * scale, acc)
    o_ref[...] = jnp.where(idx == MASKED_BUCKET, NEG_INF, acc)


def _bias_tiles(table, bucket_idx, scale=1.0):
    n, r, c = bucket_idx.shape
    return pl.pallas_call(
        functools.partial(_bias_tiles_kernel, scale=scale),
        out_shape=jax.ShapeDtypeStruct((REL_BIAS_HEADS, n, r, c), F32),
        grid=(REL_BIAS_HEADS,),
        in_specs=[pl.BlockSpec(memory_space=pltpu.SMEM),
                  pl.BlockSpec((n, r, c), lambda h: (0, 0, 0))],
        out_specs=pl.BlockSpec((None, n, r, c), lambda h: (h, 0, 0, 0)),
        compiler_params=_params("parallel"),
        name="bias_tiles",
    )(table.astype(F32), jnp.asarray(bucket_idx))


def _conv_kernel(x_ref, w_ref, b_ref, o_ref, xp_ref, *, rows):
    s, tc = x_ref.shape
    pad = 16
    half = SSD_CONV_WIDTH // 2
    xp_ref[0:pad, :] = jnp.zeros((pad, tc), BF16)
    xp_ref[s + pad:s + 2 * pad, :] = jnp.zeros((pad, tc), BF16)
    xp_ref[pad:s + pad, :] = x_ref[...]
    win = rows + 2 * pad
    side_taps = [k for k in range(SSD_CONV_WIDTH) if k != half]
    out_row = lax.broadcasted_iota(jnp.int32, (rows, win), 0)
    win_row = lax.broadcasted_iota(jnp.int32, (rows, win), 1)
    sel = jnp.concatenate([(win_row == out_row + (pad + k - half)).astype(BF16) for k in side_taps],
                          axis=0)
    w = w_ref[...]
    b = b_ref[...]
    for ch in range(s // rows):
        r0 = ch * rows
        window = xp_ref[r0:r0 + win, :]
        shifted = jnp.dot(sel, window, preferred_element_type=F32)
        acc = b + w[half:half + 1, :] * window[pad:pad + rows].astype(F32)
        for t, k in enumerate(side_taps):
            acc = acc + w[k:k + 1, :] * shifted[t * rows:(t + 1) * rows]
        o_ref[r0:r0 + rows, :] = _silu(acc).astype(o_ref.dtype)


def _ssd_conv(zx, conv_w, conv_b, *, d_inner, tc=256, rows=128):
    bsz, s, _ = zx.shape
    c = conv_w.shape[1]
    off = d_inner // tc
    assert zx.dtype == BF16 and s % rows == 0 and rows % 16 == 0
    return pl.pallas_call(
        functools.partial(_conv_kernel, rows=rows),
        out_shape=jax.ShapeDtypeStruct((bsz, s, c), BF16),
        grid=(bsz, c // tc),
        in_specs=[pl.BlockSpec((None, s, tc), lambda b, j: (b, 0, off + j)),
                  pl.BlockSpec((SSD_CONV_WIDTH, tc), lambda b, j: (0, j)),
                  pl.BlockSpec((1, tc), lambda b, j: (0, j))],
        out_specs=pl.BlockSpec((None, s, tc), lambda b, j: (b, 0, j)),
        scratch_shapes=[pltpu.VMEM((s + 32, tc), BF16)],
        compiler_params=_params("parallel", "parallel"),
        name="ssd_conv",
    )(zx, conv_w, conv_b.reshape(1, c))


def _expand_heads(v, head_of_lane):
    r_heads = v.shape[1]
    out = v[:, r_heads - 1:r_heads]
    for r in range(r_heads - 2, -1, -1):
        out = jnp.where(head_of_lane == r, v[:, r:r + 1], out)
    return out


def _split3(x):
    hi = x.astype(BF16)
    r1 = x - hi.astype(F32)
    mid = r1.astype(BF16)
    lo = (r1 - mid.astype(F32)).astype(BF16)
    return hi, mid, lo


def _ssd_scan_kernel(xf_ref, bf_ref, cf_ref, drf_ref, xb_ref, bb_ref, cb_ref, drb_ref,
                     pr_ref, dsk_ref, yf_ref, yb_ref, sf_ref, sb_ref, *, groups):
    @pl.when(pl.program_id(2) == 0)
    def _():
        sf_ref[...] = jnp.zeros(sf_ref.shape, F32)
        sb_ref[...] = jnp.zeros(sb_ref.shape, F32)

    L, R, P, N = SSD_CHUNK, SSD_HEADS_PER_GROUP, SSD_HEAD_DIM, SSD_D_STATE
    W = SSD_GROUP_WIDTH
    row = lax.broadcasted_iota(jnp.int32, (L, L), 0)
    col = lax.broadcasted_iota(jnp.int32, (L, L), 1)
    head_of_lane = lax.broadcasted_iota(jnp.int32, (1, W), 1) // P
    head_of_lane_x = lax.broadcasted_iota(jnp.int32, (L, W), 1) // P
    head_masks = [(head_of_lane_x == r).astype(BF16) for r in range(R)]
    tris = {False: col <= row, True: col >= row}
    tri_b = {rev: t.astype(BF16) for rev, t in tris.items()}
    tri_tb = {False: (row <= col).astype(BF16), True: (row >= col).astype(BF16)}

    items = []
    for g in range(groups):
        pr = pr_ref[g]
        xs, ns = slice(g * W, (g + 1) * W), slice(g * N, (g + 1) * N)
        items.append(dict(g=g, xs=xs, ns=ns, x=xf_ref, b=bf_ref, c=cf_ref, dr=drf_ref, s=sf_ref,
                          y=yf_ref, rev=False, bias_r=pr[:, 0:1], alog_r=pr[:, 1:2]))
        items.append(dict(g=g, xs=xs, ns=ns, x=xb_ref, b=bb_ref, c=cb_ref, dr=drb_ref, s=sb_ref,
                          y=yb_ref, rev=True, bias_r=pr[:, 2:3], alog_r=pr[:, 3:4]))

    for d in items:
        rev = d["rev"]
        d["dtr"] = _softplus(d["dr"][d["g"] * R:(d["g"] + 1) * R, :] + d["bias_r"])
        adt_r = d["dtr"] * (-LOG2_E * jnp.exp(d["alog_r"]))
        pieces = _split3(adt_r)
        d["acs_c"] = sum(lax.dot_general(tri_b[rev], piece, NT_DIMS, preferred_element_type=F32)
                         for piece in pieces)
        acs3 = jnp.dot(jnp.concatenate(pieces, axis=0), tri_tb[rev], preferred_element_type=F32)
        d["acs_r"] = acs3[0:R] + acs3[R:2 * R] + acs3[2 * R:3 * R]
        bm = d["b"][:, d["ns"]]
        cm = d["c"][:, d["ns"]]
        d["cb"] = lax.dot_general(cm, bm, NT_DIMS, preferred_element_type=F32)
        d["bt"] = bm.astype(F32).T

    for d in items:
        acs_c, acs_r = d["acs_c"], d["acs_r"]
        last_r = acs_r[:, 0:1] if d["rev"] else acs_r[:, L - 1:L]
        acs_k = acs_r - jnp.log2(d["dtr"])
        key_scale = jnp.exp2(last_r - acs_k)
        ws, bts, growth = [], [], []
        for r in range(R):
            acs_b = jnp.broadcast_to(acs_c[:, r:r + 1], (L, L))
            seg = acs_b - acs_k[r:r + 1, :]
            lmat = jnp.exp2(jnp.where(tris[d["rev"]], seg, -jnp.inf))
            ws.append((d["cb"] * lmat).astype(BF16))
            bts.append((d["bt"] * key_scale[r:r + 1, :]).astype(BF16))
            growth.append(jnp.exp2(acs_b))
        d["wb"] = jnp.concatenate([jnp.concatenate(ws, axis=1), jnp.concatenate(bts, axis=1)], axis=0)
        lane_lo = lax.broadcasted_iota(jnp.int32, (L, L), 1) < P
        d["growth"] = jnp.concatenate(
            [jnp.where(lane_lo, growth[r], growth[r + 1]) for r in range(0, R, 2)], axis=1)

    for d in items:
        xb16 = d["x"][:, d["xs"]]
        xstack = jnp.concatenate([xb16 * mask for mask in head_masks], axis=0)
        both = jnp.dot(d["wb"], xstack, preferred_element_type=F32)
        d["yd"], d["upd"] = both[0:L], both[L:L + N]
        d["state"] = d["s"][d["g"]]
        d["y_off"] = jnp.dot(d["c"][:, d["ns"]], d["state"].astype(BF16), preferred_element_type=F32)

    for d in items:
        acs_c = d["acs_c"]
        last_c = acs_c[0:1, :] if d["rev"] else acs_c[L - 1:L, :]
        y = d["yd"] + d["y_off"] * d["growth"]
        d["s"][d["g"]] = d["state"] * _expand_heads(jnp.exp2(last_c), head_of_lane) + d["upd"]
        if not d["rev"]:
            y = y + dsk_ref[d["g"]] * d["x"][:, d["xs"]].astype(F32)
        d["y"][:, d["xs"]] = y.astype(d["y"].dtype)


def _ssd_scan(xbc, dt_t, dt_bias, a_log, d_skip, *, groups=4):
    bsz, s, _ = xbc.shape
    G, R, P, N, L = SSD_GROUPS, SSD_HEADS_PER_GROUP, SSD_HEAD_DIM, SSD_D_STATE, SSD_CHUNK
    assert G % groups == 0 and L == 2 * P
    d_inner = G * R * P
    nc = s // L
    xw = groups * R * P
    nw = groups * N
    b_off = d_inner // nw
    c_off = b_off + G // groups
    gblocks = G // groups
    pr = jnp.stack([dt_bias[0].reshape(G, R), a_log[0].reshape(G, R),
                    dt_bias[1].reshape(G, R), a_log[1].reshape(G, R)], axis=2).astype(F32)
    dsk = jnp.repeat(d_skip.astype(F32).reshape(G, 1, R), P, axis=2)

    def fwd(c):
        return c

    def bwd(c):
        return nc - 1 - c

    def dir_specs(d, cm):
        return [
            pl.BlockSpec((None, L, xw), lambda b, g, c: (b, cm(c), g)),
            pl.BlockSpec((None, L, nw), lambda b, g, c: (b, cm(c), b_off + g)),
            pl.BlockSpec((None, L, nw), lambda b, g, c: (b, cm(c), c_off + g)),
            pl.BlockSpec((groups * R, L), lambda b, g, c: (d * gblocks + g, b * nc + cm(c))),
        ]

    in_specs = dir_specs(0, fwd) + dir_specs(1, bwd) + [
        pl.BlockSpec((groups, R, 4), lambda b, g, c: (g, 0, 0)),
        pl.BlockSpec((groups, 1, R * P), lambda b, g, c: (g, 0, 0)),
    ]
    out_sd = jax.ShapeDtypeStruct((bsz, s, d_inner), BF16)
    return pl.pallas_call(
        functools.partial(_ssd_scan_kernel, groups=groups),
        out_shape=(out_sd, out_sd),
        grid=(bsz, G // groups, nc),
        in_specs=in_specs,
        out_specs=(pl.BlockSpec((None, L, xw), lambda b, g, c: (b, c, g)),
                   pl.BlockSpec((None, L, xw), lambda b, g, c: (b, nc - 1 - c, g))),
        scratch_shapes=[pltpu.VMEM((groups, N, R * P), F32), pltpu.VMEM((groups, N, R * P), F32)],
        compiler_params=_params("parallel", "parallel", "arbitrary"),
        name="ssd_scan",
    )(xbc, xbc, xbc, dt_t, xbc, xbc, xbc, dt_t, pr, dsk)


def _ssd_out_kernel(yf_ref, yb_ref, z_ref, g_ref, w_ref, x_ref, o_ref):
    y = (yf_ref[...].astype(F32) + yb_ref[...].astype(F32)) * _silu(z_ref[...].astype(F32))
    yn = _rms(y, g_ref[...]).astype(BF16)
    o_ref[...] = x_ref[...] + jnp.dot(yn, w_ref[...], preferred_element_type=F32)


def _ssd_out(yf, yb, zx, norm_g, out_w, x, *, tm=512):
    t, d_inner = yf.shape
    d = out_w.shape[1]
    assert t % tm == 0 and out_w.dtype == BF16
    return pl.pallas_call(
        _ssd_out_kernel,
        out_shape=jax.ShapeDtypeStruct((t, d), F32),
        grid=(t // tm,),
        in_specs=[pl.BlockSpec((tm, d_inner), lambda i: (i, 0)),
                  pl.BlockSpec((tm, d_inner), lambda i: (i, 0)),
                  pl.BlockSpec((tm, d_inner), lambda i: (i, 0)),
                  pl.BlockSpec((1, d_inner), lambda i: (0, 0)),
                  pl.BlockSpec((d_inner, d), lambda i: (0, 0)),
                  pl.BlockSpec((tm, d), lambda i: (i, 0))],
        out_specs=pl.BlockSpec((tm, d), lambda i: (i, 0)),
        compiler_params=_params("parallel"),
        name="ssd_out",
    )(yf, yb, zx, norm_g.reshape(1, d_inner), out_w, x)


def _ssd_layer(x, norm_g, in_w, conv_w, conv_b, dt_bias, a_log, d_skip, ssd_norm_g, out_w, bsz, s):
    d_inner = ssd_norm_g.shape[0]
    conv_ch = conv_w.shape[1]
    zx_cols = d_inner + conv_ch
    zx, dt_t = _mm(x, in_w, g=norm_g, n_cols=zx_cols, side_wt=in_w[:, zx_cols:].T, out_dtype=BF16,
                   tn=zx_cols // 2, name="ssd_in")
    xbc = _ssd_conv(zx.reshape(bsz, s, zx_cols), conv_w, conv_b, d_inner=d_inner)
    yf, yb = _ssd_scan(xbc, dt_t, dt_bias, a_log, d_skip)
    t = bsz * s
    return _ssd_out(yf.reshape(t, d_inner), yb.reshape(t, d_inner), zx, ssd_norm_g, out_w, x)


def _dil_attn_kernel(q_ref, kp_ref, kc_ref, kn_ref, vp_ref, vc_ref, vn_ref, bias_ref,
                     o_ref, lse_ref, *, n_steps, blk, split_rows):
    j = pl.program_id(1)
    hd = DIL_HEAD_DIM
    first = (j == 0).astype(jnp.int32)
    last = (j == n_steps - 1).astype(jnp.int32)
    lo, hi = slice(0, blk), slice(blk, 2 * blk)
    if split_rows:
        subs = ((lo, ((0, hi), (1, lo), (1, hi)), first),
                (hi, ((1, lo), (1, hi), (2, lo)), 2 * last))
    else:
        subs = tuple((rows, ((0, rows), (1, rows), (2, rows)), first + 2 * last) for rows in (lo, hi))
    k_refs, v_refs = (kp_ref, kc_ref, kn_ref), (vp_ref, vc_ref, vn_ref)
    lane = lax.broadcasted_iota(jnp.int32, (blk, 2 * hd), 1)
    scale = 1.0 / math.sqrt(hd)
    pairs = DIL_HEADS // 2
    tiles = [(si, a, slice(a * 2 * hd, (a + 1) * 2 * hd)) for si in range(2) for a in range(pairs)]
    scores = []
    for si, a, sl in tiles:
        q_rows, kv, variant = subs[si]
        q = (q_ref[q_rows, sl].astype(F32) * scale).astype(BF16)
        zero = jnp.zeros_like(q)
        qbd = jnp.concatenate([jnp.where(lane < hd, q, zero), jnp.where(lane >= hd, q, zero)], axis=0)
        k = jnp.concatenate([k_refs[ri][rows, sl] for ri, rows in kv], axis=0)
        scores.append(lax.dot_general(k, qbd, NT_DIMS, preferred_element_type=F32)
                      + bias_ref[variant, a])
    probs = []
    for (si, a, sl), s in zip(tiles, scores):
        m = jnp.max(s, axis=0, keepdims=True)
        p = jnp.exp(s - m)
        den = jnp.sum(p, axis=0, keepdims=True)
        probs.append((p * (1.0 / den)).astype(BF16))
        lse_ref[si, a:a + 1, :] = m + jnp.log(den)
    for (si, a, sl), pn in zip(tiles, probs):
        q_rows, kv, _ = subs[si]
        v = jnp.concatenate([v_refs[ri][rows, sl] for ri, rows in kv], axis=0)
        r = lax.dot_general(pn, v, TN_DIMS, preferred_element_type=F32)
        o_ref[q_rows, sl] = jnp.where(lane < hd, r[0:blk], r[blk:2 * blk]).astype(o_ref.dtype)


def _dil_bias(table, window, dilation):
    blk = window // (2 * dilation)
    delta = np.arange(3 * blk)[:, None] - blk - np.arange(blk)[None, :]
    bucket = _rel_bucket_np(delta * dilation)
    band = np.abs(delta) <= blk
    key_blk = np.arange(3 * blk)[:, None] // blk
    idx = []
    for variant in range(4):
        ok = band & ~((key_blk == 0) & bool(variant & 1)) & ~((key_blk == 2) & bool(variant & 2))
        idx.append(np.where(ok, bucket, MASKED_BUCKET))
    tiles = _bias_tiles(table, np.stack(idx).astype(np.int32))
    tiles = tiles.reshape(DIL_HEADS // 2, 2, 4, 3 * blk, blk)
    return jnp.transpose(tiles, (2, 0, 3, 1, 4)).reshape(4, DIL_HEADS // 2, 3 * blk, 2 * blk)


def _dil_group(qkv, bias, window, dilation, bsz, s):
    width = DIL_HEADS * DIL_HEAD_DIM
    d = dilation
    n = s // d
    blk = window // (2 * d)
    assert n % blk == 0 and blk % 8 == 0
    nb = n // blk
    pairs = DIL_HEADS // 2
    split_rows = d == 1
    n_steps, n_res = (nb // 2, 1) if split_rows else (nb, d // 2)
    assert (nb if split_rows else d) % 2 == 0

    def spec(which, shift):
        def index(b, jj, r):
            jn = jnp.clip(jj + shift, 0, n_steps - 1)
            return ((b * n_steps + jn) * n_res + r, which)
        return pl.BlockSpec((2 * blk, width), index)

    o, lse = pl.pallas_call(
        functools.partial(_dil_attn_kernel, n_steps=n_steps, blk=blk, split_rows=split_rows),
        out_shape=(jax.ShapeDtypeStruct((bsz * s, width), BF16),
                   jax.ShapeDtypeStruct((bsz, n_steps, n_res, 2, pairs, 2 * blk), F32)),
        grid=(bsz, n_steps, n_res),
        in_specs=[spec(0, 0), spec(1, -1), spec(1, 0), spec(1, 1),
                  spec(2, -1), spec(2, 0), spec(2, 1),
                  pl.BlockSpec((4, pairs, 3 * blk, 2 * blk), lambda b, jj, r: (0, 0, 0, 0))],
        out_specs=(pl.BlockSpec((2 * blk, width), lambda b, jj, r: ((b * n_steps + jj) * n_res + r, 0)),
                   pl.BlockSpec((None, None, None, 2, pairs, 2 * blk),
                                lambda b, jj, r: (b, jj, r, 0, 0, 0))),
        compiler_params=_params("parallel", "parallel", "parallel"),
        name=f"dil_attn_d{d}",
    )(qkv, qkv, qkv, qkv, qkv, qkv, qkv, bias)
    lse = jnp.transpose(lse.reshape(bsz, nb, d, pairs, 2, blk), (0, 1, 5, 2, 3, 4))
    return o, lse.reshape(bsz * s, DIL_HEADS)


def _dil_out_kernel(o0_ref, o1_ref, o2_ref, l0_ref, l1_ref, l2_ref, w_ref, x_ref, o_ref, *, perms):
    width = o0_ref.shape[1]
    lses = [l0_ref[...], l1_ref[...], l2_ref[...]]
    m = jnp.maximum(jnp.maximum(lses[0], lses[1]), lses[2])
    es = [jnp.exp(l - m) for l in lses]
    inv = 1.0 / (es[0] + es[1] + es[2])
    head_row = lax.broadcasted_iota(jnp.int32, (DIL_HEADS, width), 0)
    head_col = lax.broadcasted_iota(jnp.int32, (DIL_HEADS, width), 1) // DIL_HEAD_DIM
    expand = (head_row == head_col).astype(BF16)
    y = None
    for e, o_ref_g, perm in zip(es, (o0_ref, o1_ref, o2_ref), perms):
        w_g = e * inv
        w_hi = w_g.astype(BF16)
        w_lo = (w_g - w_hi.astype(F32)).astype(BF16)
        wts = (jnp.dot(w_hi, expand, preferred_element_type=F32)
               + jnp.dot(w_lo, expand, preferred_element_type=F32))
        o_nat = _permute_rows(o_ref_g[...], *perm, inverse=True)
        term = wts * o_nat.astype(F32)
        y = term if y is None else y + term
    o_ref[...] = x_ref[...] + jnp.dot(y.astype(BF16), w_ref[...], preferred_element_type=F32)


def _dil_out(os_, lses, perms, out_w, x, *, tm=1024):
    t, width = os_[0].shape
    d = out_w.shape[1]
    assert t % tm == 0 and all(tm % pb == 0 for pb, _ in perms) and out_w.dtype == BF16
    row = lambda i: (i, 0)
    return pl.pallas_call(
        functools.partial(_dil_out_kernel, perms=perms),
        out_shape=jax.ShapeDtypeStruct((t, d), F32),
        grid=(t // tm,),
        in_specs=[pl.BlockSpec((tm, width), row)] * 3 + [pl.BlockSpec((tm, DIL_HEADS), row)] * 3
                 + [pl.BlockSpec((width, d), lambda i: (0, 0)), pl.BlockSpec((tm, d), row)],
        out_specs=pl.BlockSpec((tm, d), row),
        compiler_params=_params("parallel"),
        name="dil_out",
    )(*os_, *lses, out_w, x)


def _dil_layer(x, norm_g, qkv_w, out_w, table, bsz, s):
    n_groups = len(DIL_CONFIGS)
    width = DIL_HEADS * DIL_HEAD_DIM
    tn = width
    per_w = width // tn
    os_, lses, perms = [], [], []
    for gi, (win, dil) in enumerate(DIL_CONFIGS):
        perm = ((win // (2 * dil)) * dil, dil)
        qkv = _mm(x, qkv_w, g=norm_g, n_cols=3 * width, perm=perm, out_dtype=BF16, tn=tn,
                  w_col_block=lambda j, gi=gi: ((j // per_w) * n_groups + gi) * per_w + j % per_w,
                  name=f"dil_qkv_d{dil}")
        o, lse = _dil_group(qkv, _dil_bias(table, win, dil), win, dil, bsz, s)
        os_.append(o)
        lses.append(lse)
        perms.append(perm)
    return _dil_out(os_, lses, tuple(perms), out_w, x)


def _diff_attn_kernel(q_ref, k_ref, v_ref, bank_ref, lam_ref, g_ref, o_ref,
                      vt_ref, qt_ref, s_ref, m_ref, acc_ref, *, tq, tk, seq, lam_init):
    qi = pl.program_id(2)
    tile = DIFF_TILE
    hd = DIFF_HEAD_DIM

    @pl.when(qi == 0)
    def _():
        for c in range(seq // tk):
            vt_ref[0:DIFF_V_DIM, c * tk:(c + 1) * tk] = (
                v_ref[c * tk:(c + 1) * tk, :].astype(F32).T.astype(BF16))
        vt_ref[DIFF_V_DIM:, :] = jnp.ones((DIFF_ONES_ROWS, seq), BF16)

    qt = (q_ref[...].astype(F32) * (LOG2_E / math.sqrt(hd))).T
    row = lax.broadcasted_iota(jnp.int32, qt.shape, 0)
    qt_ref[:, 0:tq] = jnp.where(row < hd, qt, 0.0).astype(BF16)
    qt_ref[:, tq:2 * tq] = jnp.where(row >= hd, qt, 0.0).astype(BF16)
    m_ref[...] = jnp.full(m_ref.shape, -jnp.inf, F32)
    acc_ref[...] = jnp.zeros(acc_ref.shape, F32)

    n_chunks = seq // tk

    def scores(kc, slot):
        start = pl.multiple_of(kc * tk, tk)
        s_ref[slot] = jnp.dot(k_ref[pl.ds(start, tk), :], qt_ref[...],
                              preferred_element_type=F32)

    def consume(kc, slot):
        start = pl.multiple_of(kc * tk, tk)
        vt = vt_ref[:, pl.ds(start, tk)]
        for i in range(2):
            rows = []
            for b in range(tk // tile):
                cols = []
                for a in range(tq // tile):
                    dlt = (kc * (tk // tile) + b) - (qi * (tq // tile) + a)
                    idx = jnp.clip(dlt, -DIFF_BANK_REACH, DIFF_BANK_REACH) + DIFF_BANK_REACH
                    lo = i * tq + a * tile
                    cols.append(s_ref[slot, b * tile:(b + 1) * tile, lo:lo + tile]
                                + bank_ref[i, idx])
                rows.append(jnp.concatenate(cols, axis=1))
            s = jnp.concatenate(rows, axis=0)
            m_old = m_ref[i]
            m_new = jnp.maximum(m_old, jnp.max(s, axis=0, keepdims=True))
            alpha = jnp.exp2(m_old - m_new)
            p = jnp.exp2(s - m_new)
            acc_ref[i] = alpha * acc_ref[i] + jnp.dot(vt, p.astype(BF16), preferred_element_type=F32)
            m_ref[i] = m_new

    scores(0, 0)

    def body(j, carry):
        for slot in range(2):
            kc = 2 * j + slot
            scores(kc + 1, 1 - slot)
            consume(kc, slot)
        return carry

    lax.fori_loop(0, n_chunks // 2 - 1, body, 0)
    scores(n_chunks - 1, 1)
    consume(n_chunks - 2, 0)
    consume(n_chunks - 1, 1)

    lf = lam_ref[...]
    lam_full = (jnp.exp(jnp.sum(lf[0:1] * lf[1:2], axis=-1, keepdims=True))
                - jnp.exp(jnp.sum(lf[2:3] * lf[3:4], axis=-1, keepdims=True)) + lam_init)
    nv = DIFF_V_DIM
    ot = (acc_ref[0, 0:nv] / acc_ref[0, nv:nv + 1]
          - lam_full * (acc_ref[1, 0:nv] / acc_ref[1, nv:nv + 1]))
    o = _rms(ot.T, g_ref[...]) * (1.0 - lam_init)
    o_ref[...] = o.astype(o_ref.dtype)


def _diff_attn(qkv, bank, lam, subln_g, lam_init, bsz, s, *, tq=1024, tk=512):
    hw = 2 * DIFF_HEAD_DIM
    H = DIFF_HEADS
    assert s % tq == 0 and s % (2 * tk) == 0 and tq % DIFF_TILE == 0 and tk % DIFF_TILE == 0
    n_bank = bank.shape[1]
    return pl.pallas_call(
        functools.partial(_diff_attn_kernel, tq=tq, tk=tk, seq=s, lam_init=lam_init),
        out_shape=jax.ShapeDtypeStruct((bsz, s, H * DIFF_V_DIM), BF16),
        grid=(bsz, H, s // tq),
        in_specs=[pl.BlockSpec((None, tq, hw), lambda b, h, i: (b, i, h)),
                  pl.BlockSpec((None, s, hw), lambda b, h, i: (b, 0, H + h)),
                  pl.BlockSpec((None, s, DIFF_V_DIM), lambda b, h, i: (b, 0, 2 * H + h)),
                  pl.BlockSpec((2, n_bank, DIFF_TILE, DIFF_TILE), lambda b, h, i: (h, 0, 0, 0)),
                  pl.BlockSpec(lam.shape, lambda b, h, i: (0, 0)),
                  pl.BlockSpec((1, DIFF_V_DIM), lambda b, h, i: (0, 0))],
        out_specs=pl.BlockSpec((None, tq, DIFF_V_DIM), lambda b, h, i: (b, i, h)),
        scratch_shapes=[pltpu.VMEM((DIFF_V_DIM + DIFF_ONES_ROWS, s), BF16),
                        pltpu.VMEM((hw, 2 * tq), BF16),
                        pltpu.VMEM((2, tk, 2 * tq), F32),
                        pltpu.VMEM((2, 1, tq), F32),
                        pltpu.VMEM((2, DIFF_V_DIM + DIFF_ONES_ROWS, tq), F32)],
        compiler_params=_params("parallel", "parallel", "arbitrary"),
        name="diff_attn",
    )(qkv, qkv, qkv, bank, lam.astype(F32), subln_g.reshape(1, DIFF_V_DIM).astype(F32))


def _diff_bank_buckets():
    reach = DIFF_BANK_REACH
    local = np.arange(DIFF_TILE)[:, None] - np.arange(DIFF_TILE)[None, :]
    tiles = np.stack([_rel_bucket_np(local + (t - reach) * DIFF_TILE) for t in range(2 * reach + 1)])
    assert (tiles[0] == tiles[0, 0, 0]).all() and (tiles[-1] == tiles[-1, 0, 0]).all()
    return tiles


def _diff_layer(x, norm_g, qkv_w, lam, subln_g, out_w, table, layer_idx, bsz, s):
    d = x.shape[1]
    lam_init = 0.8 - 0.6 * math.exp(-0.3 * layer_idx)
    qkv = _mm(x, qkv_w, g=norm_g, out_dtype=BF16, tn=qkv_w.shape[1], name="diff_qkv")
    bank = _bias_tiles(table, _diff_bank_buckets(), scale=LOG2_E)
    o = _diff_attn(qkv.reshape(bsz, s, 3 * d), bank, lam, subln_g, lam_init, bsz, s)
    return _mm(o.reshape(bsz * s, d), out_w, res=x, name="diff_out")


def kernel(x, rel_bias, norm_mix_g, norm_mlp_g, mlp_w1, mlp_w2, ssd_in_w, ssd_conv_w, ssd_conv_b, ssd_dt_bias, ssd_a_log, ssd_d, ssd_norm_g, ssd_out_w, dil_qkv_w, dil_out_w, diff_qkv_w, diff_lambda, diff_subln_g, diff_out_w, final_norm_g):
    bsz, s, d = x.shape
    depth = norm_mix_g.shape[0]
    (mlp_w1, mlp_w2, ssd_in_w, ssd_out_w, dil_qkv_w, dil_out_w, diff_qkv_w, diff_out_w) = (
        w.astype(BF16) for w in (mlp_w1, mlp_w2, ssd_in_w, ssd_out_w, dil_qkv_w, dil_out_w,
                                 diff_qkv_w, diff_out_w))
    h = x.reshape(bsz * s, d)
    for i in range(depth):
        kind, j = i % 3, i // 3
        if kind == 0:
            h = _ssd_layer(h, norm_mix_g[i], ssd_in_w[j], ssd_conv_w[j], ssd_conv_b[j],
                           ssd_dt_bias[j], ssd_a_log[j], ssd_d[j], ssd_norm_g[j], ssd_out_w[j], bsz, s)
        elif kind == 1:
            h = _dil_layer(h, norm_mix_g[i], dil_qkv_w[j], dil_out_w[j], rel_bias, bsz, s)
        else:
            h = _diff_layer(h, norm_mix_g[i], diff_qkv_w[j], diff_lambda[j], diff_subln_g[j],
                            diff_out_w[j], rel_bias, i, bsz, s)
        h = _mlp(h, norm_mlp_g[i], mlp_w1[i], mlp_w2[i],
                 final_norm_g if i == depth - 1 else None)
    return h.reshape(bsz, s, d)
```

```python
import functools
import math

import numpy as np
import jax
import jax.numpy as jnp
from jax import lax
from jax.experimental import pallas as pl
from jax.experimental.pallas import tpu as pltpu

F32 = jnp.float32
BF16 = jnp.bfloat16

RMS_EPS = 1e-6
NEG_INF = -1e30
LOG2_E = math.log2(math.e)

V7X_VMEM_BYTES = 64 * 1024 * 1024
VMEM_LIMIT_BYTES = V7X_VMEM_BYTES - 8 * 1024 * 1024

SSD_HEAD_DIM = 64
SSD_GROUPS = 8
SSD_HEADS_PER_GROUP = 4
SSD_D_STATE = 128
SSD_CONV_WIDTH = 5
SSD_CHUNK = 128
SSD_GROUP_WIDTH = SSD_HEADS_PER_GROUP * SSD_HEAD_DIM

DIL_CONFIGS = ((128, 1), (512, 4), (2048, 16))
DIL_HEADS = 16
DIL_HEAD_DIM = 64

DIFF_HEADS = 8
DIFF_HEAD_DIM = 64
DIFF_V_DIM = 128
DIFF_TILE = 128
DIFF_BANK_REACH = 6

REL_BUCKETS = 32
REL_MAX_DIST = 1024
REL_BIAS_HEADS = 16
MASKED_BUCKET = REL_BUCKETS

NT_DIMS = (((1,), (1,)), ((), ()))
TN_DIMS = (((0,), (0,)), ((), ()))


def _params(*semantics):
    return pltpu.CompilerParams(dimension_semantics=semantics,
                                vmem_limit_bytes=VMEM_LIMIT_BYTES)


def _rms(x, g):
    ms = jnp.mean(x * x, axis=-1, keepdims=True)
    return x * lax.rsqrt(ms + RMS_EPS) * g


def _silu(x):
    return x * (1.0 / (1.0 + jnp.exp(-x)))


def _softplus(x):
    return jnp.maximum(x, 0.0) + jnp.log1p(jnp.exp(-jnp.abs(x)))


def _residue_major_perm(pb, d, inverse=False):
    i = lax.broadcasted_iota(jnp.int32, (pb, pb), 0)
    j = lax.broadcasted_iota(jnp.int32, (pb, pb), 1)
    per = pb // d
    if inverse:
        i, j = j, i
    return (j == (i % per) * d + i // per).astype(BF16)


def _permute_rows(x, pb, d, inverse=False):
    if d == 1:
        return x
    pm = _residue_major_perm(pb, d, inverse)
    blocks = [jnp.dot(pm, x[r0:r0 + pb], preferred_element_type=F32).astype(BF16)
              for r0 in range(0, x.shape[0], pb)]
    return blocks[0] if len(blocks) == 1 else jnp.concatenate(blocks, axis=0)


def _mm_kernel(*refs, norm, res, perm, side):
    refs = list(refs)
    x_ref, w_ref = refs[0], refs[1]
    pos = 2
    g_ref = r_ref = ws_ref = os_ref = None
    if norm:
        g_ref = refs[pos]
        pos += 1
    if res:
        r_ref = refs[pos]
        pos += 1
    if side:
        ws_ref = refs[pos]
        pos += 1
    o_ref = refs[pos]
    pos += 1
    if side:
        os_ref = refs[pos]
        pos += 1
    xs_ref = refs[pos]

    @pl.when(pl.program_id(1) == 0)
    def _():
        x = x_ref[...].astype(F32)
        if norm:
            x = _rms(x, g_ref[...])
        x = x.astype(BF16)
        if perm is not None:
            x = _permute_rows(x, *perm)
        xs_ref[...] = x
        if side:
            os_ref[...] = lax.dot_general(ws_ref[...], x, NT_DIMS, preferred_element_type=F32)

    acc = jnp.dot(xs_ref[...], w_ref[...], preferred_element_type=F32)
    if res:
        acc = r_ref[...] + acc
    o_ref[...] = acc.astype(o_ref.dtype)


def _layer_spec(w, layer, block, index):
    if w.ndim == 2:
        return pl.BlockSpec(block, index)
    return pl.BlockSpec((None,) + block, lambda *g: (layer,) + index(*g))


def _mm(x, w, *, layer=0, g=None, res=None, n_cols=None, w_col_block=None, perm=None, side_wt=None,
        out_dtype=F32, tm=1024, tn=1024, name="mm"):
    t, k = x.shape
    n = w.shape[-1] if n_cols is None else n_cols
    tn = min(tn, n)
    tm = min(tm, t)
    assert t % tm == 0 and n % tn == 0 and w.shape[-2] == k and w.dtype == BF16
    if w_col_block is None:
        w_col_block = lambda j: j
    if perm is not None:
        assert tm % perm[0] == 0 and perm[0] % perm[1] == 0
    in_specs = [pl.BlockSpec((tm, k), lambda i, j: (i, 0)),
                _layer_spec(w, layer, (k, tn), lambda i, j: (0, w_col_block(j)))]
    args = [x, w]
    if g is not None:
        in_specs.append(pl.BlockSpec((1, k), lambda i, j: (0, 0)))
        args.append(g.reshape(1, k))
    if res is not None:
        in_specs.append(pl.BlockSpec((tm, tn), lambda i, j: (i, j)))
        args.append(res)
    out_shape = jax.ShapeDtypeStruct((t, n), out_dtype)
    out_specs = pl.BlockSpec((tm, tn), lambda i, j: (i, j))
    if side_wt is not None:
        m = side_wt.shape[0]
        in_specs.append(pl.BlockSpec((m, k), lambda i, j: (0, 0)))
        args.append(side_wt)
        out_shape = (out_shape, jax.ShapeDtypeStruct((m, t), F32))
        out_specs = (out_specs, pl.BlockSpec((m, tm), lambda i, j: (0, i)))
    return pl.pallas_call(
        functools.partial(_mm_kernel, norm=g is not None, res=res is not None, perm=perm,
                          side=side_wt is not None),
        out_shape=out_shape,
        grid=(t // tm, n // tn),
        in_specs=in_specs,
        out_specs=out_specs,
        scratch_shapes=[pltpu.VMEM((tm, k), BF16)],
        compiler_params=_params("parallel", "arbitrary"),
        name=name,
    )(*args)


def _mlp_kernel(*refs, final):
    if final:
        x_ref, g_ref, w1_ref, w2_ref, fg_ref, o_ref, xn_ref, acc_ref = refs
    else:
        x_ref, g_ref, w1_ref, w2_ref, o_ref, xn_ref, acc_ref = refs
        fg_ref = None
    f = pl.program_id(1)

    @pl.when(f == 0)
    def _():
        x = x_ref[...]
        xn_ref[...] = _rms(x, g_ref[...]).astype(BF16)
        acc_ref[...] = x

    h = jnp.dot(xn_ref[...], w1_ref[...], preferred_element_type=F32)
    h = jnp.square(jnp.maximum(h, 0.0)).astype(BF16)
    acc_ref[...] += jnp.dot(h, w2_ref[...], preferred_element_type=F32)

    @pl.when(f == pl.num_programs(1) - 1)
    def _():
        y = acc_ref[...]
        if final:
            y = _rms(y, fg_ref[...])
        o_ref[...] = y


def _mlp(x, g, w1, w2, layer, final_g=None, *, tm=1024, tf=1024):
    t, d = x.shape
    ff = w1.shape[-1]
    assert t % tm == 0 and ff % tf == 0 and w1.dtype == BF16 and w2.dtype == BF16
    final = final_g is not None
    in_specs = [pl.BlockSpec((tm, d), lambda i, f: (i, 0)),
                pl.BlockSpec((1, d), lambda i, f: (0, 0)),
                _layer_spec(w1, layer, (d, tf), lambda i, f: (0, f)),
                _layer_spec(w2, layer, (tf, d), lambda i, f: (f, 0))]
    args = [x, g.reshape(1, d), w1, w2]
    if final:
        in_specs.append(pl.BlockSpec((1, d), lambda i, f: (0, 0)))
        args.append(final_g.reshape(1, d))
    return pl.pallas_call(
        functools.partial(_mlp_kernel, final=final),
        out_shape=jax.ShapeDtypeStruct((t, d), F32),
        grid=(t // tm, ff // tf),
        in_specs=in_specs,
        out_specs=pl.BlockSpec((tm, d), lambda i, f: (i, 0)),
        scratch_shapes=[pltpu.VMEM((tm, d), BF16), pltpu.VMEM((tm, d), F32)],
        compiler_params=_params("parallel", "arbitrary"),
        name="mlp",
    )(*args)


def _rel_bucket_np(rel):
    half = REL_BUCKETS // 2
    max_exact = half // 2
    rel = np.asarray(rel, np.int32)
    ret = np.where(rel > 0, half, 0)
    n = np.abs(rel)
    nf = np.maximum(n, 1).astype(np.float32)
    scale = np.float32(math.log(REL_MAX_DIST / max_exact))
    large = max_exact + (np.log(nf / np.float32(max_exact)) / scale
                         * np.float32(half - max_exact)).astype(np.int32)
    large = np.minimum(large, half - 1)
    return (ret + np.where(n < max_exact, n, large)).astype(np.int32)


def _bias_tiles_kernel(tbl_ref, idx_ref, o_ref, *, scale):
    h = pl.program_id(0)
    idx = idx_ref[...]
    acc = jnp.zeros(idx.shape, F32)
    for b in range(REL_BUCKETS):
        acc = jnp.where(idx == b, tbl_ref[b, h] * scale, acc)
    o_ref[...] = jnp.where(idx == MASKED_BUCKET, NEG_INF, acc)


def _bias_tiles(table, bucket_idx, scale=1.0):
    n, r, c = bucket_idx.shape
    return pl.pallas_call(
        functools.partial(_bias_tiles_kernel, scale=scale),
        out_shape=jax.ShapeDtypeStruct((REL_BIAS_HEADS, n, r, c), F32),
        grid=(REL_BIAS_HEADS,),
        in_specs=[pl.BlockSpec(memory_space=pltpu.SMEM),
                  pl.BlockSpec((n, r, c), lambda h: (0, 0, 0))],
        out_specs=pl.BlockSpec((None, n, r, c), lambda h: (h, 0, 0, 0)),
        compiler_params=_params("parallel"),
        name="bias_tiles",
    )(table.astype(F32), jnp.asarray(bucket_idx))


def _conv_kernel(x_ref, w_ref, b_ref, o_ref, xp_ref, *, rows):
    s, tc = x_ref.shape
    pad = 16
    half = SSD_CONV_WIDTH // 2
    xp_ref[0:pad, :] = jnp.zeros((pad, tc), BF16)
    xp_ref[s + pad:s + 2 * pad, :] = jnp.zeros((pad, tc), BF16)
    xp_ref[pad:s + pad, :] = x_ref[...]
    win = rows + 2 * pad
    side_taps = [k for k in range(SSD_CONV_WIDTH) if k != half]
    out_row = lax.broadcasted_iota(jnp.int32, (rows, win), 0)
    win_row = lax.broadcasted_iota(jnp.int32, (rows, win), 1)
    sel = jnp.concatenate([(win_row == out_row + (pad + k - half)).astype(BF16) for k in side_taps],
                          axis=0)
    w = w_ref[...]
    b = b_ref[...]
    for ch in range(s // rows):
        r0 = ch * rows
        window = xp_ref[r0:r0 + win, :]
        shifted = jnp.dot(sel, window, preferred_element_type=F32)
        acc = b + w[half:half + 1, :] * window[pad:pad + rows].astype(F32)
        for t, k in enumerate(side_taps):
            acc = acc + w[k:k + 1, :] * shifted[t * rows:(t + 1) * rows]
        o_ref[r0:r0 + rows, :] = _silu(acc).astype(o_ref.dtype)


def _ssd_conv(zx, conv_w, conv_b, *, d_inner, tc=256, rows=128):
    bsz, s, _ = zx.shape
    c = conv_w.shape[1]
    off = d_inner // tc
    assert zx.dtype == BF16 and s % rows == 0 and rows % 16 == 0
    return pl.pallas_call(
        functools.partial(_conv_kernel, rows=rows),
        out_shape=jax.ShapeDtypeStruct((bsz, s, c), BF16),
        grid=(bsz, c // tc),
        in_specs=[pl.BlockSpec((None, s, tc), lambda b, j: (b, 0, off + j)),
                  pl.BlockSpec((SSD_CONV_WIDTH, tc), lambda b, j: (0, j)),
                  pl.BlockSpec((1, tc), lambda b, j: (0, j))],
        out_specs=pl.BlockSpec((None, s, tc), lambda b, j: (b, 0, j)),
        scratch_shapes=[pltpu.VMEM((s + 32, tc), BF16)],
        compiler_params=_params("parallel", "parallel"),
        name="ssd_conv",
    )(zx, conv_w, conv_b.reshape(1, c))


def _expand_heads(v, head_of_lane):
    r_heads = v.shape[1]
    out = v[:, r_heads - 1:r_heads]
    for r in range(r_heads - 2, -1, -1):
        out = jnp.where(head_of_lane == r, v[:, r:r + 1], out)
    return out


def _split3(x):
    hi = x.astype(BF16)
    r1 = x - hi.astype(F32)
    mid = r1.astype(BF16)
    lo = (r1 - mid.astype(F32)).astype(BF16)
    return hi, mid, lo


def _ssd_scan_kernel(xf_ref, bf_ref, cf_ref, drf_ref, xb_ref, bb_ref, cb_ref, drb_ref,
                     pr_ref, dsk_ref, yf_ref, yb_ref, sf_ref, sb_ref, *, groups):
    @pl.when(pl.program_id(2) == 0)
    def _():
        sf_ref[...] = jnp.zeros(sf_ref.shape, F32)
        sb_ref[...] = jnp.zeros(sb_ref.shape, F32)

    L, R, P, N = SSD_CHUNK, SSD_HEADS_PER_GROUP, SSD_HEAD_DIM, SSD_D_STATE
    W = SSD_GROUP_WIDTH
    row = lax.broadcasted_iota(jnp.int32, (L, L), 0)
    col = lax.broadcasted_iota(jnp.int32, (L, L), 1)
    head_of_lane = lax.broadcasted_iota(jnp.int32, (1, W), 1) // P
    head_of_lane_x = lax.broadcasted_iota(jnp.int32, (L, W), 1) // P
    head_masks = [(head_of_lane_x == r).astype(BF16) for r in range(R)]
    tris = {False: col <= row, True: col >= row}
    tri_b = {rev: t.astype(BF16) for rev, t in tris.items()}
    tri_tb = {False: (row <= col).astype(BF16), True: (row >= col).astype(BF16)}

    items = []
    for g in range(groups):
        pr = pr_ref[g]
        xs, ns = slice(g * W, (g + 1) * W), slice(g * N, (g + 1) * N)
        items.append(dict(g=g, xs=xs, ns=ns, x=xf_ref, b=bf_ref, c=cf_ref, dr=drf_ref, s=sf_ref,
                          y=yf_ref, rev=False, bias_r=pr[:, 0:1], alog_r=pr[:, 1:2]))
        items.append(dict(g=g, xs=xs, ns=ns, x=xb_ref, b=bb_ref, c=cb_ref, dr=drb_ref, s=sb_ref,
                          y=yb_ref, rev=True, bias_r=pr[:, 2:3], alog_r=pr[:, 3:4]))

    for d in items:
        rev = d["rev"]
        d["dtr"] = _softplus(d["dr"][d["g"] * R:(d["g"] + 1) * R, :] + d["bias_r"])
        adt_r = d["dtr"] * (-LOG2_E * jnp.exp(d["alog_r"]))
        pieces = _split3(adt_r)
        d["acs_c"] = sum(lax.dot_general(tri_b[rev], piece, NT_DIMS, preferred_element_type=F32)
                         for piece in pieces)
        acs3 = jnp.dot(jnp.concatenate(pieces, axis=0), tri_tb[rev], preferred_element_type=F32)
        d["acs_r"] = acs3[0:R] + acs3[R:2 * R] + acs3[2 * R:3 * R]
        bm = d["b"][:, d["ns"]]
        cm = d["c"][:, d["ns"]]
        d["cb"] = lax.dot_general(cm, bm, NT_DIMS, preferred_element_type=F32)
        d["bt"] = bm.astype(F32).T

    for d in items:
        acs_c, acs_r = d["acs_c"], d["acs_r"]
        last_r = acs_r[:, 0:1] if d["rev"] else acs_r[:, L - 1:L]
        acs_k = acs_r - jnp.log2(d["dtr"])
        key_scale = jnp.exp2(last_r - acs_k)
        ws, bts, growth = [], [], []
        for r in range(R):
            acs_b = jnp.broadcast_to(acs_c[:, r:r + 1], (L, L))
            seg = acs_b - acs_k[r:r + 1, :]
            lmat = jnp.exp2(jnp.where(tris[d["rev"]], seg, -jnp.inf))
            ws.append((d["cb"] * lmat).astype(BF16))
            bts.append((d["bt"] * key_scale[r:r + 1, :]).astype(BF16))
            growth.append(jnp.exp2(acs_b))
        d["wb"] = jnp.concatenate([jnp.concatenate(ws, axis=1), jnp.concatenate(bts, axis=1)], axis=0)
        lane_lo = lax.broadcasted_iota(jnp.int32, (L, L), 1) < P
        d["growth"] = jnp.concatenate(
            [jnp.where(lane_lo, growth[r], growth[r + 1]) for r in range(0, R, 2)], axis=1)

    for d in items:
        xb16 = d["x"][:, d["xs"]]
        xstack = jnp.concatenate([xb16 * mask for mask in head_masks], axis=0)
        both = jnp.dot(d["wb"], xstack, preferred_element_type=F32)
        d["yd"], d["upd"] = both[0:L], both[L:L + N]
        d["state"] = d["s"][d["g"]]
        d["y_off"] = jnp.dot(d["c"][:, d["ns"]], d["state"].astype(BF16), preferred_element_type=F32)

    for d in items:
        acs_c = d["acs_c"]
        last_c = acs_c[0:1, :] if d["rev"] else acs_c[L - 1:L, :]
        y = d["yd"] + d["y_off"] * d["growth"]
        d["s"][d["g"]] = d["state"] * _expand_heads(jnp.exp2(last_c), head_of_lane) + d["upd"]
        if not d["rev"]:
            y = y + dsk_ref[d["g"]] * d["x"][:, d["xs"]].astype(F32)
        d["y"][:, d["xs"]] = y.astype(d["y"].dtype)


def _ssd_scan(xbc, dt_t, dt_bias, a_log, d_skip, *, groups=4):
    bsz, s, _ = xbc.shape
    G, R, P, N, L = SSD_GROUPS, SSD_HEADS_PER_GROUP, SSD_HEAD_DIM, SSD_D_STATE, SSD_CHUNK
    assert G % groups == 0 and L == 2 * P
    d_inner = G * R * P
    nc = s // L
    xw = groups * R * P
    nw = groups * N
    b_off = d_inner // nw
    c_off = b_off + G // groups
    gblocks = G // groups
    pr = jnp.stack([dt_bias[0].reshape(G, R), a_log[0].reshape(G, R),
                    dt_bias[1].reshape(G, R), a_log[1].reshape(G, R)], axis=2).astype(F32)
    dsk = jnp.repeat(d_skip.astype(F32).reshape(G, 1, R), P, axis=2)

    def fwd(c):
        return c

    def bwd(c):
        return nc - 1 - c

    def dir_specs(d, cm):
        return [
            pl.BlockSpec((None, L, xw), lambda b, g, c: (b, cm(c), g)),
            pl.BlockSpec((None, L, nw), lambda b, g, c: (b, cm(c), b_off + g)),
            pl.BlockSpec((None, L, nw), lambda b, g, c: (b, cm(c), c_off + g)),
            pl.BlockSpec((groups * R, L), lambda b, g, c: (d * gblocks + g, b * nc + cm(c))),
        ]

    in_specs = dir_specs(0, fwd) + dir_specs(1, bwd) + [
        pl.BlockSpec((groups, R, 4), lambda b, g, c: (g, 0, 0)),
        pl.BlockSpec((groups, 1, R * P), lambda b, g, c: (g, 0, 0)),
    ]
    out_sd = jax.ShapeDtypeStruct((bsz, s, d_inner), BF16)
    return pl.pallas_call(
        functools.partial(_ssd_scan_kernel, groups=groups),
        out_shape=(out_sd, out_sd),
        grid=(bsz, G // groups, nc),
        in_specs=in_specs,
        out_specs=(pl.BlockSpec((None, L, xw), lambda b, g, c: (b, c, g)),
                   pl.BlockSpec((None, L, xw), lambda b, g, c: (b, nc - 1 - c, g))),
        scratch_shapes=[pltpu.VMEM((groups, N, R * P), F32), pltpu.VMEM((groups, N, R * P), F32)],
        compiler_params=_params("parallel", "parallel", "arbitrary"),
        name="ssd_scan",
    )(xbc, xbc, xbc, dt_t, xbc, xbc, xbc, dt_t, pr, dsk)


def _ssd_out_kernel(yf_ref, yb_ref, z_ref, g_ref, w_ref, x_ref, o_ref):
    y = (yf_ref[...].astype(F32) + yb_ref[...].astype(F32)) * _silu(z_ref[...].astype(F32))
    yn = _rms(y, g_ref[...]).astype(BF16)
    o_ref[...] = x_ref[...] + jnp.dot(yn, w_ref[...], preferred_element_type=F32)


def _ssd_out(yf, yb, zx, norm_g, out_w, layer, x, *, tm=512):
    t, d_inner = yf.shape
    d = out_w.shape[-1]
    assert t % tm == 0 and out_w.dtype == BF16
    return pl.pallas_call(
        _ssd_out_kernel,
        out_shape=jax.ShapeDtypeStruct((t, d), F32),
        grid=(t // tm,),
        in_specs=[pl.BlockSpec((tm, d_inner), lambda i: (i, 0)),
                  pl.BlockSpec((tm, d_inner), lambda i: (i, 0)),
                  pl.BlockSpec((tm, d_inner), lambda i: (i, 0)),
                  pl.BlockSpec((1, d_inner), lambda i: (0, 0)),
                  _layer_spec(out_w, layer, (d_inner, d), lambda i: (0, 0)),
                  pl.BlockSpec((tm, d), lambda i: (i, 0))],
        out_specs=pl.BlockSpec((tm, d), lambda i: (i, 0)),
        compiler_params=_params("parallel"),
        name="ssd_out",
    )(yf, yb, zx, norm_g.reshape(1, d_inner), out_w, x)


def _ssd_layer(x, norm_g, in_w, layer, conv_w, conv_b, dt_bias, a_log, d_skip, ssd_norm_g, out_w,
               bsz, s):
    d_inner = ssd_norm_g.shape[0]
    conv_ch = conv_w.shape[1]
    zx_cols = d_inner + conv_ch
    zx, dt_t = _mm(x, in_w, layer=layer, g=norm_g, n_cols=zx_cols, side_wt=in_w[layer, :, zx_cols:].T,
                   out_dtype=BF16, tn=zx_cols // 2, name="ssd_in")
    xbc = _ssd_conv(zx.reshape(bsz, s, zx_cols), conv_w, conv_b, d_inner=d_inner)
    yf, yb = _ssd_scan(xbc, dt_t, dt_bias, a_log, d_skip)
    t = bsz * s
    return _ssd_out(yf.reshape(t, d_inner), yb.reshape(t, d_inner), zx, ssd_norm_g, out_w, layer, x)


def _dil_attn_kernel(q_ref, kp_ref, kc_ref, kn_ref, vp_ref, vc_ref, vn_ref, bias_ref,
                     o_ref, lse_ref, *, n_steps, blk, split_rows):
    j = pl.program_id(1)
    hd = DIL_HEAD_DIM
    first = (j == 0).astype(jnp.int32)
    last = (j == n_steps - 1).astype(jnp.int32)
    lo, hi = slice(0, blk), slice(blk, 2 * blk)
    if split_rows:
        subs = ((lo, ((0, hi), (1, lo), (1, hi)), first),
                (hi, ((1, lo), (1, hi), (2, lo)), 2 * last))
    else:
        subs = tuple((rows, ((0, rows), (1, rows), (2, rows)), first + 2 * last) for rows in (lo, hi))
    k_refs, v_refs = (kp_ref, kc_ref, kn_ref), (vp_ref, vc_ref, vn_ref)
    lane = lax.broadcasted_iota(jnp.int32, (blk, 2 * hd), 1)
    scale = 1.0 / math.sqrt(hd)
    pairs = DIL_HEADS // 2
    tiles = [(si, a, slice(a * 2 * hd, (a + 1) * 2 * hd)) for si in range(2) for a in range(pairs)]
    scores = []
    for si, a, sl in tiles:
        q_rows, kv, variant = subs[si]
        q = (q_ref[q_rows, sl].astype(F32) * scale).astype(BF16)
        zero = jnp.zeros_like(q)
        qbd = jnp.concatenate([jnp.where(lane < hd, q, zero), jnp.where(lane >= hd, q, zero)], axis=0)
        k = jnp.concatenate([k_refs[ri][rows, sl] for ri, rows in kv], axis=0)
        scores.append(lax.dot_general(k, qbd, NT_DIMS, preferred_element_type=F32)
                      + bias_ref[variant, a])
    probs = []
    for (si, a, sl), s in zip(tiles, scores):
        m = jnp.max(s, axis=0, keepdims=True)
        p = jnp.exp(s - m)
        den = jnp.sum(p, axis=0, keepdims=True)
        probs.append((p * (1.0 / den)).astype(BF16))
        lse_ref[si, a:a + 1, :] = m + jnp.log(den)
    for (si, a, sl), pn in zip(tiles, probs):
        q_rows, kv, _ = subs[si]
        v = jnp.concatenate([v_refs[ri][rows, sl] for ri, rows in kv], axis=0)
        r = lax.dot_general(pn, v, TN_DIMS, preferred_element_type=F32)
        o_ref[q_rows, sl] = jnp.where(lane < hd, r[0:blk], r[blk:2 * blk]).astype(o_ref.dtype)


def _dil_bias(table, window, dilation):
    blk = window // (2 * dilation)
    delta = np.arange(3 * blk)[:, None] - blk - np.arange(blk)[None, :]
    bucket = _rel_bucket_np(delta * dilation)
    band = np.abs(delta) <= blk
    key_blk = np.arange(3 * blk)[:, None] // blk
    idx = []
    for variant in range(4):
        ok = band & ~((key_blk == 0) & bool(variant & 1)) & ~((key_blk == 2) & bool(variant & 2))
        idx.append(np.where(ok, bucket, MASKED_BUCKET))
    tiles = _bias_tiles(table, np.stack(idx).astype(np.int32))
    tiles = tiles.reshape(DIL_HEADS // 2, 2, 4, 3 * blk, blk)
    return jnp.transpose(tiles, (2, 0, 3, 1, 4)).reshape(4, DIL_HEADS // 2, 3 * blk, 2 * blk)


def _dil_group(qkv, bias, window, dilation, bsz, s):
    width = DIL_HEADS * DIL_HEAD_DIM
    d = dilation
    n = s // d
    blk = window // (2 * d)
    assert n % blk == 0 and blk % 8 == 0
    nb = n // blk
    pairs = DIL_HEADS // 2
    split_rows = d == 1
    n_steps, n_res = (nb // 2, 1) if split_rows else (nb, d // 2)
    assert (nb if split_rows else d) % 2 == 0

    def spec(which, shift):
        def index(b, jj, r):
            jn = jnp.clip(jj + shift, 0, n_steps - 1)
            return ((b * n_steps + jn) * n_res + r, which)
        return pl.BlockSpec((2 * blk, width), index)

    o, lse = pl.pallas_call(
        functools.partial(_dil_attn_kernel, n_steps=n_steps, blk=blk, split_rows=split_rows),
        out_shape=(jax.ShapeDtypeStruct((bsz * s, width), BF16),
                   jax.ShapeDtypeStruct((bsz, n_steps, n_res, 2, pairs, 2 * blk), F32)),
        grid=(bsz, n_steps, n_res),
        in_specs=[spec(0, 0), spec(1, -1), spec(1, 0), spec(1, 1),
                  spec(2, -1), spec(2, 0), spec(2, 1),
                  pl.BlockSpec((4, pairs, 3 * blk, 2 * blk), lambda b, jj, r: (0, 0, 0, 0))],
        out_specs=(pl.BlockSpec((2 * blk, width), lambda b, jj, r: ((b * n_steps + jj) * n_res + r, 0)),
                   pl.BlockSpec((None, None, None, 2, pairs, 2 * blk),
                                lambda b, jj, r: (b, jj, r, 0, 0, 0))),
        compiler_params=_params("parallel", "parallel", "parallel"),
        name=f"dil_attn_d{d}",
    )(qkv, qkv, qkv, qkv, qkv, qkv, qkv, bias)
    lse = jnp.transpose(lse.reshape(bsz, nb, d, pairs, 2, blk), (0, 1, 5, 2, 3, 4))
    return o, lse.reshape(bsz * s, DIL_HEADS)


def _dil_out_kernel(o0_ref, o1_ref, o2_ref, l0_ref, l1_ref, l2_ref, w_ref, x_ref, o_ref, *, perms):
    width = o0_ref.shape[1]
    lses = [l0_ref[...], l1_ref[...], l2_ref[...]]
    m = jnp.maximum(jnp.maximum(lses[0], lses[1]), lses[2])
    es = [jnp.exp(l - m) for l in lses]
    inv = 1.0 / (es[0] + es[1] + es[2])
    head_row = lax.broadcasted_iota(jnp.int32, (DIL_HEADS, width), 0)
    head_col = lax.broadcasted_iota(jnp.int32, (DIL_HEADS, width), 1) // DIL_HEAD_DIM
    expand = (head_row == head_col).astype(BF16)
    y = None
    for e, o_ref_g, perm in zip(es, (o0_ref, o1_ref, o2_ref), perms):
        w_g = e * inv
        w_hi = w_g.astype(BF16)
        w_lo = (w_g - w_hi.astype(F32)).astype(BF16)
        wts = (jnp.dot(w_hi, expand, preferred_element_type=F32)
               + jnp.dot(w_lo, expand, preferred_element_type=F32))
        o_nat = _permute_rows(o_ref_g[...], *perm, inverse=True)
        term = wts * o_nat.astype(F32)
        y = term if y is None else y + term
    o_ref[...] = x_ref[...] + jnp.dot(y.astype(BF16), w_ref[...], preferred_element_type=F32)


def _dil_out(os_, lses, perms, out_w, layer, x, *, tm=1024):
    t, width = os_[0].shape
    d = out_w.shape[-1]
    assert t % tm == 0 and all(tm % pb == 0 for pb, _ in perms) and out_w.dtype == BF16
    row = lambda i: (i, 0)
    return pl.pallas_call(
        functools.partial(_dil_out_kernel, perms=perms),
        out_shape=jax.ShapeDtypeStruct((t, d), F32),
        grid=(t // tm,),
        in_specs=[pl.BlockSpec((tm, width), row)] * 3 + [pl.BlockSpec((tm, DIL_HEADS), row)] * 3
                 + [_layer_spec(out_w, layer, (width, d), lambda i: (0, 0)), pl.BlockSpec((tm, d), row)],
        out_specs=pl.BlockSpec((tm, d), row),
        compiler_params=_params("parallel"),
        name="dil_out",
    )(*os_, *lses, out_w, x)


def _dil_layer(x, norm_g, qkv_w, out_w, layer, table, bsz, s):
    n_groups = len(DIL_CONFIGS)
    width = DIL_HEADS * DIL_HEAD_DIM
    tn = width
    per_w = width // tn
    os_, lses, perms = [], [], []
    for gi, (win, dil) in enumerate(DIL_CONFIGS):
        perm = ((win // (2 * dil)) * dil, dil)
        qkv = _mm(x, qkv_w, layer=layer, g=norm_g, n_cols=3 * width, perm=perm, out_dtype=BF16, tn=tn,
                  w_col_block=lambda j, gi=gi: ((j // per_w) * n_groups + gi) * per_w + j % per_w,
                  name=f"dil_qkv_d{dil}")
        o, lse = _dil_group(qkv, _dil_bias(table, win, dil), win, dil, bsz, s)
        os_.append(o)
        lses.append(lse)
        perms.append(perm)
    return _dil_out(os_, lses, tuple(perms), out_w, layer, x)


def _diff_attn_kernel(q_ref, k_ref, v_ref, bank_ref, lam_ref, g_ref, o_ref,
                      vx_ref, q2_ref, s_ref, m_ref, acc_ref, *, tq, tk, seq, lam_init):
    qi = pl.program_id(2)
    tile = DIFF_TILE
    hd = DIFF_HEAD_DIM
    nv = DIFF_V_DIM

    @pl.when(qi == 0)
    def _():
        lane = lax.broadcasted_iota(jnp.int32, (seq, nv), 1)
        vx_ref[:, 0:nv] = v_ref[...]
        vx_ref[:, nv:2 * nv] = jnp.where(lane == 0, 1.0, 0.0).astype(BF16)

    q = (q_ref[...].astype(F32) * (LOG2_E / math.sqrt(hd))).astype(BF16)
    lane = lax.broadcasted_iota(jnp.int32, q.shape, 1)
    zero = jnp.zeros_like(q)
    q2_ref[0:tq, :] = jnp.where(lane < hd, q, zero)
    q2_ref[tq:2 * tq, :] = jnp.where(lane >= hd, q, zero)
    m_ref[...] = jnp.full(m_ref.shape, -jnp.inf, F32)
    acc_ref[...] = jnp.zeros(acc_ref.shape, F32)

    n_chunks = seq // tk
    kt = tk // tile

    def scores(kc, slot):
        start = pl.multiple_of(kc * tk, tk)
        s_ref[slot] = lax.dot_general(q2_ref[...], k_ref[pl.ds(start, tk), :], NT_DIMS,
                                      preferred_element_type=F32)

    def consume(kc, slot):
        start = pl.multiple_of(kc * tk, tk)
        vx = vx_ref[pl.ds(start, tk), :]
        for i in range(2):
            rows = []
            for a in range(tq // tile):
                cols = []
                for b in range(kt):
                    dlt = (kc * kt + b) - (qi * (tq // tile) + a)
                    idx = jnp.clip(dlt, -DIFF_BANK_REACH, DIFF_BANK_REACH) + DIFF_BANK_REACH
                    r0 = i * tq + a * tile
                    cols.append(s_ref[slot, r0:r0 + tile, b * tile:(b + 1) * tile]
                                + bank_ref[i, idx])
                rows.append(cols)
            part = jnp.concatenate([functools.reduce(jnp.maximum, cols) for cols in rows], axis=0)
            row_max = jnp.broadcast_to(jnp.max(part, axis=-1, keepdims=True), part.shape)
            m_old = m_ref[i]
            m_new = jnp.maximum(m_old, row_max)
            alpha = jnp.exp2(m_old - m_new)
            p = jnp.concatenate(
                [jnp.concatenate([jnp.exp2(c - m_new[a * tile:(a + 1) * tile]).astype(BF16)
                                  for c in cols], axis=1) for a, cols in enumerate(rows)], axis=0)
            pv = jnp.dot(p, vx, preferred_element_type=F32)
            acc_ref[i] = jnp.concatenate([alpha, alpha], axis=1) * acc_ref[i] + pv
            m_ref[i] = m_new

    scores(0, 0)

    def body(j, carry):
        for slot in range(2):
            kc = 2 * j + slot
            scores(kc + 1, 1 - slot)
            consume(kc, slot)
        return carry

    lax.fori_loop(0, n_chunks // 2 - 1, body, 0)
    scores(n_chunks - 1, 1)
    consume(n_chunks - 2, 0)
    consume(n_chunks - 1, 1)

    lf = lam_ref[...]
    lam_full = (jnp.exp(jnp.sum(lf[0:1] * lf[1:2], axis=-1, keepdims=True))
                - jnp.exp(jnp.sum(lf[2:3] * lf[3:4], axis=-1, keepdims=True)) + lam_init)
    o = (acc_ref[0, :, 0:nv] / acc_ref[0, :, nv:nv + 1]
         - lam_full * (acc_ref[1, :, 0:nv] / acc_ref[1, :, nv:nv + 1]))
    o = _rms(o, g_ref[...]) * (1.0 - lam_init)
    o_ref[...] = o.astype(o_ref.dtype)


def _diff_attn(qkv, bank, lam, subln_g, lam_init, bsz, s, *, tq=1024, tk=1024):
    hw = 2 * DIFF_HEAD_DIM
    H = DIFF_HEADS
    assert s % tq == 0 and s % (2 * tk) == 0 and tq % DIFF_TILE == 0 and tk % DIFF_TILE == 0
    n_bank = bank.shape[1]
    return pl.pallas_call(
        functools.partial(_diff_attn_kernel, tq=tq, tk=tk, seq=s, lam_init=lam_init),
        out_shape=jax.ShapeDtypeStruct((bsz, s, H * DIFF_V_DIM), BF16),
        grid=(bsz, H, s // tq),
        in_specs=[pl.BlockSpec((None, tq, hw), lambda b, h, i: (b, i, h)),
                  pl.BlockSpec((None, s, hw), lambda b, h, i: (b, 0, H + h)),
                  pl.BlockSpec((None, s, DIFF_V_DIM), lambda b, h, i: (b, 0, 2 * H + h)),
                  pl.BlockSpec((2, n_bank, DIFF_TILE, DIFF_TILE), lambda b, h, i: (h, 0, 0, 0)),
                  pl.BlockSpec(lam.shape, lambda b, h, i: (0, 0)),
                  pl.BlockSpec((1, DIFF_V_DIM), lambda b, h, i: (0, 0))],
        out_specs=pl.BlockSpec((None, tq, DIFF_V_DIM), lambda b, h, i: (b, i, h)),
        scratch_shapes=[pltpu.VMEM((s, 2 * DIFF_V_DIM), BF16),
                        pltpu.VMEM((2 * tq, hw), BF16),
                        pltpu.VMEM((2, 2 * tq, tk), F32),
                        pltpu.VMEM((2, tq, DIFF_TILE), F32),
                        pltpu.VMEM((2, tq, 2 * DIFF_V_DIM), F32)],
        compiler_params=_params("parallel", "parallel", "arbitrary"),
        name="diff_attn",
    )(qkv, qkv, qkv, bank, lam.astype(F32), subln_g.reshape(1, DIFF_V_DIM).astype(F32))


def _diff_bank_buckets():
    reach = DIFF_BANK_REACH
    local = np.arange(DIFF_TILE)[None, :] - np.arange(DIFF_TILE)[:, None]
    tiles = np.stack([_rel_bucket_np(local + (t - reach) * DIFF_TILE) for t in range(2 * reach + 1)])
    assert (tiles[0] == tiles[0, 0, 0]).all() and (tiles[-1] == tiles[-1, 0, 0]).all()
    return tiles


def _diff_layer(x, norm_g, qkv_w, lam, subln_g, out_w, layer, table, layer_idx, bsz, s):
    d = x.shape[1]
    lam_init = 0.8 - 0.6 * math.exp(-0.3 * layer_idx)
    qkv = _mm(x, qkv_w, layer=layer, g=norm_g, out_dtype=BF16, tn=qkv_w.shape[-1], name="diff_qkv")
    bank = _bias_tiles(table, _diff_bank_buckets(), scale=LOG2_E)
    o = _diff_attn(qkv.reshape(bsz, s, 3 * d), bank, lam, subln_g, lam_init, bsz, s)
    return _mm(o.reshape(bsz * s, d), out_w, layer=layer, res=x, name="diff_out")


def kernel(x, rel_bias, norm_mix_g, norm_mlp_g, mlp_w1, mlp_w2, ssd_in_w, ssd_conv_w, ssd_conv_b, ssd_dt_bias, ssd_a_log, ssd_d, ssd_norm_g, ssd_out_w, dil_qkv_w, dil_out_w, diff_qkv_w, diff_lambda, diff_subln_g, diff_out_w, final_norm_g):
    bsz, s, d = x.shape
    depth = norm_mix_g.shape[0]
    (mlp_w1, mlp_w2, ssd_in_w, ssd_out_w, dil_qkv_w, dil_out_w, diff_qkv_w, diff_out_w) = (
        w.astype(BF16) for w in (mlp_w1, mlp_w2, ssd_in_w, ssd_out_w, dil_qkv_w, dil_out_w,
                                 diff_qkv_w, diff_out_w))
    h = x.reshape(bsz * s, d)
    for i in range(depth):
        kind, j = i % 3, i // 3
        if kind == 0:
            h = _ssd_layer(h, norm_mix_g[i], ssd_in_w, j, ssd_conv_w[j], ssd_conv_b[j],
                           ssd_dt_bias[j], ssd_a_log[j], ssd_d[j], ssd_norm_g[j], ssd_out_w, bsz, s)
        elif kind == 1:
            h = _dil_layer(h, norm_mix_g[i], dil_qkv_w, dil_out_w, j, rel_bias, bsz, s)
        else:
            h = _diff_layer(h, norm_mix_g[i], diff_qkv_w, diff_lambda[j], diff_subln_g[j],
                            diff_out_w, j, rel_bias, i, bsz, s)
        h = _mlp(h, norm_mlp_g[i], mlp_w1, mlp_w2, i,
                 final_norm_g if i == depth - 1 else None)
    return h.reshape(bsz, s, d)
```

```python
import functools
import math

import numpy as np
import jax
import jax.numpy as jnp
from jax import lax
from jax.experimental import pallas as pl
from jax.experimental.pallas import tpu as pltpu

F32 = jnp.float32
BF16 = jnp.bfloat16

RMS_EPS = 1e-6
NEG_INF = -1e30
LOG2_E = math.log2(math.e)

V7X_VMEM_BYTES = 64 * 1024 * 1024
VMEM_LIMIT_BYTES = V7X_VMEM_BYTES - 8 * 1024 * 1024

SSD_HEAD_DIM = 64
SSD_GROUPS = 8
SSD_HEADS_PER_GROUP = 4
SSD_D_STATE = 128
SSD_CONV_WIDTH = 5
SSD_CHUNK = 128
SSD_GROUP_WIDTH = SSD_HEADS_PER_GROUP * SSD_HEAD_DIM

DIL_CONFIGS = ((128, 1), (512, 4), (2048, 16))
DIL_HEADS = 16
DIL_HEAD_DIM = 64
DIL_TILES_PER_STEP = 4

DIFF_HEADS = 8
DIFF_HEAD_DIM = 64
DIFF_V_DIM = 128
DIFF_TILE = 128
DIFF_BANK_REACH = 6

REL_BUCKETS = 32
REL_MAX_DIST = 1024
REL_BIAS_HEADS = 16
MASKED_BUCKET = REL_BUCKETS

NT_DIMS = (((1,), (1,)), ((), ()))
TN_DIMS = (((0,), (0,)), ((), ()))


def _params(*semantics):
    return pltpu.CompilerParams(dimension_semantics=semantics,
                                vmem_limit_bytes=VMEM_LIMIT_BYTES)


def _rms(x, g):
    ms = jnp.mean(x * x, axis=-1, keepdims=True)
    return x * lax.rsqrt(ms + RMS_EPS) * g


def _silu(x):
    return x * (1.0 / (1.0 + jnp.exp(-x)))


def _softplus(x):
    return jnp.maximum(x, 0.0) + jnp.log1p(jnp.exp(-jnp.abs(x)))


def _residue_major_perm(pb, d, inverse=False):
    i = lax.broadcasted_iota(jnp.int32, (pb, pb), 0)
    j = lax.broadcasted_iota(jnp.int32, (pb, pb), 1)
    per = pb // d
    if inverse:
        i, j = j, i
    return (j == (i % per) * d + i // per).astype(BF16)


def _permute_rows(x, pb, d, inverse=False):
    if d == 1:
        return x
    pm = _residue_major_perm(pb, d, inverse)
    blocks = [jnp.dot(pm, x[r0:r0 + pb], preferred_element_type=F32).astype(BF16)
              for r0 in range(0, x.shape[0], pb)]
    return blocks[0] if len(blocks) == 1 else jnp.concatenate(blocks, axis=0)


def _mm_kernel(*refs, norm, res, perm, side):
    refs = list(refs)
    x_ref, w_ref = refs[0], refs[1]
    pos = 2
    g_ref = r_ref = ws_ref = os_ref = None
    if norm:
        g_ref = refs[pos]
        pos += 1
    if res:
        r_ref = refs[pos]
        pos += 1
    if side:
        ws_ref = refs[pos]
        pos += 1
    o_ref = refs[pos]
    pos += 1
    if side:
        os_ref = refs[pos]
        pos += 1
    xs_ref = refs[pos]

    @pl.when(pl.program_id(1) == 0)
    def _():
        x = x_ref[...].astype(F32)
        if norm:
            x = _rms(x, g_ref[...])
        x = x.astype(BF16)
        if perm is not None:
            x = _permute_rows(x, *perm)
        xs_ref[...] = x
        if side:
            os_ref[...] = lax.dot_general(ws_ref[...], x, NT_DIMS, preferred_element_type=F32)

    acc = jnp.dot(xs_ref[...], w_ref[...], preferred_element_type=F32)
    if res:
        acc = r_ref[...] + acc
    o_ref[...] = acc.astype(o_ref.dtype)


def _layer_spec(w, layer, block, index):
    if w.ndim == 2:
        return pl.BlockSpec(block, index)
    return pl.BlockSpec((None,) + block, lambda *g: (layer,) + index(*g))


def _mm(x, w, *, layer=0, g=None, res=None, n_cols=None, w_col_block=None, perm=None, side_wt=None,
        out_dtype=F32, tm=1024, tn=1024, name="mm"):
    t, k = x.shape
    n = w.shape[-1] if n_cols is None else n_cols
    tn = min(tn, n)
    tm = min(tm, t)
    assert t % tm == 0 and n % tn == 0 and w.shape[-2] == k and w.dtype == BF16
    if w_col_block is None:
        w_col_block = lambda j: j
    if perm is not None:
        assert tm % perm[0] == 0 and perm[0] % perm[1] == 0
    in_specs = [pl.BlockSpec((tm, k), lambda i, j: (i, 0)),
                _layer_spec(w, layer, (k, tn), lambda i, j: (0, w_col_block(j)))]
    args = [x, w]
    if g is not None:
        in_specs.append(pl.BlockSpec((1, k), lambda i, j: (0, 0)))
        args.append(g.reshape(1, k))
    if res is not None:
        in_specs.append(pl.BlockSpec((tm, tn), lambda i, j: (i, j)))
        args.append(res)
    out_shape = jax.ShapeDtypeStruct((t, n), out_dtype)
    out_specs = pl.BlockSpec((tm, tn), lambda i, j: (i, j))
    if side_wt is not None:
        m = side_wt.shape[0]
        in_specs.append(pl.BlockSpec((m, k), lambda i, j: (0, 0)))
        args.append(side_wt)
        out_shape = (out_shape, jax.ShapeDtypeStruct((m, t), F32))
        out_specs = (out_specs, pl.BlockSpec((m, tm), lambda i, j: (0, i)))
    return pl.pallas_call(
        functools.partial(_mm_kernel, norm=g is not None, res=res is not None, perm=perm,
                          side=side_wt is not None),
        out_shape=out_shape,
        grid=(t // tm, n // tn),
        in_specs=in_specs,
        out_specs=out_specs,
        scratch_shapes=[pltpu.VMEM((tm, k), BF16)],
        compiler_params=_params("parallel", "arbitrary"),
        name=name,
    )(*args)


def _mlp_kernel(*refs, final):
    if final:
        x_ref, g_ref, w1_ref, w2_ref, fg_ref, o_ref, xn_ref, acc_ref = refs
    else:
        x_ref, g_ref, w1_ref, w2_ref, o_ref, xn_ref, acc_ref = refs
        fg_ref = None
    f = pl.program_id(1)

    @pl.when(f == 0)
    def _():
        x = x_ref[...]
        xn_ref[...] = _rms(x, g_ref[...]).astype(BF16)
        acc_ref[...] = x

    h = jnp.dot(xn_ref[...], w1_ref[...], preferred_element_type=F32)
    h = jnp.square(jnp.maximum(h, 0.0)).astype(BF16)
    acc_ref[...] += jnp.dot(h, w2_ref[...], preferred_element_type=F32)

    @pl.when(f == pl.num_programs(1) - 1)
    def _():
        y = acc_ref[...]
        if final:
            y = _rms(y, fg_ref[...])
        o_ref[...] = y


def _mlp(x, g, w1, w2, layer, final_g=None, *, tm=1024, tf=1024):
    t, d = x.shape
    ff = w1.shape[-1]
    assert t % tm == 0 and ff % tf == 0 and w1.dtype == BF16 and w2.dtype == BF16
    final = final_g is not None
    in_specs = [pl.BlockSpec((tm, d), lambda i, f: (i, 0)),
                pl.BlockSpec((1, d), lambda i, f: (0, 0)),
                _layer_spec(w1, layer, (d, tf), lambda i, f: (0, f)),
                _layer_spec(w2, layer, (tf, d), lambda i, f: (f, 0))]
    args = [x, g.reshape(1, d), w1, w2]
    if final:
        in_specs.append(pl.BlockSpec((1, d), lambda i, f: (0, 0)))
        args.append(final_g.reshape(1, d))
    return pl.pallas_call(
        functools.partial(_mlp_kernel, final=final),
        out_shape=jax.ShapeDtypeStruct((t, d), F32),
        grid=(t // tm, ff // tf),
        in_specs=in_specs,
        out_specs=pl.BlockSpec((tm, d), lambda i, f: (i, 0)),
        scratch_shapes=[pltpu.VMEM((tm, d), BF16), pltpu.VMEM((tm, d), F32)],
        compiler_params=_params("parallel", "arbitrary"),
        name="mlp",
    )(*args)


def _rel_bucket_np(rel):
    half = REL_BUCKETS // 2
    max_exact = half // 2
    rel = np.asarray(rel, np.int32)
    ret = np.where(rel > 0, half, 0)
    n = np.abs(rel)
    nf = np.maximum(n, 1).astype(np.float32)
    scale = np.float32(math.log(REL_MAX_DIST / max_exact))
    large = max_exact + (np.log(nf / np.float32(max_exact)) / scale
                         * np.float32(half - max_exact)).astype(np.int32)
    large = np.minimum(large, half - 1)
    return (ret + np.where(n < max_exact, n, large)).astype(np.int32)


def _bias_tiles_kernel(tbl_ref, idx_ref, o_ref, *, scale):
    h = pl.program_id(0)
    idx = idx_ref[...]
    acc = jnp.zeros(idx.shape, F32)
    for b in range(REL_BUCKETS):
        acc = jnp.where(idx == b, tbl_ref[b, h] * scale, acc)
    o_ref[...] = jnp.where(idx == MASKED_BUCKET, NEG_INF, acc)


def _bias_tiles(table, bucket_idx, scale=1.0):
    n, r, c = bucket_idx.shape
    return pl.pallas_call(
        functools.partial(_bias_tiles_kernel, scale=scale),
        out_shape=jax.ShapeDtypeStruct((REL_BIAS_HEADS, n, r, c), F32),
        grid=(REL_BIAS_HEADS,),
        in_specs=[pl.BlockSpec(memory_space=pltpu.SMEM),
                  pl.BlockSpec((n, r, c), lambda h: (0, 0, 0))],
        out_specs=pl.BlockSpec((None, n, r, c), lambda h: (h, 0, 0, 0)),
        compiler_params=_params("parallel"),
        name="bias_tiles",
    )(table.astype(F32), jnp.asarray(bucket_idx))


def _conv_kernel(x_ref, w_ref, b_ref, o_ref, xp_ref, *, rows):
    s, tc = x_ref.shape
    pad = 16
    half = SSD_CONV_WIDTH // 2
    xp_ref[0:pad, :] = jnp.zeros((pad, tc), BF16)
    xp_ref[s + pad:s + 2 * pad, :] = jnp.zeros((pad, tc), BF16)
    xp_ref[pad:s + pad, :] = x_ref[...]
    win = rows + 2 * pad
    side_taps = [k for k in range(SSD_CONV_WIDTH) if k != half]
    out_row = lax.broadcasted_iota(jnp.int32, (rows, win), 0)
    win_row = lax.broadcasted_iota(jnp.int32, (rows, win), 1)
    sel = jnp.concatenate([(win_row == out_row + (pad + k - half)).astype(BF16) for k in side_taps],
                          axis=0)
    w = w_ref[...]
    b = b_ref[...]
    for ch in range(s // rows):
        r0 = ch * rows
        window = xp_ref[r0:r0 + win, :]
        shifted = jnp.dot(sel, window, preferred_element_type=F32)
        acc = b + w[half:half + 1, :] * window[pad:pad + rows].astype(F32)
        for t, k in enumerate(side_taps):
            acc = acc + w[k:k + 1, :] * shifted[t * rows:(t + 1) * rows]
        o_ref[r0:r0 + rows, :] = _silu(acc).astype(o_ref.dtype)


def _ssd_conv(zx, conv_w, conv_b, *, d_inner, tc=512, rows=128):
    bsz, s, _ = zx.shape
    c = conv_w.shape[1]
    off = d_inner // tc
    assert zx.dtype == BF16 and s % rows == 0 and rows % 16 == 0
    return pl.pallas_call(
        functools.partial(_conv_kernel, rows=rows),
        out_shape=jax.ShapeDtypeStruct((bsz, s, c), BF16),
        grid=(bsz, c // tc),
        in_specs=[pl.BlockSpec((None, s, tc), lambda b, j: (b, 0, off + j)),
                  pl.BlockSpec((SSD_CONV_WIDTH, tc), lambda b, j: (0, j)),
                  pl.BlockSpec((1, tc), lambda b, j: (0, j))],
        out_specs=pl.BlockSpec((None, s, tc), lambda b, j: (b, 0, j)),
        scratch_shapes=[pltpu.VMEM((s + 32, tc), BF16)],
        compiler_params=_params("parallel", "parallel"),
        name="ssd_conv",
    )(zx, conv_w, conv_b.reshape(1, c))


def _expand_heads(v, head_of_lane):
    r_heads = v.shape[1]
    out = v[:, r_heads - 1:r_heads]
    for r in range(r_heads - 2, -1, -1):
        out = jnp.where(head_of_lane == r, v[:, r:r + 1], out)
    return out


def _split3(x):
    hi = x.astype(BF16)
    r1 = x - hi.astype(F32)
    mid = r1.astype(BF16)
    lo = (r1 - mid.astype(F32)).astype(BF16)
    return hi, mid, lo


def _ssd_scan_kernel(xf_ref, bf_ref, cf_ref, drf_ref, xb_ref, bb_ref, cb_ref, drb_ref,
                     pr_ref, dsk_ref, yf_ref, yb_ref, sf_ref, sb_ref, *, groups):
    @pl.when(pl.program_id(2) == 0)
    def _():
        sf_ref[...] = jnp.zeros(sf_ref.shape, F32)
        sb_ref[...] = jnp.zeros(sb_ref.shape, F32)

    L, R, P, N = SSD_CHUNK, SSD_HEADS_PER_GROUP, SSD_HEAD_DIM, SSD_D_STATE
    W = SSD_GROUP_WIDTH
    row = lax.broadcasted_iota(jnp.int32, (L, L), 0)
    col = lax.broadcasted_iota(jnp.int32, (L, L), 1)
    head_of_lane = lax.broadcasted_iota(jnp.int32, (1, W), 1) // P
    head_of_lane_x = lax.broadcasted_iota(jnp.int32, (L, W), 1) // P
    head_masks = [(head_of_lane_x == r).astype(BF16) for r in range(R)]
    tris = {False: col <= row, True: col >= row}
    tri_b = {rev: t.astype(BF16) for rev, t in tris.items()}
    tri_tb = {False: (row <= col).astype(BF16), True: (row >= col).astype(BF16)}

    items = []
    for g in range(groups):
        pr = pr_ref[g]
        xs, ns = slice(g * W, (g + 1) * W), slice(g * N, (g + 1) * N)
        items.append(dict(g=g, xs=xs, ns=ns, x=xf_ref, b=bf_ref, c=cf_ref, dr=drf_ref, s=sf_ref,
                          y=yf_ref, rev=False, bias_r=pr[:, 0:1], alog_r=pr[:, 1:2]))
        items.append(dict(g=g, xs=xs, ns=ns, x=xb_ref, b=bb_ref, c=cb_ref, dr=drb_ref, s=sb_ref,
                          y=yb_ref, rev=True, bias_r=pr[:, 2:3], alog_r=pr[:, 3:4]))

    for d in items:
        rev = d["rev"]
        d["dtr"] = _softplus(d["dr"][d["g"] * R:(d["g"] + 1) * R, :] + d["bias_r"])
        adt_r = d["dtr"] * (-LOG2_E * jnp.exp(d["alog_r"]))
        pieces = _split3(adt_r)
        d["acs_c"] = sum(lax.dot_general(tri_b[rev], piece, NT_DIMS, preferred_element_type=F32)
                         for piece in pieces)
        acs3 = jnp.dot(jnp.concatenate(pieces, axis=0), tri_tb[rev], preferred_element_type=F32)
        d["acs_r"] = acs3[0:R] + acs3[R:2 * R] + acs3[2 * R:3 * R]
        bm = d["b"][:, d["ns"]]
        cm = d["c"][:, d["ns"]]
        d["cb"] = lax.dot_general(cm, bm, NT_DIMS, preferred_element_type=F32)
        d["bt"] = bm.astype(F32).T

    for d in items:
        acs_c, acs_r = d["acs_c"], d["acs_r"]
        last_r = acs_r[:, 0:1] if d["rev"] else acs_r[:, L - 1:L]
        acs_k = acs_r - jnp.log2(d["dtr"])
        key_scale = jnp.exp2(last_r - acs_k)
        ws, bts, growth = [], [], []
        for r in range(R):
            acs_b = jnp.broadcast_to(acs_c[:, r:r + 1], (L, L))
            seg = acs_b - acs_k[r:r + 1, :]
            lmat = jnp.exp2(jnp.where(tris[d["rev"]], seg, -jnp.inf))
            ws.append((d["cb"] * lmat).astype(BF16))
            bts.append((d["bt"] * key_scale[r:r + 1, :]).astype(BF16))
            growth.append(jnp.exp2(acs_b))
        d["wb"] = jnp.concatenate([jnp.concatenate(ws, axis=1), jnp.concatenate(bts, axis=1)], axis=0)
        lane_lo = lax.broadcasted_iota(jnp.int32, (L, L), 1) < P
        d["growth"] = jnp.concatenate(
            [jnp.where(lane_lo, growth[r], growth[r + 1]) for r in range(0, R, 2)], axis=1)

    for d in items:
        xb16 = d["x"][:, d["xs"]]
        xstack = jnp.concatenate([xb16 * mask for mask in head_masks], axis=0)
        both = jnp.dot(d["wb"], xstack, preferred_element_type=F32)
        d["yd"], d["upd"] = both[0:L], both[L:L + N]
        d["state"] = d["s"][d["g"]]
        d["y_off"] = jnp.dot(d["c"][:, d["ns"]], d["state"].astype(BF16), preferred_element_type=F32)

    for d in items:
        acs_c = d["acs_c"]
        last_c = acs_c[0:1, :] if d["rev"] else acs_c[L - 1:L, :]
        y = d["yd"] + d["y_off"] * d["growth"]
        d["s"][d["g"]] = d["state"] * _expand_heads(jnp.exp2(last_c), head_of_lane) + d["upd"]
        if not d["rev"]:
            y = y + dsk_ref[d["g"]] * d["x"][:, d["xs"]].astype(F32)
        d["y"][:, d["xs"]] = y.astype(d["y"].dtype)


def _ssd_scan(xbc, dt_t, dt_bias, a_log, d_skip, *, groups=8):
    bsz, s, _ = xbc.shape
    G, R, P, N, L = SSD_GROUPS, SSD_HEADS_PER_GROUP, SSD_HEAD_DIM, SSD_D_STATE, SSD_CHUNK
    assert G % groups == 0 and L == 2 * P
    d_inner = G * R * P
    nc = s // L
    xw = groups * R * P
    nw = groups * N
    b_off = d_inner // nw
    c_off = b_off + G // groups
    gblocks = G // groups
    pr = jnp.stack([dt_bias[0].reshape(G, R), a_log[0].reshape(G, R),
                    dt_bias[1].reshape(G, R), a_log[1].reshape(G, R)], axis=2).astype(F32)
    dsk = jnp.repeat(d_skip.astype(F32).reshape(G, 1, R), P, axis=2)

    def fwd(c):
        return c

    def bwd(c):
        return nc - 1 - c

    def dir_specs(d, cm):
        return [
            pl.BlockSpec((None, L, xw), lambda b, g, c: (b, cm(c), g)),
            pl.BlockSpec((None, L, nw), lambda b, g, c: (b, cm(c), b_off + g)),
            pl.BlockSpec((None, L, nw), lambda b, g, c: (b, cm(c), c_off + g)),
            pl.BlockSpec((groups * R, L), lambda b, g, c: (d * gblocks + g, b * nc + cm(c))),
        ]

    in_specs = dir_specs(0, fwd) + dir_specs(1, bwd) + [
        pl.BlockSpec((groups, R, 4), lambda b, g, c: (g, 0, 0)),
        pl.BlockSpec((groups, 1, R * P), lambda b, g, c: (g, 0, 0)),
    ]
    out_sd = jax.ShapeDtypeStruct((bsz, s, d_inner), BF16)
    return pl.pallas_call(
        functools.partial(_ssd_scan_kernel, groups=groups),
        out_shape=(out_sd, out_sd),
        grid=(bsz, G // groups, nc),
        in_specs=in_specs,
        out_specs=(pl.BlockSpec((None, L, xw), lambda b, g, c: (b, c, g)),
                   pl.BlockSpec((None, L, xw), lambda b, g, c: (b, nc - 1 - c, g))),
        scratch_shapes=[pltpu.VMEM((groups, N, R * P), F32), pltpu.VMEM((groups, N, R * P), F32)],
        compiler_params=_params("parallel", "parallel", "arbitrary"),
        name="ssd_scan",
    )(xbc, xbc, xbc, dt_t, xbc, xbc, xbc, dt_t, pr, dsk)


def _ssd_out_kernel(yf_ref, yb_ref, z_ref, g_ref, w_ref, x_ref, o_ref):
    y = (yf_ref[...].astype(F32) + yb_ref[...].astype(F32)) * _silu(z_ref[...].astype(F32))
    yn = _rms(y, g_ref[...]).astype(BF16)
    o_ref[...] = x_ref[...] + jnp.dot(yn, w_ref[...], preferred_element_type=F32)


def _ssd_out(yf, yb, zx, norm_g, out_w, layer, x, *, tm=512):
    t, d_inner = yf.shape
    d = out_w.shape[-1]
    assert t % tm == 0 and out_w.dtype == BF16
    return pl.pallas_call(
        _ssd_out_kernel,
        out_shape=jax.ShapeDtypeStruct((t, d), F32),
        grid=(t // tm,),
        in_specs=[pl.BlockSpec((tm, d_inner), lambda i: (i, 0)),
                  pl.BlockSpec((tm, d_inner), lambda i: (i, 0)),
                  pl.BlockSpec((tm, d_inner), lambda i: (i, 0)),
                  pl.BlockSpec((1, d_inner), lambda i: (0, 0)),
                  _layer_spec(out_w, layer, (d_inner, d), lambda i: (0, 0)),
                  pl.BlockSpec((tm, d), lambda i: (i, 0))],
        out_specs=pl.BlockSpec((tm, d), lambda i: (i, 0)),
        compiler_params=_params("parallel"),
        name="ssd_out",
    )(yf, yb, zx, norm_g.reshape(1, d_inner), out_w, x)


def _ssd_layer(x, norm_g, in_w, layer, conv_w, conv_b, dt_bias, a_log, d_skip, ssd_norm_g, out_w,
               bsz, s):
    d_inner = ssd_norm_g.shape[0]
    conv_ch = conv_w.shape[1]
    zx_cols = d_inner + conv_ch
    zx, dt_t = _mm(x, in_w, layer=layer, g=norm_g, n_cols=zx_cols, side_wt=in_w[layer, :, zx_cols:].T,
                   out_dtype=BF16, tn=zx_cols // 2, name="ssd_in")
    xbc = _ssd_conv(zx.reshape(bsz, s, zx_cols), conv_w, conv_b, d_inner=d_inner)
    yf, yb = _ssd_scan(xbc, dt_t, dt_bias, a_log, d_skip)
    t = bsz * s
    return _ssd_out(yf.reshape(t, d_inner), yb.reshape(t, d_inner), zx, ssd_norm_g, out_w, layer, x)


def _dil_attn_kernel(q_ref, kp_ref, kc_ref, kn_ref, vp_ref, vc_ref, vn_ref, bias_ref,
                     o_ref, lse_ref, *, n_steps, blk, per_step, split_rows):
    j = pl.program_id(1)
    hd = DIL_HEAD_DIM
    first = (j == 0).astype(jnp.int32)
    last = (j == n_steps - 1).astype(jnp.int32)
    piece = lambda n: slice(n * blk, (n + 1) * blk)
    subs = []
    for n in range(per_step):
        if split_rows:
            kv = tuple(divmod(per_step + n + t - 1, per_step) for t in range(3))
            kv = tuple((ri, piece(pi)) for ri, pi in kv)
            variant = (first if n == 0 else 0) + (2 * last if n == per_step - 1 else 0)
        else:
            kv = tuple((ri, piece(n)) for ri in range(3))
            variant = first + 2 * last
        subs.append((piece(n), kv, variant))
    k_refs, v_refs = (kp_ref, kc_ref, kn_ref), (vp_ref, vc_ref, vn_ref)
    lane = lax.broadcasted_iota(jnp.int32, (blk, 2 * hd), 1)
    scale = 1.0 / math.sqrt(hd)
    pairs = DIL_HEADS // 2
    tiles = [(si, a, slice(a * 2 * hd, (a + 1) * 2 * hd))
             for si in range(per_step) for a in range(pairs)]
    scores = []
    for si, a, sl in tiles:
        q_rows, kv, variant = subs[si]
        q = (q_ref[q_rows, sl].astype(F32) * scale).astype(BF16)
        zero = jnp.zeros_like(q)
        qbd = jnp.concatenate([jnp.where(lane < hd, q, zero), jnp.where(lane >= hd, q, zero)], axis=0)
        k = jnp.concatenate([k_refs[ri][rows, sl] for ri, rows in kv], axis=0)
        scores.append(lax.dot_general(k, qbd, NT_DIMS, preferred_element_type=F32)
                      + bias_ref[variant, a])
    probs = []
    for (si, a, sl), s in zip(tiles, scores):
        m = jnp.max(s, axis=0, keepdims=True)
        p = jnp.exp(s - m)
        den = jnp.sum(p, axis=0, keepdims=True)
        probs.append((p * (1.0 / den)).astype(BF16))
        lse_ref[si, a:a + 1, :] = m + jnp.log(den)
    for (si, a, sl), pn in zip(tiles, probs):
        q_rows, kv, _ = subs[si]
        v = jnp.concatenate([v_refs[ri][rows, sl] for ri, rows in kv], axis=0)
        r = lax.dot_general(pn, v, TN_DIMS, preferred_element_type=F32)
        o_ref[q_rows, sl] = jnp.where(lane < hd, r[0:blk], r[blk:2 * blk]).astype(o_ref.dtype)


def _dil_bias(table, window, dilation):
    blk = window // (2 * dilation)
    delta = np.arange(3 * blk)[:, None] - blk - np.arange(blk)[None, :]
    bucket = _rel_bucket_np(delta * dilation)
    band = np.abs(delta) <= blk
    key_blk = np.arange(3 * blk)[:, None] // blk
    idx = []
    for variant in range(4):
        ok = band & ~((key_blk == 0) & bool(variant & 1)) & ~((key_blk == 2) & bool(variant & 2))
        idx.append(np.where(ok, bucket, MASKED_BUCKET))
    tiles = _bias_tiles(table, np.stack(idx).astype(np.int32))
    tiles = tiles.reshape(DIL_HEADS // 2, 2, 4, 3 * blk, blk)
    return jnp.transpose(tiles, (2, 0, 3, 1, 4)).reshape(4, DIL_HEADS // 2, 3 * blk, 2 * blk)


def _dil_group(qkv, bias, window, dilation, bsz, s):
    width = DIL_HEADS * DIL_HEAD_DIM
    d = dilation
    n = s // d
    blk = window // (2 * d)
    assert n % blk == 0 and blk % 8 == 0
    nb = n // blk
    pairs = DIL_HEADS // 2
    per_step = DIL_TILES_PER_STEP
    split_rows = d == 1
    n_steps, n_res = (nb // per_step, 1) if split_rows else (nb, d // per_step)
    assert (nb if split_rows else d) % per_step == 0

    def spec(which, shift):
        def index(b, jj, r):
            jn = jnp.clip(jj + shift, 0, n_steps - 1)
            return ((b * n_steps + jn) * n_res + r, which)
        return pl.BlockSpec((per_step * blk, width), index)

    o, lse = pl.pallas_call(
        functools.partial(_dil_attn_kernel, n_steps=n_steps, blk=blk, per_step=per_step,
                          split_rows=split_rows),
        out_shape=(jax.ShapeDtypeStruct((bsz * s, width), BF16),
                   jax.ShapeDtypeStruct((bsz, n_steps, n_res, per_step, pairs, 2 * blk), F32)),
        grid=(bsz, n_steps, n_res),
        in_specs=[spec(0, 0), spec(1, -1), spec(1, 0), spec(1, 1),
                  spec(2, -1), spec(2, 0), spec(2, 1),
                  pl.BlockSpec((4, pairs, 3 * blk, 2 * blk), lambda b, jj, r: (0, 0, 0, 0))],
        out_specs=(pl.BlockSpec((per_step * blk, width),
                                lambda b, jj, r: ((b * n_steps + jj) * n_res + r, 0)),
                   pl.BlockSpec((None, None, None, per_step, pairs, 2 * blk),
                                lambda b, jj, r: (b, jj, r, 0, 0, 0))),
        compiler_params=_params("parallel", "parallel", "parallel"),
        name=f"dil_attn_d{d}",
    )(qkv, qkv, qkv, qkv, qkv, qkv, qkv, bias)
    lse = jnp.transpose(lse.reshape(bsz, nb, d, pairs, 2, blk), (0, 1, 5, 2, 3, 4))
    return o, lse.reshape(bsz * s, DIL_HEADS)


def _dil_out_kernel(o0_ref, o1_ref, o2_ref, l0_ref, l1_ref, l2_ref, w_ref, x_ref, o_ref, *, perms):
    width = o0_ref.shape[1]
    lses = [l0_ref[...], l1_ref[...], l2_ref[...]]
    m = jnp.maximum(jnp.maximum(lses[0], lses[1]), lses[2])
    es = [jnp.exp(l - m) for l in lses]
    inv = 1.0 / (es[0] + es[1] + es[2])
    head_row = lax.broadcasted_iota(jnp.int32, (DIL_HEADS, width), 0)
    head_col = lax.broadcasted_iota(jnp.int32, (DIL_HEADS, width), 1) // DIL_HEAD_DIM
    expand = (head_row == head_col).astype(BF16)
    y = None
    for e, o_ref_g, perm in zip(es, (o0_ref, o1_ref, o2_ref), perms):
        w_g = e * inv
        w_hi = w_g.astype(BF16)
        w_lo = (w_g - w_hi.astype(F32)).astype(BF16)
        wts = (jnp.dot(w_hi, expand, preferred_element_type=F32)
               + jnp.dot(w_lo, expand, preferred_element_type=F32))
        o_nat = _permute_rows(o_ref_g[...], *perm, inverse=True)
        term = wts * o_nat.astype(F32)
        y = term if y is None else y + term
    o_ref[...] = x_ref[...] + jnp.dot(y.astype(BF16), w_ref[...], preferred_element_type=F32)


def _dil_out(os_, lses, perms, out_w, layer, x, *, tm=1024):
    t, width = os_[0].shape
    d = out_w.shape[-1]
    assert t % tm == 0 and all(tm % pb == 0 for pb, _ in perms) and out_w.dtype == BF16
    row = lambda i: (i, 0)
    return pl.pallas_call(
        functools.partial(_dil_out_kernel, perms=perms),
        out_shape=jax.ShapeDtypeStruct((t, d), F32),
        grid=(t // tm,),
        in_specs=[pl.BlockSpec((tm, width), row)] * 3 + [pl.BlockSpec((tm, DIL_HEADS), row)] * 3
                 + [_layer_spec(out_w, layer, (width, d), lambda i: (0, 0)), pl.BlockSpec((tm, d), row)],
        out_specs=pl.BlockSpec((tm, d), row),
        compiler_params=_params("parallel"),
        name="dil_out",
    )(*os_, *lses, out_w, x)


def _dil_layer(x, norm_g, qkv_w, out_w, layer, table, bsz, s):
    n_groups = len(DIL_CONFIGS)
    width = DIL_HEADS * DIL_HEAD_DIM
    tn = width
    per_w = width // tn
    os_, lses, perms = [], [], []
    for gi, (win, dil) in enumerate(DIL_CONFIGS):
        perm = ((win // (2 * dil)) * dil, dil)
        qkv = _mm(x, qkv_w, layer=layer, g=norm_g, n_cols=3 * width, perm=perm, out_dtype=BF16, tn=tn,
                  w_col_block=lambda j, gi=gi: ((j // per_w) * n_groups + gi) * per_w + j % per_w,
                  name=f"dil_qkv_d{dil}")
        o, lse = _dil_group(qkv, _dil_bias(table, win, dil), win, dil, bsz, s)
        os_.append(o)
        lses.append(lse)
        perms.append(perm)
    return _dil_out(os_, lses, tuple(perms), out_w, layer, x)


def _diff_attn_kernel(q_ref, k_ref, v_ref, bank_ref, lam_ref, g_ref, o_ref,
                      vx_ref, q2_ref, s_ref, m_ref, acc_ref, *, tq, tk, seq, lam_init):
    qi = pl.program_id(2)
    tile = DIFF_TILE
    hd = DIFF_HEAD_DIM
    nv = DIFF_V_DIM

    @pl.when(qi == 0)
    def _():
        lane = lax.broadcasted_iota(jnp.int32, (seq, nv), 1)
        vx_ref[:, 0:nv] = v_ref[...]
        vx_ref[:, nv:2 * nv] = jnp.where(lane == 0, 1.0, 0.0).astype(BF16)

    q = (q_ref[...].astype(F32) * (LOG2_E / math.sqrt(hd))).astype(BF16)
    lane = lax.broadcasted_iota(jnp.int32, q.shape, 1)
    zero = jnp.zeros_like(q)
    q2_ref[0:tq, :] = jnp.where(lane < hd, q, zero)
    q2_ref[tq:2 * tq, :] = jnp.where(lane >= hd, q, zero)
    m_ref[...] = jnp.full(m_ref.shape, -jnp.inf, F32)
    acc_ref[...] = jnp.zeros(acc_ref.shape, F32)

    n_chunks = seq // tk
    kt = tk // tile

    def scores(kc, slot):
        start = pl.multiple_of(kc * tk, tk)
        s_ref[slot] = lax.dot_general(q2_ref[...], k_ref[pl.ds(start, tk), :], NT_DIMS,
                                      preferred_element_type=F32)

    def consume(kc, slot):
        start = pl.multiple_of(kc * tk, tk)
        vx = vx_ref[pl.ds(start, tk), :]
        for i in range(2):
            rows = []
            for a in range(tq // tile):
                cols = []
                for b in range(kt):
                    dlt = (kc * kt + b) - (qi * (tq // tile) + a)
                    idx = jnp.clip(dlt, -DIFF_BANK_REACH, DIFF_BANK_REACH) + DIFF_BANK_REACH
                    r0 = i * tq + a * tile
                    cols.append(s_ref[slot, r0:r0 + tile, b * tile:(b + 1) * tile]
                                + bank_ref[i, idx])
                rows.append(cols)
            part = jnp.concatenate([functools.reduce(jnp.maximum, cols) for cols in rows], axis=0)
            row_max = jnp.broadcast_to(jnp.max(part, axis=-1, keepdims=True), part.shape)
            m_old = m_ref[i]
            m_new = jnp.maximum(m_old, row_max)
            alpha = jnp.exp2(m_old - m_new)
            p = jnp.concatenate(
                [jnp.concatenate([jnp.exp2(c - m_new[a * tile:(a + 1) * tile]).astype(BF16)
                                  for c in cols], axis=1) for a, cols in enumerate(rows)], axis=0)
            pv = jnp.dot(p, vx, preferred_element_type=F32)
            acc_ref[i] = jnp.concatenate([alpha, alpha], axis=1) * acc_ref[i] + pv
            m_ref[i] = m_new

    scores(0, 0)

    def body(j, carry):
        for slot in range(2):
            kc = 2 * j + slot
            scores(kc + 1, 1 - slot)
            consume(kc, slot)
        return carry

    lax.fori_loop(0, n_chunks // 2 - 1, body, 0)
    scores(n_chunks - 1, 1)
    consume(n_chunks - 2, 0)
    consume(n_chunks - 1, 1)

    lf = lam_ref[...]
    lam_full = (jnp.exp(jnp.sum(lf[0:1] * lf[1:2], axis=-1, keepdims=True))
                - jnp.exp(jnp.sum(lf[2:3] * lf[3:4], axis=-1, keepdims=True)) + lam_init)
    o = (acc_ref[0, :, 0:nv] / acc_ref[0, :, nv:nv + 1]
         - lam_full * (acc_ref[1, :, 0:nv] / acc_ref[1, :, nv:nv + 1]))
    o = _rms(o, g_ref[...]) * (1.0 - lam_init)
    o_ref[...] = o.astype(o_ref.dtype)


def _diff_attn(qkv, bank, lam, subln_g, lam_init, bsz, s, *, tq=1024, tk=1024):
    hw = 2 * DIFF_HEAD_DIM
    H = DIFF_HEADS
    assert s % tq == 0 and s % (2 * tk) == 0 and tq % DIFF_TILE == 0 and tk % DIFF_TILE == 0
    n_bank = bank.shape[1]
    return pl.pallas_call(
        functools.partial(_diff_attn_kernel, tq=tq, tk=tk, seq=s, lam_init=lam_init),
        out_shape=jax.ShapeDtypeStruct((bsz, s, H * DIFF_V_DIM), BF16),
        grid=(bsz, H, s // tq),
        in_specs=[pl.BlockSpec((None, tq, hw), lambda b, h, i: (b, i, h)),
                  pl.BlockSpec((None, s, hw), lambda b, h, i: (b, 0, H + h)),
                  pl.BlockSpec((None, s, DIFF_V_DIM), lambda b, h, i: (b, 0, 2 * H + h)),
                  pl.BlockSpec((2, n_bank, DIFF_TILE, DIFF_TILE), lambda b, h, i: (h, 0, 0, 0)),
                  pl.BlockSpec(lam.shape, lambda b, h, i: (0, 0)),
                  pl.BlockSpec((1, DIFF_V_DIM), lambda b, h, i: (0, 0))],
        out_specs=pl.BlockSpec((None, tq, DIFF_V_DIM), lambda b, h, i: (b, i, h)),
        scratch_shapes=[pltpu.VMEM((s, 2 * DIFF_V_DIM), BF16),
                        pltpu.VMEM((2 * tq, hw), BF16),
                        pltpu.VMEM((2, 2 * tq, tk), F32),
                        pltpu.VMEM((2, tq, DIFF_TILE), F32),
                        pltpu.VMEM((2, tq, 2 * DIFF_V_DIM), F32)],
        compiler_params=_params("parallel", "parallel", "arbitrary"),
        name="diff_attn",
    )(qkv, qkv, qkv, bank, lam.astype(F32), subln_g.reshape(1, DIFF_V_DIM).astype(F32))


def _diff_bank_buckets():
    reach = DIFF_BANK_REACH
    local = np.arange(DIFF_TILE)[None, :] - np.arange(DIFF_TILE)[:, None]
    tiles = np.stack([_rel_bucket_np(local + (t - reach) * DIFF_TILE) for t in range(2 * reach + 1)])
    assert (tiles[0] == tiles[0, 0, 0]).all() and (tiles[-1] == tiles[-1, 0, 0]).all()
    return tiles


def _diff_layer(x, norm_g, qkv_w, lam, subln_g, out_w, layer, table, layer_idx, bsz, s):
    d = x.shape[1]
    lam_init = 0.8 - 0.6 * math.exp(-0.3 * layer_idx)
    qkv = _mm(x, qkv_w, layer=layer, g=norm_g, out_dtype=BF16, tn=qkv_w.shape[-1], name="diff_qkv")
    bank = _bias_tiles(table, _diff_bank_buckets(), scale=LOG2_E)
    o = _diff_attn(qkv.reshape(bsz, s, 3 * d), bank, lam, subln_g, lam_init, bsz, s)
    return _mm(o.reshape(bsz * s, d), out_w, layer=layer, res=x, name="diff_out")


def kernel(x, rel_bias, norm_mix_g, norm_mlp_g, mlp_w1, mlp_w2, ssd_in_w, ssd_conv_w, ssd_conv_b, ssd_dt_bias, ssd_a_log, ssd_d, ssd_norm_g, ssd_out_w, dil_qkv_w, dil_out_w, diff_qkv_w, diff_lambda, diff_subln_g, diff_out_w, final_norm_g):
    bsz, s, d = x.shape
    depth = norm_mix_g.shape[0]
    (mlp_w1, mlp_w2, ssd_in_w, ssd_out_w, dil_qkv_w, dil_out_w, diff_qkv_w, diff_out_w) = (
        w.astype(BF16) for w in (mlp_w1, mlp_w2, ssd_in_w, ssd_out_w, dil_qkv_w, dil_out_w,
                                 diff_qkv_w, diff_out_w))
    h = x.reshape(bsz * s, d)
    for i in range(depth):
        kind, j = i % 3, i // 3
        if kind == 0:
            h = _ssd_layer(h, norm_mix_g[i], ssd_in_w, j, ssd_conv_w[j], ssd_conv_b[j],
                           ssd_dt_bias[j], ssd_a_log[j], ssd_d[j], ssd_norm_g[j], ssd_out_w, bsz, s)
        elif kind == 1:
            h = _dil_layer(h, norm_mix_g[i], dil_qkv_w, dil_out_w, j, rel_bias, bsz, s)
        else:
            h = _diff_layer(h, norm_mix_g[i], diff_qkv_w, diff_lambda[j], diff_subln_g[j],
                            diff_out_w, j, rel_bias, i, bsz, s)
        h = _mlp(h, norm_mlp_g[i], mlp_w1, mlp_w2, i,
                 final_norm_g if i == depth - 1 else None)
    return h.reshape(bsz, s, d)
```

```python
import functools
import math

import numpy as np
import jax
import jax.numpy as jnp
from jax import lax
from jax.experimental import pallas as pl
from jax.experimental.pallas import tpu as pltpu

F32 = jnp.float32
BF16 = jnp.bfloat16

RMS_EPS = 1e-6
NEG_INF = -1e30
LOG2_E = math.log2(math.e)

V7X_VMEM_BYTES = 64 * 1024 * 1024
VMEM_LIMIT_BYTES = V7X_VMEM_BYTES - 8 * 1024 * 1024

SSD_HEAD_DIM = 64
SSD_GROUPS = 8
SSD_HEADS_PER_GROUP = 4
SSD_D_STATE = 128
SSD_CONV_WIDTH = 5
SSD_CHUNK = 128
SSD_GROUP_WIDTH = SSD_HEADS_PER_GROUP * SSD_HEAD_DIM

DIL_CONFIGS = ((128, 1), (512, 4), (2048, 16))
DIL_HEADS = 16
DIL_HEAD_DIM = 64
DIL_TILES_PER_STEP = 8

DIFF_HEADS = 8
DIFF_HEAD_DIM = 64
DIFF_V_DIM = 128
DIFF_TILE = 128
DIFF_BANK_REACH = 6

REL_BUCKETS = 32
REL_MAX_DIST = 1024
REL_BIAS_HEADS = 16
MASKED_BUCKET = REL_BUCKETS

NT_DIMS = (((1,), (1,)), ((), ()))
TN_DIMS = (((0,), (0,)), ((), ()))


def _params(*semantics):
    return pltpu.CompilerParams(dimension_semantics=semantics,
                                vmem_limit_bytes=VMEM_LIMIT_BYTES)


def _rms(x, g):
    ms = jnp.mean(x * x, axis=-1, keepdims=True)
    return x * lax.rsqrt(ms + RMS_EPS) * g


def _silu(x):
    return x * (1.0 / (1.0 + jnp.exp(-x)))


def _softplus(x):
    return jnp.maximum(x, 0.0) + jnp.log1p(jnp.exp(-jnp.abs(x)))


def _residue_major_perm(pb, d, inverse=False):
    i = lax.broadcasted_iota(jnp.int32, (pb, pb), 0)
    j = lax.broadcasted_iota(jnp.int32, (pb, pb), 1)
    per = pb // d
    if inverse:
        i, j = j, i
    return (j == (i % per) * d + i // per).astype(BF16)


def _permute_rows(x, pb, d, inverse=False, out_dtype=BF16):
    if d == 1:
        return x.astype(out_dtype)
    pm = _residue_major_perm(pb, d, inverse)
    blocks = [jnp.dot(pm, x[r0:r0 + pb], preferred_element_type=F32).astype(out_dtype)
              for r0 in range(0, x.shape[0], pb)]
    return blocks[0] if len(blocks) == 1 else jnp.concatenate(blocks, axis=0)


def _mm_kernel(*refs, norm, res, perm, side):
    refs = list(refs)
    x_ref, w_ref = refs[0], refs[1]
    pos = 2
    g_ref = r_ref = ws_ref = os_ref = None
    if norm:
        g_ref = refs[pos]
        pos += 1
    if res:
        r_ref = refs[pos]
        pos += 1
    if side:
        ws_ref = refs[pos]
        pos += 1
    o_ref = refs[pos]
    pos += 1
    if side:
        os_ref = refs[pos]
        pos += 1
    xs_ref = refs[pos]

    @pl.when(pl.program_id(1) == 0)
    def _():
        x = x_ref[...].astype(F32)
        if norm:
            x = _rms(x, g_ref[...])
        x = x.astype(BF16)
        if perm is not None:
            x = _permute_rows(x, *perm)
        xs_ref[...] = x
        if side:
            os_ref[...] = lax.dot_general(ws_ref[...], x, NT_DIMS, preferred_element_type=F32)

    acc = jnp.dot(xs_ref[...], w_ref[...], preferred_element_type=F32)
    if res:
        acc = r_ref[...] + acc
    o_ref[...] = acc.astype(o_ref.dtype)


def _layer_spec(w, layer, block, index):
    if w.ndim == 2:
        return pl.BlockSpec(block, index)
    return pl.BlockSpec((None,) + block, lambda *g: (layer,) + index(*g))


def _mm(x, w, *, layer=0, g=None, res=None, n_cols=None, w_col_block=None, perm=None, side_wt=None,
        out_dtype=F32, tm=1024, tn=1024, name="mm"):
    t, k = x.shape
    n = w.shape[-1] if n_cols is None else n_cols
    tn = min(tn, n)
    tm = min(tm, t)
    assert t % tm == 0 and n % tn == 0 and w.shape[-2] == k and w.dtype == BF16
    if w_col_block is None:
        w_col_block = lambda j: j
    if perm is not None:
        assert tm % perm[0] == 0 and perm[0] % perm[1] == 0
    in_specs = [pl.BlockSpec((tm, k), lambda i, j: (i, 0)),
                _layer_spec(w, layer, (k, tn), lambda i, j: (0, w_col_block(j)))]
    args = [x, w]
    if g is not None:
        in_specs.append(pl.BlockSpec((1, k), lambda i, j: (0, 0)))
        args.append(g.reshape(1, k))
    if res is not None:
        in_specs.append(pl.BlockSpec((tm, tn), lambda i, j: (i, j)))
        args.append(res)
    out_shape = jax.ShapeDtypeStruct((t, n), out_dtype)
    out_specs = pl.BlockSpec((tm, tn), lambda i, j: (i, j))
    if side_wt is not None:
        m = side_wt.shape[0]
        in_specs.append(pl.BlockSpec((m, k), lambda i, j: (0, 0)))
        args.append(side_wt)
        out_shape = (out_shape, jax.ShapeDtypeStruct((m, t), F32))
        out_specs = (out_specs, pl.BlockSpec((m, tm), lambda i, j: (0, i)))
    return pl.pallas_call(
        functools.partial(_mm_kernel, norm=g is not None, res=res is not None, perm=perm,
                          side=side_wt is not None),
        out_shape=out_shape,
        grid=(t // tm, n // tn),
        in_specs=in_specs,
        out_specs=out_specs,
        scratch_shapes=[pltpu.VMEM((tm, k), BF16)],
        compiler_params=_params("parallel", "arbitrary"),
        name=name,
    )(*args)


def _mlp_kernel(*refs, final):
    if final:
        x_ref, g_ref, w1_ref, w2_ref, fg_ref, o_ref, xn_ref, acc_ref = refs
    else:
        x_ref, g_ref, w1_ref, w2_ref, o_ref, xn_ref, acc_ref = refs
        fg_ref = None
    f = pl.program_id(1)

    @pl.when(f == 0)
    def _():
        x = x_ref[...]
        xn_ref[...] = _rms(x, g_ref[...]).astype(BF16)
        acc_ref[...] = x

    h = jnp.dot(xn_ref[...], w1_ref[...], preferred_element_type=F32)
    h = jnp.square(jnp.maximum(h, 0.0)).astype(BF16)
    acc_ref[...] += jnp.dot(h, w2_ref[...], preferred_element_type=F32)

    @pl.when(f == pl.num_programs(1) - 1)
    def _():
        y = acc_ref[...]
        if final:
            y = _rms(y, fg_ref[...])
        o_ref[...] = y


def _mlp(x, g, w1, w2, layer, final_g=None, *, tm=1024, tf=1024):
    t, d = x.shape
    ff = w1.shape[-1]
    assert t % tm == 0 and ff % tf == 0 and w1.dtype == BF16 and w2.dtype == BF16
    final = final_g is not None
    in_specs = [pl.BlockSpec((tm, d), lambda i, f: (i, 0)),
                pl.BlockSpec((1, d), lambda i, f: (0, 0)),
                _layer_spec(w1, layer, (d, tf), lambda i, f: (0, f)),
                _layer_spec(w2, layer, (tf, d), lambda i, f: (f, 0))]
    args = [x, g.reshape(1, d), w1, w2]
    if final:
        in_specs.append(pl.BlockSpec((1, d), lambda i, f: (0, 0)))
        args.append(final_g.reshape(1, d))
    return pl.pallas_call(
        functools.partial(_mlp_kernel, final=final),
        out_shape=jax.ShapeDtypeStruct((t, d), F32),
        grid=(t // tm, ff // tf),
        in_specs=in_specs,
        out_specs=pl.BlockSpec((tm, d), lambda i, f: (i, 0)),
        scratch_shapes=[pltpu.VMEM((tm, d), BF16), pltpu.VMEM((tm, d), F32)],
        compiler_params=_params("parallel", "arbitrary"),
        name="mlp",
    )(*args)


def _rel_bucket_np(rel):
    half = REL_BUCKETS // 2
    max_exact = half // 2
    rel = np.asarray(rel, np.int32)
    ret = np.where(rel > 0, half, 0)
    n = np.abs(rel)
    nf = np.maximum(n, 1).astype(np.float32)
    scale = np.float32(math.log(REL_MAX_DIST / max_exact))
    large = max_exact + (np.log(nf / np.float32(max_exact)) / scale
                         * np.float32(half - max_exact)).astype(np.int32)
    large = np.minimum(large, half - 1)
    return (ret + np.where(n < max_exact, n, large)).astype(np.int32)


def _bias_tiles_kernel(tbl_ref, idx_ref, o_ref, *, scale):
    h = pl.program_id(0)
    idx = idx_ref[...]
    acc = jnp.zeros(idx.shape, F32)
    for b in range(REL_BUCKETS):
        acc = jnp.where(idx == b, tbl_ref[b, h] * scale, acc)
    o_ref[...] = jnp.where(idx == MASKED_BUCKET, NEG_INF, acc)


def _bias_tiles(table, bucket_idx, scale=1.0):
    n, r, c = bucket_idx.shape
    return pl.pallas_call(
        functools.partial(_bias_tiles_kernel, scale=scale),
        out_shape=jax.ShapeDtypeStruct((REL_BIAS_HEADS, n, r, c), F32),
        grid=(REL_BIAS_HEADS,),
        in_specs=[pl.BlockSpec(memory_space=pltpu.SMEM),
                  pl.BlockSpec((n, r, c), lambda h: (0, 0, 0))],
        out_specs=pl.BlockSpec((None, n, r, c), lambda h: (h, 0, 0, 0)),
        compiler_params=_params("parallel"),
        name="bias_tiles",
    )(table.astype(F32), jnp.asarray(bucket_idx))


def _conv_kernel(x_ref, w_ref, b_ref, o_ref, xp_ref, *, rows):
    s, tc = x_ref.shape
    pad = 16
    half = SSD_CONV_WIDTH // 2
    xp_ref[0:pad, :] = jnp.zeros((pad, tc), BF16)
    xp_ref[s + pad:s + 2 * pad, :] = jnp.zeros((pad, tc), BF16)
    xp_ref[pad:s + pad, :] = x_ref[...]
    win = rows + 2 * pad
    side_taps = [k for k in range(SSD_CONV_WIDTH) if k != half]
    out_row = lax.broadcasted_iota(jnp.int32, (rows, win), 0)
    win_row = lax.broadcasted_iota(jnp.int32, (rows, win), 1)
    sel = jnp.concatenate([(win_row == out_row + (pad + k - half)).astype(BF16) for k in side_taps],
                          axis=0)
    w = w_ref[...]
    b = b_ref[...]
    for ch in range(s // rows):
        r0 = ch * rows
        window = xp_ref[r0:r0 + win, :]
        shifted = jnp.dot(sel, window, preferred_element_type=F32)
        acc = b + w[half:half + 1, :] * window[pad:pad + rows].astype(F32)
        for t, k in enumerate(side_taps):
            acc = acc + w[k:k + 1, :] * shifted[t * rows:(t + 1) * rows]
        o_ref[r0:r0 + rows, :] = _silu(acc).astype(o_ref.dtype)


def _ssd_conv(zx, conv_w, conv_b, *, d_inner, tc=256, rows=128):
    bsz, s, _ = zx.shape
    c = conv_w.shape[1]
    off = d_inner // tc
    assert zx.dtype == BF16 and s % rows == 0 and rows % 16 == 0
    return pl.pallas_call(
        functools.partial(_conv_kernel, rows=rows),
        out_shape=jax.ShapeDtypeStruct((bsz, s, c), BF16),
        grid=(bsz, c // tc),
        in_specs=[pl.BlockSpec((None, s, tc), lambda b, j: (b, 0, off + j)),
                  pl.BlockSpec((SSD_CONV_WIDTH, tc), lambda b, j: (0, j)),
                  pl.BlockSpec((1, tc), lambda b, j: (0, j))],
        out_specs=pl.BlockSpec((None, s, tc), lambda b, j: (b, 0, j)),
        scratch_shapes=[pltpu.VMEM((s + 32, tc), BF16)],
        compiler_params=_params("parallel", "parallel"),
        name="ssd_conv",
    )(zx, conv_w, conv_b.reshape(1, c))


def _expand_heads(v, head_of_lane):
    r_heads = v.shape[1]
    out = v[:, r_heads - 1:r_heads]
    for r in range(r_heads - 2, -1, -1):
        out = jnp.where(head_of_lane == r, v[:, r:r + 1], out)
    return out


def _split3(x):
    hi = x.astype(BF16)
    r1 = x - hi.astype(F32)
    mid = r1.astype(BF16)
    lo = (r1 - mid.astype(F32)).astype(BF16)
    return hi, mid, lo


def _ssd_scan_kernel(xf_ref, bf_ref, cf_ref, drf_ref, xb_ref, bb_ref, cb_ref, drb_ref,
                     pr_ref, dsk_ref, yf_ref, yb_ref, sf_ref, sb_ref, *, groups):
    @pl.when(pl.program_id(2) == 0)
    def _():
        sf_ref[...] = jnp.zeros(sf_ref.shape, F32)
        sb_ref[...] = jnp.zeros(sb_ref.shape, F32)

    L, R, P, N = SSD_CHUNK, SSD_HEADS_PER_GROUP, SSD_HEAD_DIM, SSD_D_STATE
    W = SSD_GROUP_WIDTH
    row = lax.broadcasted_iota(jnp.int32, (L, L), 0)
    col = lax.broadcasted_iota(jnp.int32, (L, L), 1)
    head_of_lane = lax.broadcasted_iota(jnp.int32, (1, W), 1) // P
    head_of_lane_x = lax.broadcasted_iota(jnp.int32, (L, W), 1) // P
    head_masks = [(head_of_lane_x == r).astype(BF16) for r in range(R)]
    tris = {False: col <= row, True: col >= row}
    tri_b = {rev: t.astype(BF16) for rev, t in tris.items()}
    tri_tb = {False: (row <= col).astype(BF16), True: (row >= col).astype(BF16)}

    items = []
    for g in range(groups):
        pr = pr_ref[g]
        xs, ns = slice(g * W, (g + 1) * W), slice(g * N, (g + 1) * N)
        items.append(dict(g=g, xs=xs, ns=ns, x=xf_ref, b=bf_ref, c=cf_ref, dr=drf_ref, s=sf_ref,
                          y=yf_ref, rev=False, bias_r=pr[:, 0:1], alog_r=pr[:, 1:2]))
        items.append(dict(g=g, xs=xs, ns=ns, x=xb_ref, b=bb_ref, c=cb_ref, dr=drb_ref, s=sb_ref,
                          y=yb_ref, rev=True, bias_r=pr[:, 2:3], alog_r=pr[:, 3:4]))

    for d in items:
        rev = d["rev"]
        d["dtr"] = _softplus(d["dr"][d["g"] * R:(d["g"] + 1) * R, :] + d["bias_r"])
        adt_r = d["dtr"] * (-LOG2_E * jnp.exp(d["alog_r"]))
        pieces = _split3(adt_r)
        d["acs_c"] = sum(lax.dot_general(tri_b[rev], piece, NT_DIMS, preferred_element_type=F32)
                         for piece in pieces)
        acs3 = jnp.dot(jnp.concatenate(pieces, axis=0), tri_tb[rev], preferred_element_type=F32)
        d["acs_r"] = acs3[0:R] + acs3[R:2 * R] + acs3[2 * R:3 * R]
        bm = d["b"][:, d["ns"]]
        cm = d["c"][:, d["ns"]]
        d["cb"] = lax.dot_general(cm, bm, NT_DIMS, preferred_element_type=F32)
        d["bt"] = bm.astype(F32).T

    for d in items:
        acs_c, acs_r = d["acs_c"], d["acs_r"]
        last_r = acs_r[:, 0:1] if d["rev"] else acs_r[:, L - 1:L]
        acs_k = acs_r - jnp.log2(d["dtr"])
        key_scale = jnp.exp2(last_r - acs_k)
        ws, bts, growth = [], [], []
        for r in range(R):
            acs_b = jnp.broadcast_to(acs_c[:, r:r + 1], (L, L))
            seg = acs_b - acs_k[r:r + 1, :]
            lmat = jnp.exp2(jnp.where(tris[d["rev"]], seg, -jnp.inf))
            ws.append((d["cb"] * lmat).astype(BF16))
            bts.append((d["bt"] * key_scale[r:r + 1, :]).astype(BF16))
            growth.append(jnp.exp2(acs_b))
        d["wb"] = jnp.concatenate([jnp.concatenate(ws, axis=1), jnp.concatenate(bts, axis=1)], axis=0)
        lane_lo = lax.broadcasted_iota(jnp.int32, (L, L), 1) < P
        d["growth"] = jnp.concatenate(
            [jnp.where(lane_lo, growth[r], growth[r + 1]) for r in range(0, R, 2)], axis=1)

    for d in items:
        xb16 = d["x"][:, d["xs"]]
        xstack = jnp.concatenate([xb16 * mask for mask in head_masks], axis=0)
        both = jnp.dot(d["wb"], xstack, preferred_element_type=F32)
        d["yd"], d["upd"] = both[0:L], both[L:L + N]
        d["state"] = d["s"][d["g"]]
        d["y_off"] = jnp.dot(d["c"][:, d["ns"]], d["state"].astype(BF16), preferred_element_type=F32)

    for d in items:
        acs_c = d["acs_c"]
        last_c = acs_c[0:1, :] if d["rev"] else acs_c[L - 1:L, :]
        y = d["yd"] + d["y_off"] * d["growth"]
        d["s"][d["g"]] = d["state"] * _expand_heads(jnp.exp2(last_c), head_of_lane) + d["upd"]
        if not d["rev"]:
            y = y + dsk_ref[d["g"]] * d["x"][:, d["xs"]].astype(F32)
        d["y"][:, d["xs"]] = y.astype(d["y"].dtype)


def _ssd_scan(xbc, dt_t, dt_bias, a_log, d_skip, *, groups=8):
    bsz, s, _ = xbc.shape
    G, R, P, N, L = SSD_GROUPS, SSD_HEADS_PER_GROUP, SSD_HEAD_DIM, SSD_D_STATE, SSD_CHUNK
    assert G % groups == 0 and L == 2 * P
    d_inner = G * R * P
    nc = s // L
    xw = groups * R * P
    nw = groups * N
    b_off = d_inner // nw
    c_off = b_off + G // groups
    gblocks = G // groups
    pr = jnp.stack([dt_bias[0].reshape(G, R), a_log[0].reshape(G, R),
                    dt_bias[1].reshape(G, R), a_log[1].reshape(G, R)], axis=2).astype(F32)
    dsk = jnp.repeat(d_skip.astype(F32).reshape(G, 1, R), P, axis=2)

    def fwd(c):
        return c

    def bwd(c):
        return nc - 1 - c

    def dir_specs(d, cm):
        return [
            pl.BlockSpec((None, L, xw), lambda b, g, c: (b, cm(c), g)),
            pl.BlockSpec((None, L, nw), lambda b, g, c: (b, cm(c), b_off + g)),
            pl.BlockSpec((None, L, nw), lambda b, g, c: (b, cm(c), c_off + g)),
            pl.BlockSpec((groups * R, L), lambda b, g, c: (d * gblocks + g, b * nc + cm(c))),
        ]

    in_specs = dir_specs(0, fwd) + dir_specs(1, bwd) + [
        pl.BlockSpec((groups, R, 4), lambda b, g, c: (g, 0, 0)),
        pl.BlockSpec((groups, 1, R * P), lambda b, g, c: (g, 0, 0)),
    ]
    out_sd = jax.ShapeDtypeStruct((bsz, s, d_inner), BF16)
    return pl.pallas_call(
        functools.partial(_ssd_scan_kernel, groups=groups),
        out_shape=(out_sd, out_sd),
        grid=(bsz, G // groups, nc),
        in_specs=in_specs,
        out_specs=(pl.BlockSpec((None, L, xw), lambda b, g, c: (b, c, g)),
                   pl.BlockSpec((None, L, xw), lambda b, g, c: (b, nc - 1 - c, g))),
        scratch_shapes=[pltpu.VMEM((groups, N, R * P), F32), pltpu.VMEM((groups, N, R * P), F32)],
        compiler_params=_params("parallel", "parallel", "arbitrary"),
        name="ssd_scan",
    )(xbc, xbc, xbc, dt_t, xbc, xbc, xbc, dt_t, pr, dsk)


def _ssd_out_kernel(yf_ref, yb_ref, z_ref, g_ref, w_ref, x_ref, o_ref):
    y = (yf_ref[...].astype(F32) + yb_ref[...].astype(F32)) * _silu(z_ref[...].astype(F32))
    yn = _rms(y, g_ref[...]).astype(BF16)
    o_ref[...] = x_ref[...] + jnp.dot(yn, w_ref[...], preferred_element_type=F32)


def _ssd_out(yf, yb, zx, norm_g, out_w, layer, x, *, tm=512):
    t, d_inner = yf.shape
    d = out_w.shape[-1]
    assert t % tm == 0 and out_w.dtype == BF16
    return pl.pallas_call(
        _ssd_out_kernel,
        out_shape=jax.ShapeDtypeStruct((t, d), F32),
        grid=(t // tm,),
        in_specs=[pl.BlockSpec((tm, d_inner), lambda i: (i, 0)),
                  pl.BlockSpec((tm, d_inner), lambda i: (i, 0)),
                  pl.BlockSpec((tm, d_inner), lambda i: (i, 0)),
                  pl.BlockSpec((1, d_inner), lambda i: (0, 0)),
                  _layer_spec(out_w, layer, (d_inner, d), lambda i: (0, 0)),
                  pl.BlockSpec((tm, d), lambda i: (i, 0))],
        out_specs=pl.BlockSpec((tm, d), lambda i: (i, 0)),
        compiler_params=_params("parallel"),
        name="ssd_out",
    )(yf, yb, zx, norm_g.reshape(1, d_inner), out_w, x)


def _ssd_layer(x, norm_g, in_w, layer, conv_w, conv_b, dt_bias, a_log, d_skip, ssd_norm_g, out_w,
               bsz, s):
    d_inner = ssd_norm_g.shape[0]
    conv_ch = conv_w.shape[1]
    zx_cols = d_inner + conv_ch
    zx, dt_t = _mm(x, in_w, layer=layer, g=norm_g, n_cols=zx_cols, side_wt=in_w[layer, :, zx_cols:].T,
                   out_dtype=BF16, tn=zx_cols // 2, name="ssd_in")
    xbc = _ssd_conv(zx.reshape(bsz, s, zx_cols), conv_w, conv_b, d_inner=d_inner)
    yf, yb = _ssd_scan(xbc, dt_t, dt_bias, a_log, d_skip)
    t = bsz * s
    return _ssd_out(yf.reshape(t, d_inner), yb.reshape(t, d_inner), zx, ssd_norm_g, out_w, layer, x)


def _dil_attn_kernel(q_ref, kp_ref, kc_ref, kn_ref, vp_ref, vc_ref, vn_ref, bias_ref,
                     o_ref, lse_ref, *, n_steps, blk, per_step, split_rows):
    j = pl.program_id(1)
    hd = DIL_HEAD_DIM
    first = (j == 0).astype(jnp.int32)
    last = (j == n_steps - 1).astype(jnp.int32)
    piece = lambda n: slice(n * blk, (n + 1) * blk)
    subs = []
    for n in range(per_step):
        if split_rows:
            kv = tuple(divmod(per_step + n + t - 1, per_step) for t in range(3))
            kv = tuple((ri, piece(pi)) for ri, pi in kv)
            variant = (first if n == 0 else 0) + (2 * last if n == per_step - 1 else 0)
        else:
            kv = tuple((ri, piece(n)) for ri in range(3))
            variant = first + 2 * last
        subs.append((piece(n), kv, variant))
    k_refs, v_refs = (kp_ref, kc_ref, kn_ref), (vp_ref, vc_ref, vn_ref)
    lane = lax.broadcasted_iota(jnp.int32, (blk, 2 * hd), 1)
    scale = 1.0 / math.sqrt(hd)
    pairs = DIL_HEADS // 2
    tiles = [(si, a, slice(a * 2 * hd, (a + 1) * 2 * hd))
             for si in range(per_step) for a in range(pairs)]
    scores = []
    for si, a, sl in tiles:
        q_rows, kv, variant = subs[si]
        q = (q_ref[q_rows, sl].astype(F32) * scale).astype(BF16)
        zero = jnp.zeros_like(q)
        qbd = jnp.concatenate([jnp.where(lane < hd, q, zero), jnp.where(lane >= hd, q, zero)], axis=0)
        k = jnp.concatenate([k_refs[ri][rows, sl] for ri, rows in kv], axis=0)
        scores.append(lax.dot_general(k, qbd, NT_DIMS, preferred_element_type=F32)
                      + bias_ref[variant, a])
    probs, lse_rows = [], [[] for _ in range(per_step)]
    for (si, a, sl), s in zip(tiles, scores):
        m = jnp.max(s, axis=0, keepdims=True)
        p = jnp.exp(s - m)
        den = jnp.sum(p, axis=0, keepdims=True)
        probs.append((p * (1.0 / den)).astype(BF16))
        lse_rows[si].append(m + jnp.log(den))
    for si, rows_ in enumerate(lse_rows):
        t = jnp.concatenate(rows_, axis=0).T
        lse_ref[subs[si][0], :] = jnp.concatenate([t[0:blk], t[blk:2 * blk]], axis=1)
    for (si, a, sl), pn in zip(tiles, probs):
        q_rows, kv, _ = subs[si]
        v = jnp.concatenate([v_refs[ri][rows, sl] for ri, rows in kv], axis=0)
        r = lax.dot_general(pn, v, TN_DIMS, preferred_element_type=F32)
        o_ref[q_rows, sl] = jnp.where(lane < hd, r[0:blk], r[blk:2 * blk]).astype(o_ref.dtype)


def _dil_bias(table, window, dilation):
    blk = window // (2 * dilation)
    delta = np.arange(3 * blk)[:, None] - blk - np.arange(blk)[None, :]
    bucket = _rel_bucket_np(delta * dilation)
    band = np.abs(delta) <= blk
    key_blk = np.arange(3 * blk)[:, None] // blk
    idx = []
    for variant in range(4):
        ok = band & ~((key_blk == 0) & bool(variant & 1)) & ~((key_blk == 2) & bool(variant & 2))
        idx.append(np.where(ok, bucket, MASKED_BUCKET))
    tiles = _bias_tiles(table, np.stack(idx).astype(np.int32))
    tiles = tiles.reshape(DIL_HEADS // 2, 2, 4, 3 * blk, blk)
    return jnp.transpose(tiles, (2, 0, 3, 1, 4)).reshape(4, DIL_HEADS // 2, 3 * blk, 2 * blk)


def _dil_group(qkv, bias, window, dilation, bsz, s):
    width = DIL_HEADS * DIL_HEAD_DIM
    d = dilation
    n = s // d
    blk = window // (2 * d)
    assert n % blk == 0 and blk % 8 == 0
    nb = n // blk
    pairs = DIL_HEADS // 2
    split_rows = d == 1
    per_step = min(DIL_TILES_PER_STEP, nb if split_rows else d)
    n_steps, n_res = (nb // per_step, 1) if split_rows else (nb, d // per_step)
    assert (nb if split_rows else d) % per_step == 0

    def spec(which, shift):
        def index(b, jj, r):
            jn = jnp.clip(jj + shift, 0, n_steps - 1)
            return ((b * n_steps + jn) * n_res + r, which)
        return pl.BlockSpec((per_step * blk, width), index)

    o, lse = pl.pallas_call(
        functools.partial(_dil_attn_kernel, n_steps=n_steps, blk=blk, per_step=per_step,
                          split_rows=split_rows),
        out_shape=(jax.ShapeDtypeStruct((bsz * s, width), BF16),
                   jax.ShapeDtypeStruct((bsz * s, DIL_HEADS), F32)),
        grid=(bsz, n_steps, n_res),
        in_specs=[spec(0, 0), spec(1, -1), spec(1, 0), spec(1, 1),
                  spec(2, -1), spec(2, 0), spec(2, 1),
                  pl.BlockSpec((4, pairs, 3 * blk, 2 * blk), lambda b, jj, r: (0, 0, 0, 0))],
        out_specs=(pl.BlockSpec((per_step * blk, width),
                                lambda b, jj, r: ((b * n_steps + jj) * n_res + r, 0)),
                   pl.BlockSpec((per_step * blk, DIL_HEADS),
                                lambda b, jj, r: ((b * n_steps + jj) * n_res + r, 0))),
        compiler_params=_params("parallel", "parallel", "parallel"),
        name=f"dil_attn_d{d}",
    )(qkv, qkv, qkv, qkv, qkv, qkv, qkv, bias)
    return o, lse


def _dil_out_kernel(o0_ref, o1_ref, o2_ref, l0_ref, l1_ref, l2_ref, w_ref, x_ref, o_ref, *, perms):
    width = o0_ref.shape[1]
    lses = []
    for l_ref, (pb, d) in zip((l0_ref, l1_ref, l2_ref), perms):
        lse = l_ref[...]
        if d > 1:
            lse = sum(_permute_rows(piece, pb, d, inverse=True, out_dtype=F32)
                      for piece in _split3(lse))
        lses.append(lse)
    m = jnp.maximum(jnp.maximum(lses[0], lses[1]), lses[2])
    es = [jnp.exp(l - m) for l in lses]
    inv = 1.0 / (es[0] + es[1] + es[2])
    col = lax.broadcasted_iota(jnp.int32, (DIL_HEADS, width), 0)
    head_of_col = 2 * (col % (DIL_HEADS // 2)) + col // (DIL_HEADS // 2)
    head_of_lane = lax.broadcasted_iota(jnp.int32, (DIL_HEADS, width), 1) // DIL_HEAD_DIM
    expand = (head_of_col == head_of_lane).astype(BF16)
    y = None
    for e, o_ref_g, perm in zip(es, (o0_ref, o1_ref, o2_ref), perms):
        w_g = e * inv
        w_hi = w_g.astype(BF16)
        w_lo = (w_g - w_hi.astype(F32)).astype(BF16)
        wts = (jnp.dot(w_hi, expand, preferred_element_type=F32)
               + jnp.dot(w_lo, expand, preferred_element_type=F32))
        o_nat = _permute_rows(o_ref_g[...], *perm, inverse=True)
        term = wts * o_nat.astype(F32)
        y = term if y is None else y + term
    o_ref[...] = x_ref[...] + jnp.dot(y.astype(BF16), w_ref[...], preferred_element_type=F32)


def _dil_out(os_, lses, perms, out_w, layer, x, *, tm=1024):
    t, width = os_[0].shape
    d = out_w.shape[-1]
    assert t % tm == 0 and all(tm % pb == 0 for pb, _ in perms) and out_w.dtype == BF16
    row = lambda i: (i, 0)
    return pl.pallas_call(
        functools.partial(_dil_out_kernel, perms=perms),
        out_shape=jax.ShapeDtypeStruct((t, d), F32),
        grid=(t // tm,),
        in_specs=[pl.BlockSpec((tm, width), row)] * 3 + [pl.BlockSpec((tm, DIL_HEADS), row)] * 3
                 + [_layer_spec(out_w, layer, (width, d), lambda i: (0, 0)), pl.BlockSpec((tm, d), row)],
        out_specs=pl.BlockSpec((tm, d), row),
        compiler_params=_params("parallel"),
        name="dil_out",
    )(*os_, *lses, out_w, x)


def _dil_layer(x, norm_g, qkv_w, out_w, layer, table, bsz, s):
    n_groups = len(DIL_CONFIGS)
    width = DIL_HEADS * DIL_HEAD_DIM
    tn = width
    per_w = width // tn
    os_, lses, perms = [], [], []
    for gi, (win, dil) in enumerate(DIL_CONFIGS):
        perm = ((win // (2 * dil)) * dil, dil)
        qkv = _mm(x, qkv_w, layer=layer, g=norm_g, n_cols=3 * width, perm=perm, out_dtype=BF16, tn=tn,
                  w_col_block=lambda j, gi=gi: ((j // per_w) * n_groups + gi) * per_w + j % per_w,
                  name=f"dil_qkv_d{dil}")
        o, lse = _dil_group(qkv, _dil_bias(table, win, dil), win, dil, bsz, s)
        os_.append(o)
        lses.append(lse)
        perms.append(perm)
    return _dil_out(os_, lses, tuple(perms), out_w, layer, x)


def _diff_attn_kernel(q_ref, k_ref, v_ref, bank_ref, lam_ref, g_ref, o_ref,
                      vx_ref, q2_ref, s_ref, m_ref, acc_ref, *, tq, tk, seq, lam_init):
    qi = pl.program_id(2)
    tile = DIFF_TILE
    hd = DIFF_HEAD_DIM
    nv = DIFF_V_DIM

    @pl.when(qi == 0)
    def _():
        lane = lax.broadcasted_iota(jnp.int32, (seq, nv), 1)
        vx_ref[:, 0:nv] = v_ref[...]
        vx_ref[:, nv:2 * nv] = jnp.where(lane == 0, 1.0, 0.0).astype(BF16)

    q = (q_ref[...].astype(F32) * (LOG2_E / math.sqrt(hd))).astype(BF16)
    lane = lax.broadcasted_iota(jnp.int32, q.shape, 1)
    zero = jnp.zeros_like(q)
    q2_ref[0:tq, :] = jnp.where(lane < hd, q, zero)
    q2_ref[tq:2 * tq, :] = jnp.where(lane >= hd, q, zero)
    m_ref[...] = jnp.full(m_ref.shape, -jnp.inf, F32)
    acc_ref[...] = jnp.zeros(acc_ref.shape, F32)

    n_chunks = seq // tk
    kt = tk // tile

    def scores(kc, slot):
        start = pl.multiple_of(kc * tk, tk)
        s_ref[slot] = lax.dot_general(q2_ref[...], k_ref[pl.ds(start, tk), :], NT_DIMS,
                                      preferred_element_type=F32)

    def consume(kc, slot):
        start = pl.multiple_of(kc * tk, tk)
        vx = vx_ref[pl.ds(start, tk), :]
        for i in range(2):
            rows = []
            for a in range(tq // tile):
                cols = []
                for b in range(kt):
                    dlt = (kc * kt + b) - (qi * (tq // tile) + a)
                    idx = jnp.clip(dlt, -DIFF_BANK_REACH, DIFF_BANK_REACH) + DIFF_BANK_REACH
                    r0 = i * tq + a * tile
                    cols.append(s_ref[slot, r0:r0 + tile, b * tile:(b + 1) * tile]
                                + bank_ref[i, idx])
                rows.append(cols)
            part = jnp.concatenate([functools.reduce(jnp.maximum, cols) for cols in rows], axis=0)
            row_max = jnp.broadcast_to(jnp.max(part, axis=-1, keepdims=True), part.shape)
            m_old = m_ref[i]
            m_new = jnp.maximum(m_old, row_max)
            alpha = jnp.exp2(m_old - m_new)
            p = jnp.concatenate(
                [jnp.concatenate([jnp.exp2(c - m_new[a * tile:(a + 1) * tile]).astype(BF16)
                                  for c in cols], axis=1) for a, cols in enumerate(rows)], axis=0)
            pv = jnp.dot(p, vx, preferred_element_type=F32)
            acc_ref[i] = jnp.concatenate([alpha, alpha], axis=1) * acc_ref[i] + pv
            m_ref[i] = m_new

    scores(0, 0)

    def body(j, carry):
        for slot in range(2):
            kc = 2 * j + slot
            scores(kc + 1, 1 - slot)
            consume(kc, slot)
        return carry

    lax.fori_loop(0, n_chunks // 2 - 1, body, 0)
    scores(n_chunks - 1, 1)
    consume(n_chunks - 2, 0)
    consume(n_chunks - 1, 1)

    lf = lam_ref[...]
    lam_full = (jnp.exp(jnp.sum(lf[0:1] * lf[1:2], axis=-1, keepdims=True))
                - jnp.exp(jnp.sum(lf[2:3] * lf[3:4], axis=-1, keepdims=True)) + lam_init)
    o = (acc_ref[0, :, 0:nv] / acc_ref[0, :, nv:nv + 1]
         - lam_full * (acc_ref[1, :, 0:nv] / acc_ref[1, :, nv:nv + 1]))
    o = _rms(o, g_ref[...]) * (1.0 - lam_init)
    o_ref[...] = o.astype(o_ref.dtype)


def _diff_attn(qkv, bank, lam, subln_g, lam_init, bsz, s, *, tq=1024, tk=1024):
    hw = 2 * DIFF_HEAD_DIM
    H = DIFF_HEADS
    assert s % tq == 0 and s % (2 * tk) == 0 and tq % DIFF_TILE == 0 and tk % DIFF_TILE == 0
    n_bank = bank.shape[1]
    return pl.pallas_call(
        functools.partial(_diff_attn_kernel, tq=tq, tk=tk, seq=s, lam_init=lam_init),
        out_shape=jax.ShapeDtypeStruct((bsz, s, H * DIFF_V_DIM), BF16),
        grid=(bsz, H, s // tq),
        in_specs=[pl.BlockSpec((None, tq, hw), lambda b, h, i: (b, i, h)),
                  pl.BlockSpec((None, s, hw), lambda b, h, i: (b, 0, H + h)),
                  pl.BlockSpec((None, s, DIFF_V_DIM), lambda b, h, i: (b, 0, 2 * H + h)),
                  pl.BlockSpec((2, n_bank, DIFF_TILE, DIFF_TILE), lambda b, h, i: (h, 0, 0, 0)),
                  pl.BlockSpec(lam.shape, lambda b, h, i: (0, 0)),
                  pl.BlockSpec((1, DIFF_V_DIM), lambda b, h, i: (0, 0))],
        out_specs=pl.BlockSpec((None, tq, DIFF_V_DIM), lambda b, h, i: (b, i, h)),
        scratch_shapes=[pltpu.VMEM((s, 2 * DIFF_V_DIM), BF16),
                        pltpu.VMEM((2 * tq, hw), BF16),
                        pltpu.VMEM((2, 2 * tq, tk), F32),
                        pltpu.VMEM((2, tq, DIFF_TILE), F32),
                        pltpu.VMEM((2, tq, 2 * DIFF_V_DIM), F32)],
        compiler_params=_params("parallel", "parallel", "arbitrary"),
        name="diff_attn",
    )(qkv, qkv, qkv, bank, lam.astype(F32), subln_g.reshape(1, DIFF_V_DIM).astype(F32))


def _diff_bank_buckets():
    reach = DIFF_BANK_REACH
    local = np.arange(DIFF_TILE)[None, :] - np.arange(DIFF_TILE)[:, None]
    tiles = np.stack([_rel_bucket_np(local + (t - reach) * DIFF_TILE) for t in range(2 * reach + 1)])
    assert (tiles[0] == tiles[0, 0, 0]).all() and (tiles[-1] == tiles[-1, 0, 0]).all()
    return tiles


def _diff_layer(x, norm_g, qkv_w, lam, subln_g, out_w, layer, table, layer_idx, bsz, s):
    d = x.shape[1]
    lam_init = 0.8 - 0.6 * math.exp(-0.3 * layer_idx)
    qkv = _mm(x, qkv_w, layer=layer, g=norm_g, out_dtype=BF16, tn=qkv_w.shape[-1], name="diff_qkv")
    bank = _bias_tiles(table, _diff_bank_buckets(), scale=LOG2_E)
    o = _diff_attn(qkv.reshape(bsz, s, 3 * d), bank, lam, subln_g, lam_init, bsz, s)
    return _mm(o.reshape(bsz * s, d), out_w, layer=layer, res=x, name="diff_out")


def kernel(x, rel_bias, norm_mix_g, norm_mlp_g, mlp_w1, mlp_w2, ssd_in_w, ssd_conv_w, ssd_conv_b, ssd_dt_bias, ssd_a_log, ssd_d, ssd_norm_g, ssd_out_w, dil_qkv_w, dil_out_w, diff_qkv_w, diff_lambda, diff_subln_g, diff_out_w, final_norm_g):
    bsz, s, d = x.shape
    depth = norm_mix_g.shape[0]
    (mlp_w1, mlp_w2, ssd_in_w, ssd_out_w, dil_qkv_w, dil_out_w, diff_qkv_w, diff_out_w) = (
        w.astype(BF16) for w in (mlp_w1, mlp_w2, ssd_in_w, ssd_out_w, dil_qkv_w, dil_out_w,
                                 diff_qkv_w, diff_out_w))
    h = x.reshape(bsz * s, d)
    for i in range(depth):
        kind, j = i % 3, i // 3
        if kind == 0:
            h = _ssd_layer(h, norm_mix_g[i], ssd_in_w, j, ssd_conv_w[j], ssd_conv_b[j],
                           ssd_dt_bias[j], ssd_a_log[j], ssd_d[j], ssd_norm_g[j], ssd_out_w, bsz, s)
        elif kind == 1:
            h = _dil_layer(h, norm_mix_g[i], dil_qkv_w, dil_out_w, j, rel_bias, bsz, s)
        else:
            h = _diff_layer(h, norm_mix_g[i], diff_qkv_w, diff_lambda[j], diff_subln_g[j],
                            diff_out_w, j, rel_bias, i, bsz, s)
        h = _mlp(h, norm_mlp_g[i], mlp_w1, mlp_w2, i,
                 final_norm_g if i == depth - 1 else None)
    return h.reshape(bsz, s, d)
```

```python
import functools
import math

import numpy as np
import jax
import jax.numpy as jnp
from jax import lax
from jax.experimental import pallas as pl
from jax.experimental.pallas import tpu as pltpu

F32 = jnp.float32
BF16 = jnp.bfloat16

RMS_EPS = 1e-6
NEG_INF = -1e30
LOG2_E = math.log2(math.e)

LANE_TILE = 128
V7X_VMEM_BYTES = 64 * 1024 * 1024
VMEM_LIMIT_BYTES = V7X_VMEM_BYTES - 8 * 1024 * 1024

SSD_HEAD_DIM = 64
SSD_GROUPS = 8
SSD_HEADS_PER_GROUP = 4
SSD_D_STATE = 128
SSD_CONV_WIDTH = 5
SSD_CHUNK = 128
SSD_GROUP_WIDTH = SSD_HEADS_PER_GROUP * SSD_HEAD_DIM

DIL_CONFIGS = ((128, 1), (512, 4), (2048, 16))
DIL_HEADS = 16
DIL_HEAD_DIM = 64
DIL_TILES_PER_STEP = 8

DIFF_HEADS = 8
DIFF_HEAD_DIM = 64
DIFF_V_DIM = 128
DIFF_TILE = 128
DIFF_BANK_REACH = 6

REL_BUCKETS = 32
REL_MAX_DIST = 1024
REL_BIAS_HEADS = 16
MASKED_BUCKET = REL_BUCKETS

NT_DIMS = (((1,), (1,)), ((), ()))
TN_DIMS = (((0,), (0,)), ((), ()))


def _params(*semantics):
    return pltpu.CompilerParams(dimension_semantics=semantics,
                                vmem_limit_bytes=VMEM_LIMIT_BYTES)


def _rms(x, g):
    ms = jnp.mean(x * x, axis=-1, keepdims=True)
    return x * lax.rsqrt(ms + RMS_EPS) * g


def _silu(x):
    return x * (1.0 / (1.0 + jnp.exp(-x)))


def _softplus(x):
    return jnp.maximum(x, 0.0) + jnp.log1p(jnp.exp(-jnp.abs(x)))


def _residue_major_perm(pb, d, inverse=False):
    i = lax.broadcasted_iota(jnp.int32, (pb, pb), 0)
    j = lax.broadcasted_iota(jnp.int32, (pb, pb), 1)
    per = pb // d
    if inverse:
        i, j = j, i
    return (j == (i % per) * d + i // per).astype(BF16)


def _permute_rows(x, pb, d, inverse=False, out_dtype=BF16):
    if d == 1:
        return x.astype(out_dtype)
    pm = _residue_major_perm(pb, d, inverse)
    blocks = [jnp.dot(pm, x[r0:r0 + pb], preferred_element_type=F32).astype(out_dtype)
              for r0 in range(0, x.shape[0], pb)]
    return blocks[0] if len(blocks) == 1 else jnp.concatenate(blocks, axis=0)


def _mm_kernel(*refs, norm, res, side):
    refs = list(refs)
    x_ref, w_ref = refs[0], refs[1]
    pos = 2
    g_ref = r_ref = ws_ref = os_ref = None
    if norm:
        g_ref = refs[pos]
        pos += 1
    if res:
        r_ref = refs[pos]
        pos += 1
    if side:
        ws_ref = refs[pos]
        pos += 1
    o_ref = refs[pos]
    pos += 1
    if side:
        os_ref = refs[pos]
        pos += 1
    xs_ref = refs[pos]

    @pl.when(pl.program_id(1) == 0)
    def _():
        x = x_ref[...].astype(F32)
        if norm:
            x = _rms(x, g_ref[...])
        x = x.astype(BF16)
        xs_ref[...] = x
        if side:
            os_ref[...] = lax.dot_general(ws_ref[...], x, NT_DIMS, preferred_element_type=F32)

    acc = jnp.dot(xs_ref[...], w_ref[...], preferred_element_type=F32)
    if res:
        acc = r_ref[...] + acc
    o_ref[...] = acc.astype(o_ref.dtype)


def _layer_spec(w, layer, block, index):
    if w.ndim == 2:
        return pl.BlockSpec(block, index)
    return pl.BlockSpec((None,) + block, lambda *g: (layer,) + index(*g))


def _mm(x, w, *, layer=0, g=None, res=None, n_cols=None, side_wt=None, out_dtype=F32,
        tm=1024, tn=1024, name="mm"):
    t, k = x.shape
    n = w.shape[-1] if n_cols is None else n_cols
    tn = min(tn, n)
    tm = min(tm, t)
    assert t % tm == 0 and n % tn == 0 and w.shape[-2] == k and w.dtype == BF16
    in_specs = [pl.BlockSpec((tm, k), lambda i, j: (i, 0)),
                _layer_spec(w, layer, (k, tn), lambda i, j: (0, j))]
    args = [x, w]
    if g is not None:
        in_specs.append(pl.BlockSpec((1, k), lambda i, j: (0, 0)))
        args.append(g.reshape(1, k))
    if res is not None:
        in_specs.append(pl.BlockSpec((tm, tn), lambda i, j: (i, j)))
        args.append(res)
    out_shape = jax.ShapeDtypeStruct((t, n), out_dtype)
    out_specs = pl.BlockSpec((tm, tn), lambda i, j: (i, j))
    if side_wt is not None:
        m = side_wt.shape[0]
        in_specs.append(pl.BlockSpec((m, k), lambda i, j: (0, 0)))
        args.append(side_wt)
        out_shape = (out_shape, jax.ShapeDtypeStruct((m, t), F32))
        out_specs = (out_specs, pl.BlockSpec((m, tm), lambda i, j: (0, i)))
    return pl.pallas_call(
        functools.partial(_mm_kernel, norm=g is not None, res=res is not None,
                          side=side_wt is not None),
        out_shape=out_shape,
        grid=(t // tm, n // tn),
        in_specs=in_specs,
        out_specs=out_specs,
        scratch_shapes=[pltpu.VMEM((tm, k), BF16)],
        compiler_params=_params("parallel", "arbitrary"),
        name=name,
    )(*args)


def _mlp_kernel(*refs, final):
    if final:
        x_ref, g_ref, w1_ref, w2_ref, fg_ref, o_ref, xn_ref, acc_ref = refs
    else:
        x_ref, g_ref, w1_ref, w2_ref, o_ref, xn_ref, acc_ref = refs
        fg_ref = None
    f = pl.program_id(1)

    @pl.when(f == 0)
    def _():
        x = x_ref[...]
        xn_ref[...] = _rms(x, g_ref[...]).astype(BF16)
        acc_ref[...] = x

    h = jnp.dot(xn_ref[...], w1_ref[...], preferred_element_type=F32)
    h = jnp.square(jnp.maximum(h, 0.0)).astype(BF16)
    acc_ref[...] += jnp.dot(h, w2_ref[...], preferred_element_type=F32)

    @pl.when(f == pl.num_programs(1) - 1)
    def _():
        y = acc_ref[...]
        if final:
            y = _rms(y, fg_ref[...])
        o_ref[...] = y


def _mlp(x, g, w1, w2, layer, final_g=None, *, tm=1024, tf=1024):
    t, d = x.shape
    ff = w1.shape[-1]
    assert t % tm == 0 and ff % tf == 0 and w1.dtype == BF16 and w2.dtype == BF16
    final = final_g is not None
    in_specs = [pl.BlockSpec((tm, d), lambda i, f: (i, 0)),
                pl.BlockSpec((1, d), lambda i, f: (0, 0)),
                _layer_spec(w1, layer, (d, tf), lambda i, f: (0, f)),
                _layer_spec(w2, layer, (tf, d), lambda i, f: (f, 0))]
    args = [x, g.reshape(1, d), w1, w2]
    if final:
        in_specs.append(pl.BlockSpec((1, d), lambda i, f: (0, 0)))
        args.append(final_g.reshape(1, d))
    return pl.pallas_call(
        functools.partial(_mlp_kernel, final=final),
        out_shape=jax.ShapeDtypeStruct((t, d), F32),
        grid=(t // tm, ff // tf),
        in_specs=in_specs,
        out_specs=pl.BlockSpec((tm, d), lambda i, f: (i, 0)),
        scratch_shapes=[pltpu.VMEM((tm, d), BF16), pltpu.VMEM((tm, d), F32)],
        compiler_params=_params("parallel", "arbitrary"),
        name="mlp",
    )(*args)


def _rel_bucket_np(rel):
    half = REL_BUCKETS // 2
    max_exact = half // 2
    rel = np.asarray(rel, np.int32)
    ret = np.where(rel > 0, half, 0)
    n = np.abs(rel)
    nf = np.maximum(n, 1).astype(np.float32)
    scale = np.float32(math.log(REL_MAX_DIST / max_exact))
    large = max_exact + (np.log(nf / np.float32(max_exact)) / scale
                         * np.float32(half - max_exact)).astype(np.int32)
    large = np.minimum(large, half - 1)
    return (ret + np.where(n < max_exact, n, large)).astype(np.int32)


def _bias_tiles_kernel(tbl_ref, idx_ref, o_ref, *, scale):
    h = pl.program_id(0)
    idx = idx_ref[...]
    acc = jnp.zeros(idx.shape, F32)
    for b in range(REL_BUCKETS):
        acc = jnp.where(idx == b, tbl_ref[b, h] * scale, acc)
    o_ref[...] = jnp.where(idx == MASKED_BUCKET, NEG_INF, acc)


def _bias_tiles(table, bucket_idx, scale=1.0):
    n, r, c = bucket_idx.shape
    return pl.pallas_call(
        functools.partial(_bias_tiles_kernel, scale=scale),
        out_shape=jax.ShapeDtypeStruct((REL_BIAS_HEADS, n, r, c), F32),
        grid=(REL_BIAS_HEADS,),
        in_specs=[pl.BlockSpec(memory_space=pltpu.SMEM),
                  pl.BlockSpec((n, r, c), lambda h: (0, 0, 0))],
        out_specs=pl.BlockSpec((None, n, r, c), lambda h: (h, 0, 0, 0)),
        compiler_params=_params("parallel"),
        name="bias_tiles",
    )(table.astype(F32), jnp.asarray(bucket_idx))


def _conv_kernel(x_ref, w_ref, b_ref, o_ref, xp_ref, *, rows):
    s, tc = x_ref.shape
    pad = 16
    half = SSD_CONV_WIDTH // 2
    xp_ref[0:pad, :] = jnp.zeros((pad, tc), BF16)
    xp_ref[s + pad:s + 2 * pad, :] = jnp.zeros((pad, tc), BF16)
    xp_ref[pad:s + pad, :] = x_ref[...]
    win = rows + 2 * pad
    side_taps = [k for k in range(SSD_CONV_WIDTH) if k != half]
    out_row = lax.broadcasted_iota(jnp.int32, (rows, win), 0)
    win_row = lax.broadcasted_iota(jnp.int32, (rows, win), 1)
    sel = jnp.concatenate([(win_row == out_row + (pad + k - half)).astype(BF16) for k in side_taps],
                          axis=0)
    w = w_ref[...]
    b = b_ref[...]
    for ch in range(s // rows):
        r0 = ch * rows
        window = xp_ref[r0:r0 + win, :]
        shifted = jnp.dot(sel, window, preferred_element_type=F32)
        acc = b + w[half:half + 1, :] * window[pad:pad + rows].astype(F32)
        for t, k in enumerate(side_taps):
            acc = acc + w[k:k + 1, :] * shifted[t * rows:(t + 1) * rows]
        o_ref[r0:r0 + rows, :] = _silu(acc).astype(o_ref.dtype)


def _ssd_conv(zx, conv_w, conv_b, *, d_inner, tc=256, rows=128):
    bsz, s, _ = zx.shape
    c = conv_w.shape[1]
    off = d_inner // tc
    assert zx.dtype == BF16 and s % rows == 0 and rows % 16 == 0
    return pl.pallas_call(
        functools.partial(_conv_kernel, rows=rows),
        out_shape=jax.ShapeDtypeStruct((bsz, s, c), BF16),
        grid=(bsz, c // tc),
        in_specs=[pl.BlockSpec((None, s, tc), lambda b, j: (b, 0, off + j)),
                  pl.BlockSpec((SSD_CONV_WIDTH, tc), lambda b, j: (0, j)),
                  pl.BlockSpec((1, tc), lambda b, j: (0, j))],
        out_specs=pl.BlockSpec((None, s, tc), lambda b, j: (b, 0, j)),
        scratch_shapes=[pltpu.VMEM((s + 32, tc), BF16)],
        compiler_params=_params("parallel", "parallel"),
        name="ssd_conv",
    )(zx, conv_w, conv_b.reshape(1, c))


def _expand_heads(v, head_of_lane):
    r_heads = v.shape[1]
    out = v[:, r_heads - 1:r_heads]
    for r in range(r_heads - 2, -1, -1):
        out = jnp.where(head_of_lane == r, v[:, r:r + 1], out)
    return out


def _split3(x):
    hi = x.astype(BF16)
    r1 = x - hi.astype(F32)
    mid = r1.astype(BF16)
    lo = (r1 - mid.astype(F32)).astype(BF16)
    return hi, mid, lo


def _ssd_scan_kernel(xf_ref, bf_ref, cf_ref, drf_ref, xb_ref, bb_ref, cb_ref, drb_ref,
                     pr_ref, dsk_ref, yf_ref, yb_ref, sf_ref, sb_ref, *, groups):
    @pl.when(pl.program_id(2) == 0)
    def _():
        sf_ref[...] = jnp.zeros(sf_ref.shape, F32)
        sb_ref[...] = jnp.zeros(sb_ref.shape, F32)

    L, R, P, N = SSD_CHUNK, SSD_HEADS_PER_GROUP, SSD_HEAD_DIM, SSD_D_STATE
    W = SSD_GROUP_WIDTH
    row = lax.broadcasted_iota(jnp.int32, (L, L), 0)
    col = lax.broadcasted_iota(jnp.int32, (L, L), 1)
    head_of_lane = lax.broadcasted_iota(jnp.int32, (1, W), 1) // P
    head_of_lane_x = lax.broadcasted_iota(jnp.int32, (L, W), 1) // P
    head_masks = [(head_of_lane_x == r).astype(BF16) for r in range(R)]
    tris = {False: col <= row, True: col >= row}
    tri_b = {rev: t.astype(BF16) for rev, t in tris.items()}
    tri_tb = {False: (row <= col).astype(BF16), True: (row >= col).astype(BF16)}

    items = []
    for g in range(groups):
        pr = pr_ref[g]
        xs, ns = slice(g * W, (g + 1) * W), slice(g * N, (g + 1) * N)
        items.append(dict(g=g, xs=xs, ns=ns, x=xf_ref, b=bf_ref, c=cf_ref, dr=drf_ref, s=sf_ref,
                          y=yf_ref, rev=False, bias_r=pr[:, 0:1], alog_r=pr[:, 1:2]))
        items.append(dict(g=g, xs=xs, ns=ns, x=xb_ref, b=bb_ref, c=cb_ref, dr=drb_ref, s=sb_ref,
                          y=yb_ref, rev=True, bias_r=pr[:, 2:3], alog_r=pr[:, 3:4]))

    for d in items:
        rev = d["rev"]
        d["dtr"] = _softplus(d["dr"][d["g"] * R:(d["g"] + 1) * R, :] + d["bias_r"])
        adt_r = d["dtr"] * (-LOG2_E * jnp.exp(d["alog_r"]))
        pieces = _split3(adt_r)
        d["acs_c"] = sum(lax.dot_general(tri_b[rev], piece, NT_DIMS, preferred_element_type=F32)
                         for piece in pieces)
        acs3 = jnp.dot(jnp.concatenate(pieces, axis=0), tri_tb[rev], preferred_element_type=F32)
        d["acs_r"] = acs3[0:R] + acs3[R:2 * R] + acs3[2 * R:3 * R]
        bm = d["b"][:, d["ns"]]
        cm = d["c"][:, d["ns"]]
        d["cb"] = lax.dot_general(cm, bm, NT_DIMS, preferred_element_type=F32)
        d["bt"] = bm.astype(F32).T

    for d in items:
        acs_c, acs_r = d["acs_c"], d["acs_r"]
        last_r = acs_r[:, 0:1] if d["rev"] else acs_r[:, L - 1:L]
        acs_k = acs_r - jnp.log2(d["dtr"])
        key_scale = jnp.exp2(last_r - acs_k)
        ws, bts, growth = [], [], []
        for r in range(R):
            acs_b = jnp.broadcast_to(acs_c[:, r:r + 1], (L, L))
            seg = acs_b - acs_k[r:r + 1, :]
            lmat = jnp.exp2(jnp.where(tris[d["rev"]], seg, -jnp.inf))
            ws.append((d["cb"] * lmat).astype(BF16))
            bts.append((d["bt"] * key_scale[r:r + 1, :]).astype(BF16))
            growth.append(jnp.exp2(acs_b))
        d["wb"] = jnp.concatenate([jnp.concatenate(ws, axis=1), jnp.concatenate(bts, axis=1)], axis=0)
        lane_lo = lax.broadcasted_iota(jnp.int32, (L, L), 1) < P
        d["growth"] = jnp.concatenate(
            [jnp.where(lane_lo, growth[r], growth[r + 1]) for r in range(0, R, 2)], axis=1)

    for d in items:
        xb16 = d["x"][:, d["xs"]]
        xstack = jnp.concatenate([xb16 * mask for mask in head_masks], axis=0)
        both = jnp.dot(d["wb"], xstack, preferred_element_type=F32)
        d["yd"], d["upd"] = both[0:L], both[L:L + N]
        d["state"] = d["s"][d["g"]]
        d["y_off"] = jnp.dot(d["c"][:, d["ns"]], d["state"].astype(BF16), preferred_element_type=F32)

    for d in items:
        acs_c = d["acs_c"]
        last_c = acs_c[0:1, :] if d["rev"] else acs_c[L - 1:L, :]
        y = d["yd"] + d["y_off"] * d["growth"]
        d["s"][d["g"]] = d["state"] * _expand_heads(jnp.exp2(last_c), head_of_lane) + d["upd"]
        if not d["rev"]:
            y = y + dsk_ref[d["g"]] * d["x"][:, d["xs"]].astype(F32)
        d["y"][:, d["xs"]] = y.astype(d["y"].dtype)


def _ssd_scan(xbc, dt_t, dt_bias, a_log, d_skip, *, groups=8):
    bsz, s, _ = xbc.shape
    G, R, P, N, L = SSD_GROUPS, SSD_HEADS_PER_GROUP, SSD_HEAD_DIM, SSD_D_STATE, SSD_CHUNK
    assert G % groups == 0 and L == 2 * P
    d_inner = G * R * P
    nc = s // L
    xw = groups * R * P
    nw = groups * N
    b_off = d_inner // nw
    c_off = b_off + G // groups
    gblocks = G // groups
    pr = jnp.stack([dt_bias[0].reshape(G, R), a_log[0].reshape(G, R),
                    dt_bias[1].reshape(G, R), a_log[1].reshape(G, R)], axis=2).astype(F32)
    dsk = jnp.repeat(d_skip.astype(F32).reshape(G, 1, R), P, axis=2)

    def fwd(c):
        return c

    def bwd(c):
        return nc - 1 - c

    def dir_specs(d, cm):
        return [
            pl.BlockSpec((None, L, xw), lambda b, g, c: (b, cm(c), g)),
            pl.BlockSpec((None, L, nw), lambda b, g, c: (b, cm(c), b_off + g)),
            pl.BlockSpec((None, L, nw), lambda b, g, c: (b, cm(c), c_off + g)),
            pl.BlockSpec((groups * R, L), lambda b, g, c: (d * gblocks + g, b * nc + cm(c))),
        ]

    in_specs = dir_specs(0, fwd) + dir_specs(1, bwd) + [
        pl.BlockSpec((groups, R, 4), lambda b, g, c: (g, 0, 0)),
        pl.BlockSpec((groups, 1, R * P), lambda b, g, c: (g, 0, 0)),
    ]
    out_sd = jax.ShapeDtypeStruct((bsz, s, d_inner), BF16)
    return pl.pallas_call(
        functools.partial(_ssd_scan_kernel, groups=groups),
        out_shape=(out_sd, out_sd),
        grid=(bsz, G // groups, nc),
        in_specs=in_specs,
        out_specs=(pl.BlockSpec((None, L, xw), lambda b, g, c: (b, c, g)),
                   pl.BlockSpec((None, L, xw), lambda b, g, c: (b, nc - 1 - c, g))),
        scratch_shapes=[pltpu.VMEM((groups, N, R * P), F32), pltpu.VMEM((groups, N, R * P), F32)],
        compiler_params=_params("parallel", "parallel", "arbitrary"),
        name="ssd_scan",
    )(xbc, xbc, xbc, dt_t, xbc, xbc, xbc, dt_t, pr, dsk)


def _ssd_out_kernel(yf_ref, yb_ref, z_ref, g_ref, w_ref, x_ref, o_ref):
    y = (yf_ref[...].astype(F32) + yb_ref[...].astype(F32)) * _silu(z_ref[...].astype(F32))
    yn = _rms(y, g_ref[...]).astype(BF16)
    o_ref[...] = x_ref[...] + jnp.dot(yn, w_ref[...], preferred_element_type=F32)


def _ssd_out(yf, yb, zx, norm_g, out_w, layer, x, *, tm=512):
    t, d_inner = yf.shape
    d = out_w.shape[-1]
    assert t % tm == 0 and out_w.dtype == BF16
    return pl.pallas_call(
        _ssd_out_kernel,
        out_shape=jax.ShapeDtypeStruct((t, d), F32),
        grid=(t // tm,),
        in_specs=[pl.BlockSpec((tm, d_inner), lambda i: (i, 0)),
                  pl.BlockSpec((tm, d_inner), lambda i: (i, 0)),
                  pl.BlockSpec((tm, d_inner), lambda i: (i, 0)),
                  pl.BlockSpec((1, d_inner), lambda i: (0, 0)),
                  _layer_spec(out_w, layer, (d_inner, d), lambda i: (0, 0)),
                  pl.BlockSpec((tm, d), lambda i: (i, 0))],
        out_specs=pl.BlockSpec((tm, d), lambda i: (i, 0)),
        compiler_params=_params("parallel"),
        name="ssd_out",
    )(yf, yb, zx, norm_g.reshape(1, d_inner), out_w, x)


def _ssd_layer(x, norm_g, in_w, layer, conv_w, conv_b, dt_bias, a_log, d_skip, ssd_norm_g, out_w,
               bsz, s):
    d_inner = ssd_norm_g.shape[0]
    conv_ch = conv_w.shape[1]
    zx_cols = d_inner + conv_ch
    zx, dt_t = _mm(x, in_w, layer=layer, g=norm_g, n_cols=zx_cols, side_wt=in_w[layer, :, zx_cols:].T,
                   out_dtype=BF16, tn=zx_cols // 2, name="ssd_in")
    xbc = _ssd_conv(zx.reshape(bsz, s, zx_cols), conv_w, conv_b, d_inner=d_inner)
    yf, yb = _ssd_scan(xbc, dt_t, dt_bias, a_log, d_skip)
    t = bsz * s
    return _ssd_out(yf.reshape(t, d_inner), yb.reshape(t, d_inner), zx, ssd_norm_g, out_w, layer, x)


def _dil_qkv_kernel(x_ref, w_ref, g_ref, o_ref, xs_ref, *, perms, blocks_per_group):
    j = pl.program_id(1)

    @pl.when(j == 0)
    def _():
        x = _rms(x_ref[...], g_ref[...]).astype(BF16)
        for gi, perm in enumerate(perms):
            xs_ref[gi] = _permute_rows(x, *perm)

    o_ref[...] = jnp.dot(xs_ref[j // blocks_per_group], w_ref[...],
                         preferred_element_type=F32).astype(o_ref.dtype)


def _dil_qkv(x, qkv_w, layer, norm_g, perms, *, tm=1024):
    t, k = x.shape
    n_groups = len(perms)
    width = DIL_HEADS * DIL_HEAD_DIM
    assert t % tm == 0 and qkv_w.shape[-1] == 3 * n_groups * width and qkv_w.dtype == BF16
    assert all(tm % pb == 0 and pb % d == 0 for pb, d in perms)
    return pl.pallas_call(
        functools.partial(_dil_qkv_kernel, perms=perms, blocks_per_group=3),
        out_shape=jax.ShapeDtypeStruct((n_groups, t, 3 * width), BF16),
        grid=(t // tm, 3 * n_groups),
        in_specs=[pl.BlockSpec((tm, k), lambda i, j: (i, 0)),
                  _layer_spec(qkv_w, layer, (k, width), lambda i, j: (0, (j % 3) * n_groups + j // 3)),
                  pl.BlockSpec((1, k), lambda i, j: (0, 0))],
        out_specs=pl.BlockSpec((None, tm, width), lambda i, j: (j // 3, i, j % 3)),
        scratch_shapes=[pltpu.VMEM((n_groups, tm, k), BF16)],
        compiler_params=_params("parallel", "arbitrary"),
        name="dil_qkv",
    )(x, qkv_w, norm_g.reshape(1, k))


def _dil_attn_kernel(q_ref, kp_ref, kc_ref, kn_ref, vp_ref, vc_ref, vn_ref, bias_ref,
                     o_ref, lse_ref, *, n_steps, blk, per_step, split_rows):
    j = pl.program_id(1)
    hd = DIL_HEAD_DIM
    first = (j == 0).astype(jnp.int32)
    last = (j == n_steps - 1).astype(jnp.int32)
    piece = lambda n: slice(n * blk, (n + 1) * blk)
    subs = []
    for n in range(per_step):
        if split_rows:
            kv = tuple(divmod(per_step + n + t - 1, per_step) for t in range(3))
            kv = tuple((ri, piece(pi)) for ri, pi in kv)
            variant = (first if n == 0 else 0) + (2 * last if n == per_step - 1 else 0)
        else:
            kv = tuple((ri, piece(n)) for ri in range(3))
            variant = first + 2 * last
        subs.append((piece(n), kv, variant))
    k_refs, v_refs = (kp_ref, kc_ref, kn_ref), (vp_ref, vc_ref, vn_ref)
    lane = lax.broadcasted_iota(jnp.int32, (blk, 2 * hd), 1)
    scale = 1.0 / math.sqrt(hd)
    pairs = DIL_HEADS // 2
    tiles = [(si, a, slice(a * 2 * hd, (a + 1) * 2 * hd))
             for si in range(per_step) for a in range(pairs)]
    scores = []
    for si, a, sl in tiles:
        q_rows, kv, variant = subs[si]
        q = (q_ref[q_rows, sl].astype(F32) * scale).astype(BF16)
        zero = jnp.zeros_like(q)
        qbd = jnp.concatenate([jnp.where(lane < hd, q, zero), jnp.where(lane >= hd, q, zero)], axis=0)
        k = jnp.concatenate([k_refs[ri][rows, sl] for ri, rows in kv], axis=0)
        scores.append(lax.dot_general(k, qbd, NT_DIMS, preferred_element_type=F32)
                      + bias_ref[variant, a])
    probs, lse_rows = [], [[] for _ in range(per_step)]
    for (si, a, sl), s in zip(tiles, scores):
        m = jnp.max(s, axis=0, keepdims=True)
        p = jnp.exp(s - m)
        den = jnp.sum(p, axis=0, keepdims=True)
        probs.append((p * (1.0 / den)).astype(BF16))
        lse_rows[si].append(m + jnp.log(den))
    for si, rows_ in enumerate(lse_rows):
        t = jnp.concatenate(rows_, axis=0).T
        lse_ref[subs[si][0], :] = jnp.concatenate([t[0:blk], t[blk:2 * blk]], axis=1)
    for (si, a, sl), pn in zip(tiles, probs):
        q_rows, kv, _ = subs[si]
        v = jnp.concatenate([v_refs[ri][rows, sl] for ri, rows in kv], axis=0)
        r = lax.dot_general(pn, v, TN_DIMS, preferred_element_type=F32)
        o_ref[q_rows, sl] = jnp.where(lane < hd, r[0:blk], r[blk:2 * blk]).astype(o_ref.dtype)


def _dil_bias(table, window, dilation):
    blk = window // (2 * dilation)
    delta = np.arange(3 * blk)[:, None] - blk - np.arange(blk)[None, :]
    bucket = _rel_bucket_np(delta * dilation)
    band = np.abs(delta) <= blk
    key_blk = np.arange(3 * blk)[:, None] // blk
    idx = []
    for variant in range(4):
        ok = band & ~((key_blk == 0) & bool(variant & 1)) & ~((key_blk == 2) & bool(variant & 2))
        idx.append(np.where(ok, bucket, MASKED_BUCKET))
    tiles = _bias_tiles(table, np.stack(idx).astype(np.int32))
    tiles = tiles.reshape(DIL_HEADS // 2, 2, 4, 3 * blk, blk)
    return jnp.transpose(tiles, (2, 0, 3, 1, 4)).reshape(4, DIL_HEADS // 2, 3 * blk, 2 * blk)


def _dil_group(qkv, gi, bias, window, dilation, bsz, s):
    width = DIL_HEADS * DIL_HEAD_DIM
    d = dilation
    n = s // d
    blk = window // (2 * d)
    assert n % blk == 0 and blk % 8 == 0
    nb = n // blk
    pairs = DIL_HEADS // 2
    split_rows = d == 1
    per_step = min(DIL_TILES_PER_STEP, nb if split_rows else d)
    n_steps, n_res = (nb // per_step, 1) if split_rows else (nb, d // per_step)
    assert (nb if split_rows else d) % per_step == 0

    def spec(which, shift):
        def index(b, jj, r):
            jn = jnp.clip(jj + shift, 0, n_steps - 1)
            return (gi, (b * n_steps + jn) * n_res + r, which)
        return pl.BlockSpec((None, per_step * blk, width), index)

    o, lse = pl.pallas_call(
        functools.partial(_dil_attn_kernel, n_steps=n_steps, blk=blk, per_step=per_step,
                          split_rows=split_rows),
        out_shape=(jax.ShapeDtypeStruct((bsz * s, width), BF16),
                   jax.ShapeDtypeStruct((bsz * s, DIL_HEADS), F32)),
        grid=(bsz, n_steps, n_res),
        in_specs=[spec(0, 0), spec(1, -1), spec(1, 0), spec(1, 1),
                  spec(2, -1), spec(2, 0), spec(2, 1),
                  pl.BlockSpec((4, pairs, 3 * blk, 2 * blk), lambda b, jj, r: (0, 0, 0, 0))],
        out_specs=(pl.BlockSpec((per_step * blk, width),
                                lambda b, jj, r: ((b * n_steps + jj) * n_res + r, 0)),
                   pl.BlockSpec((per_step * blk, DIL_HEADS),
                                lambda b, jj, r: ((b * n_steps + jj) * n_res + r, 0))),
        compiler_params=_params("parallel", "parallel", "parallel"),
        name=f"dil_attn_d{d}",
    )(qkv, qkv, qkv, qkv, qkv, qkv, qkv, bias)
    return o, lse


def _dil_out_kernel(o0_ref, o1_ref, o2_ref, l0_ref, l1_ref, l2_ref, w_ref, x_ref, o_ref, *, perms):
    width = o0_ref.shape[1]
    nh = DIL_HEADS
    os_nat, lses = [], []
    for o_ref_g, l_ref, (pb, d) in zip((o0_ref, o1_ref, o2_ref), (l0_ref, l1_ref, l2_ref), perms):
        if d == 1:
            os_nat.append(o_ref_g[...].astype(F32))
            lses.append(l_ref[...])
            continue
        hi, mid, _ = _split3(l_ref[...])
        filler = jnp.zeros((hi.shape[0], LANE_TILE - 2 * nh), BF16)
        both = _permute_rows(jnp.concatenate([o_ref_g[...], hi, mid, filler], axis=1), pb, d,
                             inverse=True, out_dtype=F32)
        os_nat.append(both[:, 0:width])
        lses.append(both[:, width:width + nh] + both[:, width + nh:width + 2 * nh])
    m = jnp.maximum(jnp.maximum(lses[0], lses[1]), lses[2])
    es = [jnp.exp(l - m) for l in lses]
    inv = 1.0 / (es[0] + es[1] + es[2])
    col = lax.broadcasted_iota(jnp.int32, (DIL_HEADS, width), 0)
    head_of_col = 2 * (col % (DIL_HEADS // 2)) + col // (DIL_HEADS // 2)
    head_of_lane = lax.broadcasted_iota(jnp.int32, (DIL_HEADS, width), 1) // DIL_HEAD_DIM
    expand = (head_of_col == head_of_lane).astype(BF16)
    y = None
    for e, o_nat in zip(es, os_nat):
        w_g = e * inv
        w_hi = w_g.astype(BF16)
        w_lo = (w_g - w_hi.astype(F32)).astype(BF16)
        wts = (jnp.dot(w_hi, expand, preferred_element_type=F32)
               + jnp.dot(w_lo, expand, preferred_element_type=F32))
        term = wts * o_nat
        y = term if y is None else y + term
    o_ref[...] = x_ref[...] + jnp.dot(y.astype(BF16), w_ref[...], preferred_element_type=F32)


def _dil_out(os_, lses, perms, out_w, layer, x, *, tm=1024):
    t, width = os_[0].shape
    d = out_w.shape[-1]
    assert t % tm == 0 and all(tm % pb == 0 for pb, _ in perms) and out_w.dtype == BF16
    row = lambda i: (i, 0)
    return pl.pallas_call(
        functools.partial(_dil_out_kernel, perms=perms),
        out_shape=jax.ShapeDtypeStruct((t, d), F32),
        grid=(t // tm,),
        in_specs=[pl.BlockSpec((tm, width), row)] * 3 + [pl.BlockSpec((tm, DIL_HEADS), row)] * 3
                 + [_layer_spec(out_w, layer, (width, d), lambda i: (0, 0)), pl.BlockSpec((tm, d), row)],
        out_specs=pl.BlockSpec((tm, d), row),
        compiler_params=_params("parallel"),
        name="dil_out",
    )(*os_, *lses, out_w, x)


def _dil_layer(x, norm_g, qkv_w, out_w, layer, table, bsz, s):
    perms = tuple(((win // (2 * dil)) * dil, dil) for win, dil in DIL_CONFIGS)
    qkv = _dil_qkv(x, qkv_w, layer, norm_g, perms)
    os_, lses = [], []
    for gi, (win, dil) in enumerate(DIL_CONFIGS):
        o, lse = _dil_group(qkv, gi, _dil_bias(table, win, dil), win, dil, bsz, s)
        os_.append(o)
        lses.append(lse)
    return _dil_out(os_, lses, perms, out_w, layer, x)


def _diff_attn_kernel(q_ref, k_ref, v_ref, bank_ref, lam_ref, g_ref, o_ref,
                      vx_ref, q2_ref, s_ref, m_ref, acc_ref, *, tq, tk, seq, lam_init):
    qi = pl.program_id(2)
    tile = DIFF_TILE
    hd = DIFF_HEAD_DIM
    nv = DIFF_V_DIM

    @pl.when(qi == 0)
    def _():
        lane = lax.broadcasted_iota(jnp.int32, (seq, nv), 1)
        vx_ref[:, 0:nv] = v_ref[...]
        vx_ref[:, nv:2 * nv] = jnp.where(lane == 0, 1.0, 0.0).astype(BF16)

    q = (q_ref[...].astype(F32) * (LOG2_E / math.sqrt(hd))).astype(BF16)
    lane = lax.broadcasted_iota(jnp.int32, q.shape, 1)
    zero = jnp.zeros_like(q)
    q2_ref[0:tq, :] = jnp.where(lane < hd, q, zero)
    q2_ref[tq:2 * tq, :] = jnp.where(lane >= hd, q, zero)
    m_ref[...] = jnp.full(m_ref.shape, -jnp.inf, F32)
    acc_ref[...] = jnp.zeros(acc_ref.shape, F32)

    n_chunks = seq // tk
    kt = tk // tile

    def scores(kc, slot):
        start = pl.multiple_of(kc * tk, tk)
        s_ref[slot] = lax.dot_general(q2_ref[...], k_ref[pl.ds(start, tk), :], NT_DIMS,
                                      preferred_element_type=F32)

    def consume(kc, slot):
        start = pl.multiple_of(kc * tk, tk)
        vx = vx_ref[pl.ds(start, tk), :]
        for i in range(2):
            rows = []
            for a in range(tq // tile):
                cols = []
                for b in range(kt):
                    dlt = (kc * kt + b) - (qi * (tq // tile) + a)
                    idx = jnp.clip(dlt, -DIFF_BANK_REACH, DIFF_BANK_REACH) + DIFF_BANK_REACH
                    r0 = i * tq + a * tile
                    cols.append(s_ref[slot, r0:r0 + tile, b * tile:(b + 1) * tile]
                                + bank_ref[i, idx])
                rows.append(cols)
            part = jnp.concatenate([functools.reduce(jnp.maximum, cols) for cols in rows], axis=0)
            row_max = jnp.broadcast_to(jnp.max(part, axis=-1, keepdims=True), part.shape)
            m_old = m_ref[i]
            m_new = jnp.maximum(m_old, row_max)
            alpha = jnp.exp2(m_old - m_new)
            p = jnp.concatenate(
                [jnp.concatenate([jnp.exp2(c - m_new[a * tile:(a + 1) * tile]).astype(BF16)
                                  for c in cols], axis=1) for a, cols in enumerate(rows)], axis=0)
            pv = jnp.dot(p, vx, preferred_element_type=F32)
            acc_ref[i] = jnp.concatenate([alpha, alpha], axis=1) * acc_ref[i] + pv
            m_ref[i] = m_new

    scores(0, 0)

    def body(j, carry):
        for slot in range(2):
            kc = 2 * j + slot
            scores(kc + 1, 1 - slot)
            consume(kc, slot)
        return carry

    lax.fori_loop(0, n_chunks // 2 - 1, body, 0)
    scores(n_chunks - 1, 1)
    consume(n_chunks - 2, 0)
    consume(n_chunks - 1, 1)

    lf = lam_ref[...]
    lam_full = (jnp.exp(jnp.sum(lf[0:1] * lf[1:2], axis=-1, keepdims=True))
                - jnp.exp(jnp.sum(lf[2:3] * lf[3:4], axis=-1, keepdims=True)) + lam_init)
    o = (acc_ref[0, :, 0:nv] / acc_ref[0, :, nv:nv + 1]
         - lam_full * (acc_ref[1, :, 0:nv] / acc_ref[1, :, nv:nv + 1]))
    o = _rms(o, g_ref[...]) * (1.0 - lam_init)
    o_ref[...] = o.astype(o_ref.dtype)


def _diff_attn(qkv, bank, lam, subln_g, lam_init, bsz, s, *, tq=1024, tk=1024):
    hw = 2 * DIFF_HEAD_DIM
    H = DIFF_HEADS
    assert s % tq == 0 and s % (2 * tk) == 0 and tq % DIFF_TILE == 0 and tk % DIFF_TILE == 0
    n_bank = bank.shape[1]
    return pl.pallas_call(
        functools.partial(_diff_attn_kernel, tq=tq, tk=tk, seq=s, lam_init=lam_init),
        out_shape=jax.ShapeDtypeStruct((bsz, s, H * DIFF_V_DIM), BF16),
        grid=(bsz, H, s // tq),
        in_specs=[pl.BlockSpec((None, tq, hw), lambda b, h, i: (b, i, h)),
                  pl.BlockSpec((None, s, hw), lambda b, h, i: (b, 0, H + h)),
                  pl.BlockSpec((None, s, DIFF_V_DIM), lambda b, h, i: (b, 0, 2 * H + h)),
                  pl.BlockSpec((2, n_bank, DIFF_TILE, DIFF_TILE), lambda b, h, i: (h, 0, 0, 0)),
                  pl.BlockSpec(lam.shape, lambda b, h, i: (0, 0)),
                  pl.BlockSpec((1, DIFF_V_DIM), lambda b, h, i: (0, 0))],
        out_specs=pl.BlockSpec((None, tq, DIFF_V_DIM), lambda b, h, i: (b, i, h)),
        scratch_shapes=[pltpu.VMEM((s, 2 * DIFF_V_DIM), BF16),
                        pltpu.VMEM((2 * tq, hw), BF16),
                        pltpu.VMEM((2, 2 * tq, tk), F32),
                        pltpu.VMEM((2, tq, DIFF_TILE), F32),
                        pltpu.VMEM((2, tq, 2 * DIFF_V_DIM), F32)],
        compiler_params=_params("parallel", "parallel", "arbitrary"),
        name="diff_attn",
    )(qkv, qkv, qkv, bank, lam.astype(F32), subln_g.reshape(1, DIFF_V_DIM).astype(F32))


def _diff_bank_buckets():
    reach = DIFF_BANK_REACH
    local = np.arange(DIFF_TILE)[None, :] - np.arange(DIFF_TILE)[:, None]
    tiles = np.stack([_rel_bucket_np(local + (t - reach) * DIFF_TILE) for t in range(2 * reach + 1)])
    assert (tiles[0] == tiles[0, 0, 0]).all() and (tiles[-1] == tiles[-1, 0, 0]).all()
    return tiles


def _diff_layer(x, norm_g, qkv_w, lam, subln_g, out_w, layer, table, layer_idx, bsz, s):
    d = x.shape[1]
    lam_init = 0.8 - 0.6 * math.exp(-0.3 * layer_idx)
    qkv = _mm(x, qkv_w, layer=layer, g=norm_g, out_dtype=BF16, tn=qkv_w.shape[-1], name="diff_qkv")
    bank = _bias_tiles(table, _diff_bank_buckets(), scale=LOG2_E)
    o = _diff_attn(qkv.reshape(bsz, s, 3 * d), bank, lam, subln_g, lam_init, bsz, s)
    return _mm(o.reshape(bsz * s, d), out_w, layer=layer, res=x, name="diff_out")


def kernel(x, rel_bias, norm_mix_g, norm_mlp_g, mlp_w1, mlp_w2, ssd_in_w, ssd_conv_w, ssd_conv_b, ssd_dt_bias, ssd_a_log, ssd_d, ssd_norm_g, ssd_out_w, dil_qkv_w, dil_out_w, diff_qkv_w, diff_lambda, diff_subln_g, diff_out_w, final_norm_g):
    bsz, s, d = x.shape
    depth = norm_mix_g.shape[0]
    (mlp_w1, mlp_w2, ssd_in_w, ssd_out_w, dil_qkv_w, dil_out_w, diff_qkv_w, diff_out_w) = (
        w.astype(BF16) for w in (mlp_w1, mlp_w2, ssd_in_w, ssd_out_w, dil_qkv_w, dil_out_w,
                                 diff_qkv_w, diff_out_w))
    h = x.reshape(bsz * s, d)
    for i in range(depth):
        kind, j = i % 3, i // 3
        if kind == 0:
            h = _ssd_layer(h, norm_mix_g[i], ssd_in_w, j, ssd_conv_w[j], ssd_conv_b[j],
                           ssd_dt_bias[j], ssd_a_log[j], ssd_d[j], ssd_norm_g[j], ssd_out_w, bsz, s)
        elif kind == 1:
            h = _dil_layer(h, norm_mix_g[i], dil_qkv_w, dil_out_w, j, rel_bias, bsz, s)
        else:
            h = _diff_layer(h, norm_mix_g[i], diff_qkv_w, diff_lambda[j], diff_subln_g[j],
                            diff_out_w, j, rel_bias, i, bsz, s)
        h = _mlp(h, norm_mlp_g[i], mlp_w1, mlp_w2, i,
                 final_norm_g if i == depth - 1 else None)
    return h.reshape(bsz, s, d)
```

```python
import functools
import math

import numpy as np
import jax
import jax.numpy as jnp
from jax import lax
from jax.experimental import pallas as pl
from jax.experimental.pallas import tpu as pltpu

F32 = jnp.float32
BF16 = jnp.bfloat16

RMS_EPS = 1e-6
NEG_INF = -1e30
LOG2_E = math.log2(math.e)

LANE_TILE = 128
V7X_VMEM_BYTES = 64 * 1024 * 1024
VMEM_LIMIT_BYTES = V7X_VMEM_BYTES - 8 * 1024 * 1024

SSD_HEAD_DIM = 64
SSD_GROUPS = 8
SSD_HEADS_PER_GROUP = 4
SSD_D_STATE = 128
SSD_CONV_WIDTH = 5
SSD_CHUNK = 128
SSD_GROUP_WIDTH = SSD_HEADS_PER_GROUP * SSD_HEAD_DIM

DIL_CONFIGS = ((128, 1), (512, 4), (2048, 16))
DIL_HEADS = 16
DIL_HEAD_DIM = 64
DIL_TILES_PER_STEP = 8

DIFF_HEADS = 8
DIFF_HEAD_DIM = 64
DIFF_V_DIM = 128
DIFF_TILE = 128
DIFF_BANK_REACH = 6

REL_BUCKETS = 32
REL_MAX_DIST = 1024
REL_BIAS_HEADS = 16
MASKED_BUCKET = REL_BUCKETS

NT_DIMS = (((1,), (1,)), ((), ()))
TN_DIMS = (((0,), (0,)), ((), ()))


def _params(*semantics):
    return pltpu.CompilerParams(dimension_semantics=semantics,
                                vmem_limit_bytes=VMEM_LIMIT_BYTES)


def _rms(x, g):
    ms = jnp.mean(x * x, axis=-1, keepdims=True)
    return x * lax.rsqrt(ms + RMS_EPS) * g


def _silu(x):
    return x * (1.0 / (1.0 + jnp.exp(-x)))


def _softplus(x):
    return jnp.maximum(x, 0.0) + jnp.log1p(jnp.exp(-jnp.abs(x)))


def _residue_major_perm(pb, d, inverse=False):
    i = lax.broadcasted_iota(jnp.int32, (pb, pb), 0)
    j = lax.broadcasted_iota(jnp.int32, (pb, pb), 1)
    per = pb // d
    if inverse:
        i, j = j, i
    return (j == (i % per) * d + i // per).astype(BF16)


def _permute_rows(x, pb, d, inverse=False, out_dtype=BF16):
    if d == 1:
        return x.astype(out_dtype)
    pm = _residue_major_perm(pb, d, inverse)
    blocks = [jnp.dot(pm, x[r0:r0 + pb], preferred_element_type=F32).astype(out_dtype)
              for r0 in range(0, x.shape[0], pb)]
    return blocks[0] if len(blocks) == 1 else jnp.concatenate(blocks, axis=0)


def _mm_kernel(*refs, norm, res, side):
    refs = list(refs)
    x_ref, w_ref = refs[0], refs[1]
    pos = 2
    g_ref = r_ref = ws_ref = os_ref = None
    if norm:
        g_ref = refs[pos]
        pos += 1
    if res:
        r_ref = refs[pos]
        pos += 1
    if side:
        ws_ref = refs[pos]
        pos += 1
    o_ref = refs[pos]
    pos += 1
    if side:
        os_ref = refs[pos]
        pos += 1
    xs_ref = refs[pos]

    @pl.when(pl.program_id(1) == 0)
    def _():
        x = x_ref[...].astype(F32)
        if norm:
            x = _rms(x, g_ref[...])
        x = x.astype(BF16)
        xs_ref[...] = x
        if side:
            os_ref[...] = lax.dot_general(ws_ref[...], x, NT_DIMS, preferred_element_type=F32)

    acc = jnp.dot(xs_ref[...], w_ref[...], preferred_element_type=F32)
    if res:
        acc = r_ref[...] + acc
    o_ref[...] = acc.astype(o_ref.dtype)


def _layer_spec(w, layer, block, index):
    if w.ndim == 2:
        return pl.BlockSpec(block, index)
    return pl.BlockSpec((None,) + block, lambda *g: (layer,) + index(*g))


def _mm(x, w, *, layer=0, g=None, res=None, n_cols=None, side_wt=None, out_dtype=F32,
        tm=1024, tn=1024, name="mm"):
    t, k = x.shape
    n = w.shape[-1] if n_cols is None else n_cols
    tn = min(tn, n)
    tm = min(tm, t)
    assert t % tm == 0 and n % tn == 0 and w.shape[-2] == k and w.dtype == BF16
    in_specs = [pl.BlockSpec((tm, k), lambda i, j: (i, 0)),
                _layer_spec(w, layer, (k, tn), lambda i, j: (0, j))]
    args = [x, w]
    if g is not None:
        in_specs.append(pl.BlockSpec((1, k), lambda i, j: (0, 0)))
        args.append(g.reshape(1, k))
    if res is not None:
        in_specs.append(pl.BlockSpec((tm, tn), lambda i, j: (i, j)))
        args.append(res)
    out_shape = jax.ShapeDtypeStruct((t, n), out_dtype)
    out_specs = pl.BlockSpec((tm, tn), lambda i, j: (i, j))
    if side_wt is not None:
        m = side_wt.shape[0]
        in_specs.append(pl.BlockSpec((m, k), lambda i, j: (0, 0)))
        args.append(side_wt)
        out_shape = (out_shape, jax.ShapeDtypeStruct((m, t), F32))
        out_specs = (out_specs, pl.BlockSpec((m, tm), lambda i, j: (0, i)))
    return pl.pallas_call(
        functools.partial(_mm_kernel, norm=g is not None, res=res is not None,
                          side=side_wt is not None),
        out_shape=out_shape,
        grid=(t // tm, n // tn),
        in_specs=in_specs,
        out_specs=out_specs,
        scratch_shapes=[pltpu.VMEM((tm, k), BF16)],
        compiler_params=_params("parallel", "arbitrary"),
        name=name,
    )(*args)


def _mlp_kernel(*refs, final):
    if final:
        x_ref, g_ref, w1_ref, w2_ref, fg_ref, o_ref, xn_ref, acc_ref = refs
    else:
        x_ref, g_ref, w1_ref, w2_ref, o_ref, xn_ref, acc_ref = refs
        fg_ref = None
    f = pl.program_id(1)

    @pl.when(f == 0)
    def _():
        x = x_ref[...]
        xn_ref[...] = _rms(x, g_ref[...]).astype(BF16)
        acc_ref[...] = x

    h = jnp.dot(xn_ref[...], w1_ref[...], preferred_element_type=F32)
    h = jnp.square(jnp.maximum(h, 0.0)).astype(BF16)
    acc_ref[...] += jnp.dot(h, w2_ref[...], preferred_element_type=F32)

    @pl.when(f == pl.num_programs(1) - 1)
    def _():
        y = acc_ref[...]
        if final:
            y = _rms(y, fg_ref[...])
        o_ref[...] = y


def _mlp(x, g, w1, w2, layer, final_g=None, *, tm=1024, tf=1024):
    t, d = x.shape
    ff = w1.shape[-1]
    assert t % tm == 0 and ff % tf == 0 and w1.dtype == BF16 and w2.dtype == BF16
    final = final_g is not None
    in_specs = [pl.BlockSpec((tm, d), lambda i, f: (i, 0)),
                pl.BlockSpec((1, d), lambda i, f: (0, 0)),
                _layer_spec(w1, layer, (d, tf), lambda i, f: (0, f)),
                _layer_spec(w2, layer, (tf, d), lambda i, f: (f, 0))]
    args = [x, g.reshape(1, d), w1, w2]
    if final:
        in_specs.append(pl.BlockSpec((1, d), lambda i, f: (0, 0)))
        args.append(final_g.reshape(1, d))
    return pl.pallas_call(
        functools.partial(_mlp_kernel, final=final),
        out_shape=jax.ShapeDtypeStruct((t, d), F32),
        grid=(t // tm, ff // tf),
        in_specs=in_specs,
        out_specs=pl.BlockSpec((tm, d), lambda i, f: (i, 0)),
        scratch_shapes=[pltpu.VMEM((tm, d), BF16), pltpu.VMEM((tm, d), F32)],
        compiler_params=_params("parallel", "arbitrary"),
        name="mlp",
    )(*args)


def _rel_bucket_np(rel):
    half = REL_BUCKETS // 2
    max_exact = half // 2
    rel = np.asarray(rel, np.int32)
    ret = np.where(rel > 0, half, 0)
    n = np.abs(rel)
    nf = np.maximum(n, 1).astype(np.float32)
    scale = np.float32(math.log(REL_MAX_DIST / max_exact))
    large = max_exact + (np.log(nf / np.float32(max_exact)) / scale
                         * np.float32(half - max_exact)).astype(np.int32)
    large = np.minimum(large, half - 1)
    return (ret + np.where(n < max_exact, n, large)).astype(np.int32)


def _bias_tiles_kernel(tbl_ref, idx_ref, o_ref, *, scale):
    h = pl.program_id(0)
    idx = idx_ref[...]
    acc = jnp.zeros(idx.shape, F32)
    for b in range(REL_BUCKETS):
        acc = jnp.where(idx == b, tbl_ref[b, h] * scale, acc)
    o_ref[...] = jnp.where(idx == MASKED_BUCKET, NEG_INF, acc)


def _bias_tiles(table, bucket_idx, scale=1.0):
    n, r, c = bucket_idx.shape
    return pl.pallas_call(
        functools.partial(_bias_tiles_kernel, scale=scale),
        out_shape=jax.ShapeDtypeStruct((REL_BIAS_HEADS, n, r, c), F32),
        grid=(REL_BIAS_HEADS,),
        in_specs=[pl.BlockSpec(memory_space=pltpu.SMEM),
                  pl.BlockSpec((n, r, c), lambda h: (0, 0, 0))],
        out_specs=pl.BlockSpec((None, n, r, c), lambda h: (h, 0, 0, 0)),
        compiler_params=_params("parallel"),
        name="bias_tiles",
    )(table.astype(F32), jnp.asarray(bucket_idx))


def _bias_pair_tiles_kernel(tbl_ref, idx_ref, o_ref):
    a = pl.program_id(0)
    idx = idx_ref[...]
    half = idx.shape[-1] // 2
    first_head = lax.broadcasted_iota(jnp.int32, idx.shape, idx.ndim - 1) < half
    acc = jnp.zeros(idx.shape, F32)
    for b in range(REL_BUCKETS):
        val = jnp.where(first_head, tbl_ref[b, 2 * a], tbl_ref[b, 2 * a + 1])
        acc = jnp.where(idx == b, val, acc)
    o_ref[...] = jnp.where(idx == MASKED_BUCKET, NEG_INF, acc)


def _bias_pair_tiles(table, bucket_idx):
    n, r, c = bucket_idx.shape
    pairs = REL_BIAS_HEADS // 2
    idx2 = np.concatenate([bucket_idx, bucket_idx], axis=-1)
    return pl.pallas_call(
        _bias_pair_tiles_kernel,
        out_shape=jax.ShapeDtypeStruct((n, pairs, r, 2 * c), F32),
        grid=(pairs,),
        in_specs=[pl.BlockSpec(memory_space=pltpu.SMEM),
                  pl.BlockSpec((n, r, 2 * c), lambda a: (0, 0, 0))],
        out_specs=pl.BlockSpec((n, None, r, 2 * c), lambda a: (0, a, 0, 0)),
        compiler_params=_params("parallel"),
        name="bias_pair_tiles",
    )(table.astype(F32), jnp.asarray(idx2))


def _conv_kernel(x_ref, w_ref, b_ref, o_ref, xp_ref, *, rows):
    s, tc = x_ref.shape
    pad = 16
    half = SSD_CONV_WIDTH // 2
    xp_ref[0:pad, :] = jnp.zeros((pad, tc), BF16)
    xp_ref[s + pad:s + 2 * pad, :] = jnp.zeros((pad, tc), BF16)
    xp_ref[pad:s + pad, :] = x_ref[...]
    win = rows + 2 * pad
    side_taps = [k for k in range(SSD_CONV_WIDTH) if k != half]
    out_row = lax.broadcasted_iota(jnp.int32, (rows, win), 0)
    win_row = lax.broadcasted_iota(jnp.int32, (rows, win), 1)
    sel = jnp.concatenate([(win_row == out_row + (pad + k - half)).astype(BF16) for k in side_taps],
                          axis=0)
    w = w_ref[...]
    b = b_ref[...]
    for ch in range(s // rows):
        r0 = ch * rows
        window = xp_ref[r0:r0 + win, :]
        shifted = jnp.dot(sel, window, preferred_element_type=F32)
        acc = b + w[half:half + 1, :] * window[pad:pad + rows].astype(F32)
        for t, k in enumerate(side_taps):
            acc = acc + w[k:k + 1, :] * shifted[t * rows:(t + 1) * rows]
        o_ref[r0:r0 + rows, :] = _silu(acc).astype(o_ref.dtype)


def _ssd_conv(zx, conv_w, conv_b, *, d_inner, tc=256, rows=128):
    bsz, s, _ = zx.shape
    c = conv_w.shape[1]
    off = d_inner // tc
    assert zx.dtype == BF16 and s % rows == 0 and rows % 16 == 0
    return pl.pallas_call(
        functools.partial(_conv_kernel, rows=rows),
        out_shape=jax.ShapeDtypeStruct((bsz, s, c), BF16),
        grid=(bsz, c // tc),
        in_specs=[pl.BlockSpec((None, s, tc), lambda b, j: (b, 0, off + j)),
                  pl.BlockSpec((SSD_CONV_WIDTH, tc), lambda b, j: (0, j)),
                  pl.BlockSpec((1, tc), lambda b, j: (0, j))],
        out_specs=pl.BlockSpec((None, s, tc), lambda b, j: (b, 0, j)),
        scratch_shapes=[pltpu.VMEM((s + 32, tc), BF16)],
        compiler_params=_params("parallel", "parallel"),
        name="ssd_conv",
    )(zx, conv_w, conv_b.reshape(1, c))


def _expand_heads(v, head_of_lane):
    r_heads = v.shape[1]
    out = v[:, r_heads - 1:r_heads]
    for r in range(r_heads - 2, -1, -1):
        out = jnp.where(head_of_lane == r, v[:, r:r + 1], out)
    return out


def _split3(x):
    hi = x.astype(BF16)
    r1 = x - hi.astype(F32)
    mid = r1.astype(BF16)
    lo = (r1 - mid.astype(F32)).astype(BF16)
    return hi, mid, lo


def _ssd_scan_kernel(xf_ref, bf_ref, cf_ref, drf_ref, xb_ref, bb_ref, cb_ref, drb_ref,
                     pr_ref, dsk_ref, yf_ref, yb_ref, sf_ref, sb_ref, *, groups):
    @pl.when(pl.program_id(2) == 0)
    def _():
        sf_ref[...] = jnp.zeros(sf_ref.shape, F32)
        sb_ref[...] = jnp.zeros(sb_ref.shape, F32)

    L, R, P, N = SSD_CHUNK, SSD_HEADS_PER_GROUP, SSD_HEAD_DIM, SSD_D_STATE
    W = SSD_GROUP_WIDTH
    row = lax.broadcasted_iota(jnp.int32, (L, L), 0)
    col = lax.broadcasted_iota(jnp.int32, (L, L), 1)
    head_of_lane = lax.broadcasted_iota(jnp.int32, (1, W), 1) // P
    head_of_lane_x = lax.broadcasted_iota(jnp.int32, (L, W), 1) // P
    head_masks = [(head_of_lane_x == r).astype(BF16) for r in range(R)]
    tris = {False: col <= row, True: col >= row}
    tri_b = {rev: t.astype(BF16) for rev, t in tris.items()}
    tri_tb = {False: (row <= col).astype(BF16), True: (row >= col).astype(BF16)}

    items = []
    for g in range(groups):
        pr = pr_ref[g]
        xs, ns = slice(g * W, (g + 1) * W), slice(g * N, (g + 1) * N)
        items.append(dict(g=g, xs=xs, ns=ns, x=xf_ref, b=bf_ref, c=cf_ref, dr=drf_ref, s=sf_ref,
                          y=yf_ref, rev=False, bias_r=pr[:, 0:1], alog_r=pr[:, 1:2]))
        items.append(dict(g=g, xs=xs, ns=ns, x=xb_ref, b=bb_ref, c=cb_ref, dr=drb_ref, s=sb_ref,
                          y=yb_ref, rev=True, bias_r=pr[:, 2:3], alog_r=pr[:, 3:4]))

    for d in items:
        rev = d["rev"]
        d["dtr"] = _softplus(d["dr"][d["g"] * R:(d["g"] + 1) * R, :] + d["bias_r"])
        adt_r = d["dtr"] * (-LOG2_E * jnp.exp(d["alog_r"]))
        pieces = _split3(adt_r)
        d["acs_c"] = sum(lax.dot_general(tri_b[rev], piece, NT_DIMS, preferred_element_type=F32)
                         for piece in pieces)
        acs3 = jnp.dot(jnp.concatenate(pieces, axis=0), tri_tb[rev], preferred_element_type=F32)
        d["acs_r"] = acs3[0:R] + acs3[R:2 * R] + acs3[2 * R:3 * R]
        bm = d["b"][:, d["ns"]]
        cm = d["c"][:, d["ns"]]
        d["cb"] = lax.dot_general(cm, bm, NT_DIMS, preferred_element_type=F32)
        d["bt"] = bm.astype(F32).T

    for d in items:
        acs_c, acs_r = d["acs_c"], d["acs_r"]
        last_r = acs_r[:, 0:1] if d["rev"] else acs_r[:, L - 1:L]
        acs_k = acs_r - jnp.log2(d["dtr"])
        key_scale = jnp.exp2(last_r - acs_k)
        ws, bts, growth = [], [], []
        for r in range(R):
            acs_b = jnp.broadcast_to(acs_c[:, r:r + 1], (L, L))
            seg = acs_b - acs_k[r:r + 1, :]
            lmat = jnp.exp2(jnp.where(tris[d["rev"]], seg, -jnp.inf))
            ws.append((d["cb"] * lmat).astype(BF16))
            bts.append((d["bt"] * key_scale[r:r + 1, :]).astype(BF16))
            growth.append(jnp.exp2(acs_b))
        d["wb"] = jnp.concatenate([jnp.concatenate(ws, axis=1), jnp.concatenate(bts, axis=1)], axis=0)
        lane_lo = lax.broadcasted_iota(jnp.int32, (L, L), 1) < P
        d["growth"] = jnp.concatenate(
            [jnp.where(lane_lo, growth[r], growth[r + 1]) for r in range(0, R, 2)], axis=1)

    for d in items:
        xb16 = d["x"][:, d["xs"]]
        xstack = jnp.concatenate([xb16 * mask for mask in head_masks], axis=0)
        both = jnp.dot(d["wb"], xstack, preferred_element_type=F32)
        d["yd"], d["upd"] = both[0:L], both[L:L + N]
        d["state"] = d["s"][d["g"]]
        d["y_off"] = jnp.dot(d["c"][:, d["ns"]], d["state"].astype(BF16), preferred_element_type=F32)

    for d in items:
        acs_c = d["acs_c"]
        last_c = acs_c[0:1, :] if d["rev"] else acs_c[L - 1:L, :]
        y = d["yd"] + d["y_off"] * d["growth"]
        d["s"][d["g"]] = d["state"] * _expand_heads(jnp.exp2(last_c), head_of_lane) + d["upd"]
        if not d["rev"]:
            y = y + dsk_ref[d["g"]] * d["x"][:, d["xs"]].astype(F32)
        d["y"][:, d["xs"]] = y.astype(d["y"].dtype)


def _ssd_scan(xbc, dt_t, dt_bias, a_log, d_skip, *, groups=8):
    bsz, s, _ = xbc.shape
    G, R, P, N, L = SSD_GROUPS, SSD_HEADS_PER_GROUP, SSD_HEAD_DIM, SSD_D_STATE, SSD_CHUNK
    assert G % groups == 0 and L == 2 * P
    d_inner = G * R * P
    nc = s // L
    xw = groups * R * P
    nw = groups * N
    b_off = d_inner // nw
    c_off = b_off + G // groups
    gblocks = G // groups
    pr = jnp.stack([dt_bias[0].reshape(G, R), a_log[0].reshape(G, R),
                    dt_bias[1].reshape(G, R), a_log[1].reshape(G, R)], axis=2).astype(F32)
    dsk = jnp.repeat(d_skip.astype(F32).reshape(G, 1, R), P, axis=2)

    def fwd(c):
        return c

    def bwd(c):
        return nc - 1 - c

    def dir_specs(d, cm):
        return [
            pl.BlockSpec((None, L, xw), lambda b, g, c: (b, cm(c), g)),
            pl.BlockSpec((None, L, nw), lambda b, g, c: (b, cm(c), b_off + g)),
            pl.BlockSpec((None, L, nw), lambda b, g, c: (b, cm(c), c_off + g)),
            pl.BlockSpec((groups * R, L), lambda b, g, c: (d * gblocks + g, b * nc + cm(c))),
        ]

    in_specs = dir_specs(0, fwd) + dir_specs(1, bwd) + [
        pl.BlockSpec((groups, R, 4), lambda b, g, c: (g, 0, 0)),
        pl.BlockSpec((groups, 1, R * P), lambda b, g, c: (g, 0, 0)),
    ]
    out_sd = jax.ShapeDtypeStruct((bsz, s, d_inner), BF16)
    return pl.pallas_call(
        functools.partial(_ssd_scan_kernel, groups=groups),
        out_shape=(out_sd, out_sd),
        grid=(bsz, G // groups, nc),
        in_specs=in_specs,
        out_specs=(pl.BlockSpec((None, L, xw), lambda b, g, c: (b, c, g)),
                   pl.BlockSpec((None, L, xw), lambda b, g, c: (b, nc - 1 - c, g))),
        scratch_shapes=[pltpu.VMEM((groups, N, R * P), F32), pltpu.VMEM((groups, N, R * P), F32)],
        compiler_params=_params("parallel", "parallel", "arbitrary"),
        name="ssd_scan",
    )(xbc, xbc, xbc, dt_t, xbc, xbc, xbc, dt_t, pr, dsk)


def _ssd_out_kernel(yf_ref, yb_ref, z_ref, g_ref, w_ref, x_ref, o_ref):
    y = (yf_ref[...].astype(F32) + yb_ref[...].astype(F32)) * _silu(z_ref[...].astype(F32))
    yn = _rms(y, g_ref[...]).astype(BF16)
    o_ref[...] = x_ref[...] + jnp.dot(yn, w_ref[...], preferred_element_type=F32)


def _ssd_out(yf, yb, zx, norm_g, out_w, layer, x, *, tm=512):
    t, d_inner = yf.shape
    d = out_w.shape[-1]
    assert t % tm == 0 and out_w.dtype == BF16
    return pl.pallas_call(
        _ssd_out_kernel,
        out_shape=jax.ShapeDtypeStruct((t, d), F32),
        grid=(t // tm,),
        in_specs=[pl.BlockSpec((tm, d_inner), lambda i: (i, 0)),
                  pl.BlockSpec((tm, d_inner), lambda i: (i, 0)),
                  pl.BlockSpec((tm, d_inner), lambda i: (i, 0)),
                  pl.BlockSpec((1, d_inner), lambda i: (0, 0)),
                  _layer_spec(out_w, layer, (d_inner, d), lambda i: (0, 0)),
                  pl.BlockSpec((tm, d), lambda i: (i, 0))],
        out_specs=pl.BlockSpec((tm, d), lambda i: (i, 0)),
        compiler_params=_params("parallel"),
        name="ssd_out",
    )(yf, yb, zx, norm_g.reshape(1, d_inner), out_w, x)


def _ssd_layer(x, norm_g, in_w, layer, conv_w, conv_b, dt_bias, a_log, d_skip, ssd_norm_g, out_w,
               bsz, s):
    d_inner = ssd_norm_g.shape[0]
    conv_ch = conv_w.shape[1]
    zx_cols = d_inner + conv_ch
    zx, dt_t = _mm(x, in_w, layer=layer, g=norm_g, n_cols=zx_cols, side_wt=in_w[layer, :, zx_cols:].T,
                   out_dtype=BF16, tn=zx_cols // 2, name="ssd_in")
    xbc = _ssd_conv(zx.reshape(bsz, s, zx_cols), conv_w, conv_b, d_inner=d_inner)
    yf, yb = _ssd_scan(xbc, dt_t, dt_bias, a_log, d_skip)
    t = bsz * s
    return _ssd_out(yf.reshape(t, d_inner), yb.reshape(t, d_inner), zx, ssd_norm_g, out_w, layer, x)


def _dil_qkv_kernel(x_ref, w_ref, g_ref, o_ref, xs_ref, *, perms, blocks_per_group):
    j = pl.program_id(1)

    @pl.when(j == 0)
    def _():
        x = _rms(x_ref[...], g_ref[...]).astype(BF16)
        for gi, perm in enumerate(perms):
            xs_ref[gi] = _permute_rows(x, *perm)

    o_ref[...] = jnp.dot(xs_ref[j // blocks_per_group], w_ref[...],
                         preferred_element_type=F32).astype(o_ref.dtype)


def _dil_qkv(x, qkv_w, layer, norm_g, perms, *, tm=1024):
    t, k = x.shape
    n_groups = len(perms)
    width = DIL_HEADS * DIL_HEAD_DIM
    assert t % tm == 0 and qkv_w.shape[-1] == 3 * n_groups * width and qkv_w.dtype == BF16
    assert all(tm % pb == 0 and pb % d == 0 for pb, d in perms)
    return pl.pallas_call(
        functools.partial(_dil_qkv_kernel, perms=perms, blocks_per_group=3),
        out_shape=jax.ShapeDtypeStruct((n_groups, t, 3 * width), BF16),
        grid=(t // tm, 3 * n_groups),
        in_specs=[pl.BlockSpec((tm, k), lambda i, j: (i, 0)),
                  _layer_spec(qkv_w, layer, (k, width), lambda i, j: (0, (j % 3) * n_groups + j // 3)),
                  pl.BlockSpec((1, k), lambda i, j: (0, 0))],
        out_specs=pl.BlockSpec((None, tm, width), lambda i, j: (j // 3, i, j % 3)),
        scratch_shapes=[pltpu.VMEM((n_groups, tm, k), BF16)],
        compiler_params=_params("parallel", "arbitrary"),
        name="dil_qkv",
    )(x, qkv_w, norm_g.reshape(1, k))


def _dil_attn_kernel(q_ref, kp_ref, kc_ref, kn_ref, vp_ref, vc_ref, vn_ref, bias_ref,
                     o_ref, lse_ref, *, n_steps, blk, per_step, split_rows):
    j = pl.program_id(1)
    hd = DIL_HEAD_DIM
    first = (j == 0).astype(jnp.int32)
    last = (j == n_steps - 1).astype(jnp.int32)
    piece = lambda n: slice(n * blk, (n + 1) * blk)
    subs = []
    for n in range(per_step):
        if split_rows:
            kv = tuple(divmod(per_step + n + t - 1, per_step) for t in range(3))
            kv = tuple((ri, piece(pi)) for ri, pi in kv)
            variant = (first if n == 0 else 0) + (2 * last if n == per_step - 1 else 0)
        else:
            kv = tuple((ri, piece(n)) for ri in range(3))
            variant = first + 2 * last
        subs.append((piece(n), kv, variant))
    k_refs, v_refs = (kp_ref, kc_ref, kn_ref), (vp_ref, vc_ref, vn_ref)
    lane = lax.broadcasted_iota(jnp.int32, (blk, 2 * hd), 1)
    scale = 1.0 / math.sqrt(hd)
    pairs = DIL_HEADS // 2
    tiles = [(si, a, slice(a * 2 * hd, (a + 1) * 2 * hd))
             for si in range(per_step) for a in range(pairs)]
    scores = []
    for si, a, sl in tiles:
        q_rows, kv, variant = subs[si]
        q = (q_ref[q_rows, sl].astype(F32) * scale).astype(BF16)
        zero = jnp.zeros_like(q)
        qbd = jnp.concatenate([jnp.where(lane < hd, q, zero), jnp.where(lane >= hd, q, zero)], axis=0)
        k = jnp.concatenate([k_refs[ri][rows, sl] for ri, rows in kv], axis=0)
        scores.append(lax.dot_general(k, qbd, NT_DIMS, preferred_element_type=F32)
                      + bias_ref[variant, a])
    probs, lse_rows = [], [[] for _ in range(per_step)]
    for (si, a, sl), s in zip(tiles, scores):
        m = jnp.max(s, axis=0, keepdims=True)
        p = jnp.exp(s - m)
        den = jnp.sum(p, axis=0, keepdims=True)
        probs.append((p * (1.0 / den)).astype(BF16))
        lse_rows[si].append(m + jnp.log(den))
    for si, rows_ in enumerate(lse_rows):
        t = jnp.concatenate(rows_, axis=0).T
        lse_ref[subs[si][0], :] = jnp.concatenate([t[0:blk], t[blk:2 * blk]], axis=1)
    for (si, a, sl), pn in zip(tiles, probs):
        q_rows, kv, _ = subs[si]
        v = jnp.concatenate([v_refs[ri][rows, sl] for ri, rows in kv], axis=0)
        r = lax.dot_general(pn, v, TN_DIMS, preferred_element_type=F32)
        o_ref[q_rows, sl] = jnp.where(lane < hd, r[0:blk], r[blk:2 * blk]).astype(o_ref.dtype)


def _dil_bias(table, window, dilation):
    blk = window // (2 * dilation)
    delta = np.arange(3 * blk)[:, None] - blk - np.arange(blk)[None, :]
    bucket = _rel_bucket_np(delta * dilation)
    band = np.abs(delta) <= blk
    key_blk = np.arange(3 * blk)[:, None] // blk
    idx = []
    for variant in range(4):
        ok = band & ~((key_blk == 0) & bool(variant & 1)) & ~((key_blk == 2) & bool(variant & 2))
        idx.append(np.where(ok, bucket, MASKED_BUCKET))
    return _bias_pair_tiles(table, np.stack(idx).astype(np.int32))


def _dil_group(qkv, gi, bias, window, dilation, bsz, s):
    width = DIL_HEADS * DIL_HEAD_DIM
    d = dilation
    n = s // d
    blk = window // (2 * d)
    assert n % blk == 0 and blk % 8 == 0
    nb = n // blk
    pairs = DIL_HEADS // 2
    split_rows = d == 1
    per_step = min(DIL_TILES_PER_STEP, nb if split_rows else d)
    n_steps, n_res = (nb // per_step, 1) if split_rows else (nb, d // per_step)
    assert (nb if split_rows else d) % per_step == 0

    def spec(which, shift):
        def index(b, jj, r):
            jn = jnp.clip(jj + shift, 0, n_steps - 1)
            return (gi, (b * n_steps + jn) * n_res + r, which)
        return pl.BlockSpec((None, per_step * blk, width), index)

    o, lse = pl.pallas_call(
        functools.partial(_dil_attn_kernel, n_steps=n_steps, blk=blk, per_step=per_step,
                          split_rows=split_rows),
        out_shape=(jax.ShapeDtypeStruct((bsz * s, width), BF16),
                   jax.ShapeDtypeStruct((bsz * s, DIL_HEADS), F32)),
        grid=(bsz, n_steps, n_res),
        in_specs=[spec(0, 0), spec(1, -1), spec(1, 0), spec(1, 1),
                  spec(2, -1), spec(2, 0), spec(2, 1),
                  pl.BlockSpec((4, pairs, 3 * blk, 2 * blk), lambda b, jj, r: (0, 0, 0, 0))],
        out_specs=(pl.BlockSpec((per_step * blk, width),
                                lambda b, jj, r: ((b * n_steps + jj) * n_res + r, 0)),
                   pl.BlockSpec((per_step * blk, DIL_HEADS),
                                lambda b, jj, r: ((b * n_steps + jj) * n_res + r, 0))),
        compiler_params=_params("parallel", "parallel", "parallel"),
        name=f"dil_attn_d{d}",
    )(qkv, qkv, qkv, qkv, qkv, qkv, qkv, bias)
    return o, lse


def _dil_out_kernel(o0_ref, o1_ref, o2_ref, l0_ref, l1_ref, l2_ref, w_ref, x_ref, o_ref, *, perms):
    width = o0_ref.shape[1]
    nh = DIL_HEADS
    os_nat, lses = [], []
    for o_ref_g, l_ref, (pb, d) in zip((o0_ref, o1_ref, o2_ref), (l0_ref, l1_ref, l2_ref), perms):
        if d == 1:
            os_nat.append(o_ref_g[...].astype(F32))
            lses.append(l_ref[...])
            continue
        hi, mid, _ = _split3(l_ref[...])
        filler = jnp.zeros((hi.shape[0], LANE_TILE - 2 * nh), BF16)
        both = _permute_rows(jnp.concatenate([o_ref_g[...], hi, mid, filler], axis=1), pb, d,
                             inverse=True, out_dtype=F32)
        os_nat.append(both[:, 0:width])
        lses.append(both[:, width:width + nh] + both[:, width + nh:width + 2 * nh])
    m = jnp.maximum(jnp.maximum(lses[0], lses[1]), lses[2])
    es = [jnp.exp(l - m) for l in lses]
    inv = 1.0 / (es[0] + es[1] + es[2])
    col = lax.broadcasted_iota(jnp.int32, (DIL_HEADS, width), 0)
    head_of_col = 2 * (col % (DIL_HEADS // 2)) + col // (DIL_HEADS // 2)
    head_of_lane = lax.broadcasted_iota(jnp.int32, (DIL_HEADS, width), 1) // DIL_HEAD_DIM
    expand = (head_of_col == head_of_lane).astype(BF16)
    y = None
    for e, o_nat in zip(es, os_nat):
        w_g = e * inv
        w_hi = w_g.astype(BF16)
        w_lo = (w_g - w_hi.astype(F32)).astype(BF16)
        wts = (jnp.dot(w_hi, expand, preferred_element_type=F32)
               + jnp.dot(w_lo, expand, preferred_element_type=F32))
        term = wts * o_nat
        y = term if y is None else y + term
    o_ref[...] = x_ref[...] + jnp.dot(y.astype(BF16), w_ref[...], preferred_element_type=F32)


def _dil_out(os_, lses, perms, out_w, layer, x, *, tm=1024):
    t, width = os_[0].shape
    d = out_w.shape[-1]
    assert t % tm == 0 and all(tm % pb == 0 for pb, _ in perms) and out_w.dtype == BF16
    row = lambda i: (i, 0)
    return pl.pallas_call(
        functools.partial(_dil_out_kernel, perms=perms),
        out_shape=jax.ShapeDtypeStruct((t, d), F32),
        grid=(t // tm,),
        in_specs=[pl.BlockSpec((tm, width), row)] * 3 + [pl.BlockSpec((tm, DIL_HEADS), row)] * 3
                 + [_layer_spec(out_w, layer, (width, d), lambda i: (0, 0)), pl.BlockSpec((tm, d), row)],
        out_specs=pl.BlockSpec((tm, d), row),
        compiler_params=_params("parallel"),
        name="dil_out",
    )(*os_, *lses, out_w, x)


def _dil_layer(x, norm_g, qkv_w, out_w, layer, table, bsz, s):
    perms = tuple(((win // (2 * dil)) * dil, dil) for win, dil in DIL_CONFIGS)
    qkv = _dil_qkv(x, qkv_w, layer, norm_g, perms)
    os_, lses = [], []
    for gi, (win, dil) in enumerate(DIL_CONFIGS):
        o, lse = _dil_group(qkv, gi, _dil_bias(table, win, dil), win, dil, bsz, s)
        os_.append(o)
        lses.append(lse)
    return _dil_out(os_, lses, perms, out_w, layer, x)


def _diff_attn_kernel(q_ref, k_ref, v_ref, bank_ref, lam_ref, g_ref, o_ref,
                      vx_ref, q2_ref, s_ref, m_ref, acc_ref, *, tq, tk, seq, lam_init):
    qi = pl.program_id(2)
    tile = DIFF_TILE
    hd = DIFF_HEAD_DIM
    nv = DIFF_V_DIM

    @pl.when(qi == 0)
    def _():
        lane = lax.broadcasted_iota(jnp.int32, (seq, nv), 1)
        vx_ref[:, 0:nv] = v_ref[...]
        vx_ref[:, nv:2 * nv] = jnp.where(lane == 0, 1.0, 0.0).astype(BF16)

    q = (q_ref[...].astype(F32) * (LOG2_E / math.sqrt(hd))).astype(BF16)
    lane = lax.broadcasted_iota(jnp.int32, q.shape, 1)
    zero = jnp.zeros_like(q)
    q2_ref[0:tq, :] = jnp.where(lane < hd, q, zero)
    q2_ref[tq:2 * tq, :] = jnp.where(lane >= hd, q, zero)
    m_ref[...] = jnp.full(m_ref.shape, -jnp.inf, F32)
    acc_ref[...] = jnp.zeros(acc_ref.shape, F32)

    n_chunks = seq // tk
    kt = tk // tile

    def scores(kc, slot):
        start = pl.multiple_of(kc * tk, tk)
        s_ref[slot] = lax.dot_general(q2_ref[...], k_ref[pl.ds(start, tk), :], NT_DIMS,
                                      preferred_element_type=F32)

    def consume(kc, slot):
        start = pl.multiple_of(kc * tk, tk)
        vx = vx_ref[pl.ds(start, tk), :]
        for i in range(2):
            rows = []
            for a in range(tq // tile):
                cols = []
                for b in range(kt):
                    dlt = (kc * kt + b) - (qi * (tq // tile) + a)
                    idx = jnp.clip(dlt, -DIFF_BANK_REACH, DIFF_BANK_REACH) + DIFF_BANK_REACH
                    r0 = i * tq + a * tile
                    cols.append(s_ref[slot, r0:r0 + tile, b * tile:(b + 1) * tile]
                                + bank_ref[i, idx])
                rows.append(cols)
            part = jnp.concatenate([functools.reduce(jnp.maximum, cols) for cols in rows], axis=0)
            row_max = jnp.broadcast_to(jnp.max(part, axis=-1, keepdims=True), part.shape)
            m_old = m_ref[i]
            m_new = jnp.maximum(m_old, row_max)
            alpha = jnp.exp2(m_old - m_new)
            p = jnp.concatenate(
                [jnp.concatenate([jnp.exp2(c - m_new[a * tile:(a + 1) * tile]).astype(BF16)
                                  for c in cols], axis=1) for a, cols in enumerate(rows)], axis=0)
            pv = jnp.dot(p, vx, preferred_element_type=F32)
            acc_ref[i] = jnp.concatenate([alpha, alpha], axis=1) * acc_ref[i] + pv
            m_ref[i] = m_new

    scores(0, 0)

    def body(j, carry):
        for slot in range(2):
            kc = 2 * j + slot
            scores(kc + 1, 1 - slot)
            consume(kc, slot)
        return carry

    lax.fori_loop(0, n_chunks // 2 - 1, body, 0)
    scores(n_chunks - 1, 1)
    consume(n_chunks - 2, 0)
    consume(n_chunks - 1, 1)

    lf = lam_ref[...]
    lam_full = (jnp.exp(jnp.sum(lf[0:1] * lf[1:2], axis=-1, keepdims=True))
                - jnp.exp(jnp.sum(lf[2:3] * lf[3:4], axis=-1, keepdims=True)) + lam_init)
    o = (acc_ref[0, :, 0:nv] / acc_ref[0, :, nv:nv + 1]
         - lam_full * (acc_ref[1, :, 0:nv] / acc_ref[1, :, nv:nv + 1]))
    o = _rms(o, g_ref[...]) * (1.0 - lam_init)
    o_ref[...] = o.astype(o_ref.dtype)


def _diff_attn(qkv, bank, lam, subln_g, lam_init, bsz, s, *, tq=1024, tk=1024):
    hw = 2 * DIFF_HEAD_DIM
    H = DIFF_HEADS
    assert s % tq == 0 and s % (2 * tk) == 0 and tq % DIFF_TILE == 0 and tk % DIFF_TILE == 0
    n_bank = bank.shape[1]
    return pl.pallas_call(
        functools.partial(_diff_attn_kernel, tq=tq, tk=tk, seq=s, lam_init=lam_init),
        out_shape=jax.ShapeDtypeStruct((bsz, s, H * DIFF_V_DIM), BF16),
        grid=(bsz, H, s // tq),
        in_specs=[pl.BlockSpec((None, tq, hw), lambda b, h, i: (b, i, h)),
                  pl.BlockSpec((None, s, hw), lambda b, h, i: (b, 0, H + h)),
                  pl.BlockSpec((None, s, DIFF_V_DIM), lambda b, h, i: (b, 0, 2 * H + h)),
                  pl.BlockSpec((2, n_bank, DIFF_TILE, DIFF_TILE), lambda b, h, i: (h, 0, 0, 0)),
                  pl.BlockSpec(lam.shape, lambda b, h, i: (0, 0)),
                  pl.BlockSpec((1, DIFF_V_DIM), lambda b, h, i: (0, 0))],
        out_specs=pl.BlockSpec((None, tq, DIFF_V_DIM), lambda b, h, i: (b, i, h)),
        scratch_shapes=[pltpu.VMEM((s, 2 * DIFF_V_DIM), BF16),
                        pltpu.VMEM((2 * tq, hw), BF16),
                        pltpu.VMEM((2, 2 * tq, tk), F32),
                        pltpu.VMEM((2, tq, DIFF_TILE), F32),
                        pltpu.VMEM((2, tq, 2 * DIFF_V_DIM), F32)],
        compiler_params=_params("parallel", "parallel", "arbitrary"),
        name="diff_attn",
    )(qkv, qkv, qkv, bank, lam.astype(F32), subln_g.reshape(1, DIFF_V_DIM).astype(F32))


def _diff_bank_buckets():
    reach = DIFF_BANK_REACH
    local = np.arange(DIFF_TILE)[None, :] - np.arange(DIFF_TILE)[:, None]
    tiles = np.stack([_rel_bucket_np(local + (t - reach) * DIFF_TILE) for t in range(2 * reach + 1)])
    assert (tiles[0] == tiles[0, 0, 0]).all() and (tiles[-1] == tiles[-1, 0, 0]).all()
    return tiles


def _diff_layer(x, norm_g, qkv_w, lam, subln_g, out_w, layer, table, layer_idx, bsz, s):
    d = x.shape[1]
    lam_init = 0.8 - 0.6 * math.exp(-0.3 * layer_idx)
    qkv = _mm(x, qkv_w, layer=layer, g=norm_g, out_dtype=BF16, tn=qkv_w.shape[-1], name="diff_qkv")
    bank = _bias_tiles(table, _diff_bank_buckets(), scale=LOG2_E)
    o = _diff_attn(qkv.reshape(bsz, s, 3 * d), bank, lam, subln_g, lam_init, bsz, s)
    return _mm(o.reshape(bsz * s, d), out_w, layer=layer, res=x, name="diff_out")


def kernel(x, rel_bias, norm_mix_g, norm_mlp_g, mlp_w1, mlp_w2, ssd_in_w, ssd_conv_w, ssd_conv_b, ssd_dt_bias, ssd_a_log, ssd_d, ssd_norm_g, ssd_out_w, dil_qkv_w, dil_out_w, diff_qkv_w, diff_lambda, diff_subln_g, diff_out_w, final_norm_g):
    bsz, s, d = x.shape
    depth = norm_mix_g.shape[0]
    (mlp_w1, mlp_w2, ssd_in_w, ssd_out_w, dil_qkv_w, dil_out_w, diff_qkv_w, diff_out_w) = (
        w.astype(BF16) for w in (mlp_w1, mlp_w2, ssd_in_w, ssd_out_w, dil_qkv_w, dil_out_w,
                                 diff_qkv_w, diff_out_w))
    h = x.reshape(bsz * s, d)
    for i in range(depth):
        kind, j = i % 3, i // 3
        if kind == 0:
            h = _ssd_layer(h, norm_mix_g[i], ssd_in_w, j, ssd_conv_w[j], ssd_conv_b[j],
                           ssd_dt_bias[j], ssd_a_log[j], ssd_d[j], ssd_norm_g[j], ssd_out_w, bsz, s)
        elif kind == 1:
            h = _dil_layer(h, norm_mix_g[i], dil_qkv_w, dil_out_w, j, rel_bias, bsz, s)
        else:
            h = _diff_layer(h, norm_mix_g[i], diff_qkv_w, diff_lambda[j], diff_subln_g[j],
                            diff_out_w, j, rel_bias, i, bsz, s)
        h = _mlp(h, norm_mlp_g[i], mlp_w1, mlp_w2, i,
                 final_norm_g if i == depth - 1 else None)
    return h.reshape(bsz, s, d)
```

```python
import functools
import math

import numpy as np
import jax
import jax.numpy as jnp
from jax import lax
from jax.experimental import pallas as pl
from jax.experimental.pallas import tpu as pltpu

F32 = jnp.float32
BF16 = jnp.bfloat16

RMS_EPS = 1e-6
NEG_INF = -1e30
LOG2_E = math.log2(math.e)

LANE_TILE = 128
V7X_VMEM_BYTES = 64 * 1024 * 1024
VMEM_LIMIT_BYTES = V7X_VMEM_BYTES - 8 * 1024 * 1024

SSD_HEAD_DIM = 64
SSD_GROUPS = 8
SSD_HEADS_PER_GROUP = 4
SSD_D_STATE = 128
SSD_CONV_WIDTH = 5
SSD_CHUNK = 128
SSD_GROUP_WIDTH = SSD_HEADS_PER_GROUP * SSD_HEAD_DIM

DIL_CONFIGS = ((128, 1), (512, 4), (2048, 16))
DIL_HEADS = 16
DIL_HEAD_DIM = 64
DIL_TILES_PER_STEP = 8

DIFF_HEADS = 8
DIFF_HEAD_DIM = 64
DIFF_V_DIM = 128
DIFF_TILE = 128
DIFF_BANK_REACH = 6

REL_BUCKETS = 32
REL_MAX_DIST = 1024
REL_BIAS_HEADS = 16
MASKED_BUCKET = REL_BUCKETS

NT_DIMS = (((1,), (1,)), ((), ()))
TN_DIMS = (((0,), (0,)), ((), ()))


def _params(*semantics):
    return pltpu.CompilerParams(dimension_semantics=semantics,
                                vmem_limit_bytes=VMEM_LIMIT_BYTES)


def _rms(x, g):
    ms = jnp.mean(x * x, axis=-1, keepdims=True)
    return x * lax.rsqrt(ms + RMS_EPS) * g


def _silu(x):
    return x * (1.0 / (1.0 + jnp.exp(-x)))


def _softplus(x):
    return jnp.maximum(x, 0.0) + jnp.log1p(jnp.exp(-jnp.abs(x)))


def _residue_major_perm(pb, d, inverse=False):
    i = lax.broadcasted_iota(jnp.int32, (pb, pb), 0)
    j = lax.broadcasted_iota(jnp.int32, (pb, pb), 1)
    per = pb // d
    if inverse:
        i, j = j, i
    return (j == (i % per) * d + i // per).astype(BF16)


def _permute_rows(x, pb, d, inverse=False, out_dtype=BF16):
    if d == 1:
        return x.astype(out_dtype)
    pm = _residue_major_perm(pb, d, inverse)
    blocks = [jnp.dot(pm, x[r0:r0 + pb], preferred_element_type=F32).astype(out_dtype)
              for r0 in range(0, x.shape[0], pb)]
    return blocks[0] if len(blocks) == 1 else jnp.concatenate(blocks, axis=0)


def _mm_kernel(*refs, norm, res, side):
    refs = list(refs)
    x_ref, w_ref = refs[0], refs[1]
    pos = 2
    g_ref = r_ref = ws_ref = os_ref = None
    if norm:
        g_ref = refs[pos]
        pos += 1
    if res:
        r_ref = refs[pos]
        pos += 1
    if side:
        ws_ref = refs[pos]
        pos += 1
    o_ref = refs[pos]
    pos += 1
    if side:
        os_ref = refs[pos]
        pos += 1
    xs_ref = refs[pos]

    @pl.when(pl.program_id(1) == 0)
    def _():
        x = x_ref[...].astype(F32)
        if norm:
            x = _rms(x, g_ref[...])
        x = x.astype(BF16)
        xs_ref[...] = x
        if side:
            os_ref[...] = lax.dot_general(ws_ref[...], x, NT_DIMS, preferred_element_type=F32)

    acc = jnp.dot(xs_ref[...], w_ref[...], preferred_element_type=F32)
    if res:
        acc = r_ref[...] + acc
    o_ref[...] = acc.astype(o_ref.dtype)


def _layer_spec(w, layer, block, index):
    if w.ndim == 2:
        return pl.BlockSpec(block, index)
    return pl.BlockSpec((None,) + block, lambda *g: (layer,) + index(*g))


def _mm(x, w, *, layer=0, g=None, res=None, n_cols=None, side_wt=None, out_dtype=F32,
        tm=1024, tn=1024, name="mm"):
    t, k = x.shape
    n = w.shape[-1] if n_cols is None else n_cols
    tn = min(tn, n)
    tm = min(tm, t)
    assert t % tm == 0 and n % tn == 0 and w.shape[-2] == k and w.dtype == BF16
    in_specs = [pl.BlockSpec((tm, k), lambda i, j: (i, 0)),
                _layer_spec(w, layer, (k, tn), lambda i, j: (0, j))]
    args = [x, w]
    if g is not None:
        in_specs.append(pl.BlockSpec((1, k), lambda i, j: (0, 0)))
        args.append(g.reshape(1, k))
    if res is not None:
        in_specs.append(pl.BlockSpec((tm, tn), lambda i, j: (i, j)))
        args.append(res)
    out_shape = jax.ShapeDtypeStruct((t, n), out_dtype)
    out_specs = pl.BlockSpec((tm, tn), lambda i, j: (i, j))
    if side_wt is not None:
        m = side_wt.shape[0]
        in_specs.append(pl.BlockSpec((m, k), lambda i, j: (0, 0)))
        args.append(side_wt)
        out_shape = (out_shape, jax.ShapeDtypeStruct((m, t), F32))
        out_specs = (out_specs, pl.BlockSpec((m, tm), lambda i, j: (0, i)))
    return pl.pallas_call(
        functools.partial(_mm_kernel, norm=g is not None, res=res is not None,
                          side=side_wt is not None),
        out_shape=out_shape,
        grid=(t // tm, n // tn),
        in_specs=in_specs,
        out_specs=out_specs,
        scratch_shapes=[pltpu.VMEM((tm, k), BF16)],
        compiler_params=_params("parallel", "arbitrary"),
        name=name,
    )(*args)


def _mlp_kernel(*refs, final):
    if final:
        x_ref, g_ref, w1_ref, w2_ref, fg_ref, o_ref, xn_ref, acc_ref = refs
    else:
        x_ref, g_ref, w1_ref, w2_ref, o_ref, xn_ref, acc_ref = refs
        fg_ref = None
    f = pl.program_id(1)

    @pl.when(f == 0)
    def _():
        x = x_ref[...]
        xn_ref[...] = _rms(x, g_ref[...]).astype(BF16)
        acc_ref[...] = x

    h = jnp.dot(xn_ref[...], w1_ref[...], preferred_element_type=F32)
    h = jnp.square(jnp.maximum(h, 0.0)).astype(BF16)
    acc_ref[...] += jnp.dot(h, w2_ref[...], preferred_element_type=F32)

    @pl.when(f == pl.num_programs(1) - 1)
    def _():
        y = acc_ref[...]
        if final:
            y = _rms(y, fg_ref[...])
        o_ref[...] = y


def _mlp(x, g, w1, w2, layer, final_g=None, *, tm=1024, tf=2048):
    t, d = x.shape
    ff = w1.shape[-1]
    assert t % tm == 0 and ff % tf == 0 and w1.dtype == BF16 and w2.dtype == BF16
    final = final_g is not None
    in_specs = [pl.BlockSpec((tm, d), lambda i, f: (i, 0)),
                pl.BlockSpec((1, d), lambda i, f: (0, 0)),
                _layer_spec(w1, layer, (d, tf), lambda i, f: (0, f)),
                _layer_spec(w2, layer, (tf, d), lambda i, f: (f, 0))]
    args = [x, g.reshape(1, d), w1, w2]
    if final:
        in_specs.append(pl.BlockSpec((1, d), lambda i, f: (0, 0)))
        args.append(final_g.reshape(1, d))
    return pl.pallas_call(
        functools.partial(_mlp_kernel, final=final),
        out_shape=jax.ShapeDtypeStruct((t, d), F32),
        grid=(t // tm, ff // tf),
        in_specs=in_specs,
        out_specs=pl.BlockSpec((tm, d), lambda i, f: (i, 0)),
        scratch_shapes=[pltpu.VMEM((tm, d), BF16), pltpu.VMEM((tm, d), F32)],
        compiler_params=_params("parallel", "arbitrary"),
        name="mlp",
    )(*args)


def _rel_bucket_np(rel):
    half = REL_BUCKETS // 2
    max_exact = half // 2
    rel = np.asarray(rel, np.int32)
    ret = np.where(rel > 0, half, 0)
    n = np.abs(rel)
    nf = np.maximum(n, 1).astype(np.float32)
    scale = np.float32(math.log(REL_MAX_DIST / max_exact))
    large = max_exact + (np.log(nf / np.float32(max_exact)) / scale
                         * np.float32(half - max_exact)).astype(np.int32)
    large = np.minimum(large, half - 1)
    return (ret + np.where(n < max_exact, n, large)).astype(np.int32)


def _bias_tiles_kernel(tbl_ref, idx_ref, o_ref, *, scale):
    h = pl.program_id(0)
    idx = idx_ref[...]
    acc = jnp.zeros(idx.shape, F32)
    for b in range(REL_BUCKETS):
        acc = jnp.where(idx == b, tbl_ref[b, h] * scale, acc)
    o_ref[...] = jnp.where(idx == MASKED_BUCKET, NEG_INF, acc)


def _bias_tiles(table, bucket_idx, scale=1.0):
    n, r, c = bucket_idx.shape
    return pl.pallas_call(
        functools.partial(_bias_tiles_kernel, scale=scale),
        out_shape=jax.ShapeDtypeStruct((REL_BIAS_HEADS, n, r, c), F32),
        grid=(REL_BIAS_HEADS,),
        in_specs=[pl.BlockSpec(memory_space=pltpu.SMEM),
                  pl.BlockSpec((n, r, c), lambda h: (0, 0, 0))],
        out_specs=pl.BlockSpec((None, n, r, c), lambda h: (h, 0, 0, 0)),
        compiler_params=_params("parallel"),
        name="bias_tiles",
    )(table.astype(F32), jnp.asarray(bucket_idx))


def _bias_pair_tiles_kernel(tbl_ref, idx_ref, o_ref):
    a = pl.program_id(0)
    idx = idx_ref[...]
    half = idx.shape[-1] // 2
    first_head = lax.broadcasted_iota(jnp.int32, idx.shape, idx.ndim - 1) < half
    acc = jnp.zeros(idx.shape, F32)
    for b in range(REL_BUCKETS):
        val = jnp.where(first_head, tbl_ref[b, 2 * a], tbl_ref[b, 2 * a + 1])
        acc = jnp.where(idx == b, val, acc)
    o_ref[...] = jnp.where(idx == MASKED_BUCKET, NEG_INF, acc)


def _bias_pair_tiles(table, bucket_idx):
    n, r, c = bucket_idx.shape
    pairs = REL_BIAS_HEADS // 2
    idx2 = np.concatenate([bucket_idx, bucket_idx], axis=-1)
    return pl.pallas_call(
        _bias_pair_tiles_kernel,
        out_shape=jax.ShapeDtypeStruct((n, pairs, r, 2 * c), F32),
        grid=(pairs,),
        in_specs=[pl.BlockSpec(memory_space=pltpu.SMEM),
                  pl.BlockSpec((n, r, 2 * c), lambda a: (0, 0, 0))],
        out_specs=pl.BlockSpec((n, None, r, 2 * c), lambda a: (0, a, 0, 0)),
        compiler_params=_params("parallel"),
        name="bias_pair_tiles",
    )(table.astype(F32), jnp.asarray(idx2))


def _conv_kernel(x_ref, w_ref, b_ref, o_ref, xp_ref, *, rows):
    s, tc = x_ref.shape
    pad = 16
    half = SSD_CONV_WIDTH // 2
    xp_ref[0:pad, :] = jnp.zeros((pad, tc), BF16)
    xp_ref[s + pad:s + 2 * pad, :] = jnp.zeros((pad, tc), BF16)
    xp_ref[pad:s + pad, :] = x_ref[...]
    win = rows + 2 * pad
    side_taps = [k for k in range(SSD_CONV_WIDTH) if k != half]
    out_row = lax.broadcasted_iota(jnp.int32, (rows, win), 0)
    win_row = lax.broadcasted_iota(jnp.int32, (rows, win), 1)
    sel = jnp.concatenate([(win_row == out_row + (pad + k - half)).astype(BF16) for k in side_taps],
                          axis=0)
    w = w_ref[...]
    b = b_ref[...]
    for ch in range(s // rows):
        r0 = ch * rows
        window = xp_ref[r0:r0 + win, :]
        shifted = jnp.dot(sel, window, preferred_element_type=F32)
        acc = b + w[half:half + 1, :] * window[pad:pad + rows].astype(F32)
        for t, k in enumerate(side_taps):
            acc = acc + w[k:k + 1, :] * shifted[t * rows:(t + 1) * rows]
        o_ref[r0:r0 + rows, :] = _silu(acc).astype(o_ref.dtype)


def _ssd_conv(zx, conv_w, conv_b, *, d_inner, tc=256, rows=128):
    bsz, s, _ = zx.shape
    c = conv_w.shape[1]
    off = d_inner // tc
    assert zx.dtype == BF16 and s % rows == 0 and rows % 16 == 0
    return pl.pallas_call(
        functools.partial(_conv_kernel, rows=rows),
        out_shape=jax.ShapeDtypeStruct((bsz, s, c), BF16),
        grid=(bsz, c // tc),
        in_specs=[pl.BlockSpec((None, s, tc), lambda b, j: (b, 0, off + j)),
                  pl.BlockSpec((SSD_CONV_WIDTH, tc), lambda b, j: (0, j)),
                  pl.BlockSpec((1, tc), lambda b, j: (0, j))],
        out_specs=pl.BlockSpec((None, s, tc), lambda b, j: (b, 0, j)),
        scratch_shapes=[pltpu.VMEM((s + 32, tc), BF16)],
        compiler_params=_params("parallel", "parallel"),
        name="ssd_conv",
    )(zx, conv_w, conv_b.reshape(1, c))


def _expand_heads(v, head_of_lane):
    r_heads = v.shape[1]
    out = v[:, r_heads - 1:r_heads]
    for r in range(r_heads - 2, -1, -1):
        out = jnp.where(head_of_lane == r, v[:, r:r + 1], out)
    return out


def _split3(x):
    hi = x.astype(BF16)
    r1 = x - hi.astype(F32)
    mid = r1.astype(BF16)
    lo = (r1 - mid.astype(F32)).astype(BF16)
    return hi, mid, lo


def _ssd_scan_kernel(xf_ref, bf_ref, cf_ref, drf_ref, xb_ref, bb_ref, cb_ref, drb_ref,
                     pr_ref, dsk_ref, yf_ref, yb_ref, sf_ref, sb_ref, *, groups):
    @pl.when(pl.program_id(2) == 0)
    def _():
        sf_ref[...] = jnp.zeros(sf_ref.shape, F32)
        sb_ref[...] = jnp.zeros(sb_ref.shape, F32)

    L, R, P, N = SSD_CHUNK, SSD_HEADS_PER_GROUP, SSD_HEAD_DIM, SSD_D_STATE
    W = SSD_GROUP_WIDTH
    row = lax.broadcasted_iota(jnp.int32, (L, L), 0)
    col = lax.broadcasted_iota(jnp.int32, (L, L), 1)
    head_of_lane = lax.broadcasted_iota(jnp.int32, (1, W), 1) // P
    head_of_lane_x = lax.broadcasted_iota(jnp.int32, (L, W), 1) // P
    head_masks = [(head_of_lane_x == r).astype(BF16) for r in range(R)]
    tris = {False: col <= row, True: col >= row}
    tri_b = {rev: t.astype(BF16) for rev, t in tris.items()}
    tri_tb = {False: (row <= col).astype(BF16), True: (row >= col).astype(BF16)}

    items = []
    for g in range(groups):
        pr = pr_ref[g]
        xs, ns = slice(g * W, (g + 1) * W), slice(g * N, (g + 1) * N)
        items.append(dict(g=g, xs=xs, ns=ns, x=xf_ref, b=bf_ref, c=cf_ref, dr=drf_ref, s=sf_ref,
                          y=yf_ref, rev=False, bias_r=pr[:, 0:1], alog_r=pr[:, 1:2]))
        items.append(dict(g=g, xs=xs, ns=ns, x=xb_ref, b=bb_ref, c=cb_ref, dr=drb_ref, s=sb_ref,
                          y=yb_ref, rev=True, bias_r=pr[:, 2:3], alog_r=pr[:, 3:4]))

    for d in items:
        rev = d["rev"]
        d["dtr"] = _softplus(d["dr"][d["g"] * R:(d["g"] + 1) * R, :] + d["bias_r"])
        adt_r = d["dtr"] * (-LOG2_E * jnp.exp(d["alog_r"]))
        pieces = _split3(adt_r)
        d["acs_c"] = sum(lax.dot_general(tri_b[rev], piece, NT_DIMS, preferred_element_type=F32)
                         for piece in pieces)
        acs3 = jnp.dot(jnp.concatenate(pieces, axis=0), tri_tb[rev], preferred_element_type=F32)
        d["acs_r"] = acs3[0:R] + acs3[R:2 * R] + acs3[2 * R:3 * R]
        bm = d["b"][:, d["ns"]]
        cm = d["c"][:, d["ns"]]
        d["cb"] = lax.dot_general(cm, bm, NT_DIMS, preferred_element_type=F32)
        d["bt"] = bm.astype(F32).T

    for d in items:
        acs_c, acs_r = d["acs_c"], d["acs_r"]
        last_r = acs_r[:, 0:1] if d["rev"] else acs_r[:, L - 1:L]
        acs_k = acs_r - jnp.log2(d["dtr"])
        key_scale = jnp.exp2(last_r - acs_k)
        ws, bts, growth = [], [], []
        for r in range(R):
            acs_b = jnp.broadcast_to(acs_c[:, r:r + 1], (L, L))
            seg = acs_b - acs_k[r:r + 1, :]
            lmat = jnp.exp2(jnp.where(tris[d["rev"]], seg, -jnp.inf))
            ws.append((d["cb"] * lmat).astype(BF16))
            bts.append((d["bt"] * key_scale[r:r + 1, :]).astype(BF16))
            growth.append(jnp.exp2(acs_b))
        d["wb"] = jnp.concatenate([jnp.concatenate(ws, axis=1), jnp.concatenate(bts, axis=1)], axis=0)
        lane_lo = lax.broadcasted_iota(jnp.int32, (L, L), 1) < P
        d["growth"] = jnp.concatenate(
            [jnp.where(lane_lo, growth[r], growth[r + 1]) for r in range(0, R, 2)], axis=1)

    for d in items:
        xb16 = d["x"][:, d["xs"]]
        xstack = jnp.concatenate([xb16 * mask for mask in head_masks], axis=0)
        both = jnp.dot(d["wb"], xstack, preferred_element_type=F32)
        d["yd"], d["upd"] = both[0:L], both[L:L + N]
        d["state"] = d["s"][d["g"]]
        d["y_off"] = jnp.dot(d["c"][:, d["ns"]], d["state"].astype(BF16), preferred_element_type=F32)

    for d in items:
        acs_c = d["acs_c"]
        last_c = acs_c[0:1, :] if d["rev"] else acs_c[L - 1:L, :]
        y = d["yd"] + d["y_off"] * d["growth"]
        d["s"][d["g"]] = d["state"] * _expand_heads(jnp.exp2(last_c), head_of_lane) + d["upd"]
        if not d["rev"]:
            y = y + dsk_ref[d["g"]] * d["x"][:, d["xs"]].astype(F32)
        d["y"][:, d["xs"]] = y.astype(d["y"].dtype)


def _ssd_scan(xbc, dt_t, dt_bias, a_log, d_skip, *, groups=8):
    bsz, s, _ = xbc.shape
    G, R, P, N, L = SSD_GROUPS, SSD_HEADS_PER_GROUP, SSD_HEAD_DIM, SSD_D_STATE, SSD_CHUNK
    assert G % groups == 0 and L == 2 * P
    d_inner = G * R * P
    nc = s // L
    xw = groups * R * P
    nw = groups * N
    b_off = d_inner // nw
    c_off = b_off + G // groups
    gblocks = G // groups
    pr = jnp.stack([dt_bias[0].reshape(G, R), a_log[0].reshape(G, R),
                    dt_bias[1].reshape(G, R), a_log[1].reshape(G, R)], axis=2).astype(F32)
    dsk = jnp.repeat(d_skip.astype(F32).reshape(G, 1, R), P, axis=2)

    def fwd(c):
        return c

    def bwd(c):
        return nc - 1 - c

    def dir_specs(d, cm):
        return [
            pl.BlockSpec((None, L, xw), lambda b, g, c: (b, cm(c), g)),
            pl.BlockSpec((None, L, nw), lambda b, g, c: (b, cm(c), b_off + g)),
            pl.BlockSpec((None, L, nw), lambda b, g, c: (b, cm(c), c_off + g)),
            pl.BlockSpec((groups * R, L), lambda b, g, c: (d * gblocks + g, b * nc + cm(c))),
        ]

    in_specs = dir_specs(0, fwd) + dir_specs(1, bwd) + [
        pl.BlockSpec((groups, R, 4), lambda b, g, c: (g, 0, 0)),
        pl.BlockSpec((groups, 1, R * P), lambda b, g, c: (g, 0, 0)),
    ]
    out_sd = jax.ShapeDtypeStruct((bsz, s, d_inner), BF16)
    return pl.pallas_call(
        functools.partial(_ssd_scan_kernel, groups=groups),
        out_shape=(out_sd, out_sd),
        grid=(bsz, G // groups, nc),
        in_specs=in_specs,
        out_specs=(pl.BlockSpec((None, L, xw), lambda b, g, c: (b, c, g)),
                   pl.BlockSpec((None, L, xw), lambda b, g, c: (b, nc - 1 - c, g))),
        scratch_shapes=[pltpu.VMEM((groups, N, R * P), F32), pltpu.VMEM((groups, N, R * P), F32)],
        compiler_params=_params("parallel", "parallel", "arbitrary"),
        name="ssd_scan",
    )(xbc, xbc, xbc, dt_t, xbc, xbc, xbc, dt_t, pr, dsk)


def _ssd_out_kernel(yf_ref, yb_ref, z_ref, g_ref, w_ref, x_ref, o_ref):
    y = (yf_ref[...].astype(F32) + yb_ref[...].astype(F32)) * _silu(z_ref[...].astype(F32))
    yn = _rms(y, g_ref[...]).astype(BF16)
    o_ref[...] = x_ref[...] + jnp.dot(yn, w_ref[...], preferred_element_type=F32)


def _ssd_out(yf, yb, zx, norm_g, out_w, layer, x, *, tm=512):
    t, d_inner = yf.shape
    d = out_w.shape[-1]
    assert t % tm == 0 and out_w.dtype == BF16
    return pl.pallas_call(
        _ssd_out_kernel,
        out_shape=jax.ShapeDtypeStruct((t, d), F32),
        grid=(t // tm,),
        in_specs=[pl.BlockSpec((tm, d_inner), lambda i: (i, 0)),
                  pl.BlockSpec((tm, d_inner), lambda i: (i, 0)),
                  pl.BlockSpec((tm, d_inner), lambda i: (i, 0)),
                  pl.BlockSpec((1, d_inner), lambda i: (0, 0)),
                  _layer_spec(out_w, layer, (d_inner, d), lambda i: (0, 0)),
                  pl.BlockSpec((tm, d), lambda i: (i, 0))],
        out_specs=pl.BlockSpec((tm, d), lambda i: (i, 0)),
        compiler_params=_params("parallel"),
        name="ssd_out",
    )(yf, yb, zx, norm_g.reshape(1, d_inner), out_w, x)


def _ssd_layer(x, norm_g, in_w, layer, conv_w, conv_b, dt_bias, a_log, d_skip, ssd_norm_g, out_w,
               bsz, s):
    d_inner = ssd_norm_g.shape[0]
    conv_ch = conv_w.shape[1]
    zx_cols = d_inner + conv_ch
    zx, dt_t = _mm(x, in_w, layer=layer, g=norm_g, n_cols=zx_cols, side_wt=in_w[layer, :, zx_cols:].T,
                   out_dtype=BF16, tn=zx_cols // 2, name="ssd_in")
    xbc = _ssd_conv(zx.reshape(bsz, s, zx_cols), conv_w, conv_b, d_inner=d_inner)
    yf, yb = _ssd_scan(xbc, dt_t, dt_bias, a_log, d_skip)
    t = bsz * s
    return _ssd_out(yf.reshape(t, d_inner), yb.reshape(t, d_inner), zx, ssd_norm_g, out_w, layer, x)


def _dil_qkv_kernel(x_ref, w_ref, g_ref, o_ref, xs_ref, *, perms, blocks_per_group):
    j = pl.program_id(1)

    @pl.when(j == 0)
    def _():
        x = _rms(x_ref[...], g_ref[...]).astype(BF16)
        for gi, perm in enumerate(perms):
            xs_ref[gi] = _permute_rows(x, *perm)

    o_ref[...] = jnp.dot(xs_ref[j // blocks_per_group], w_ref[...],
                         preferred_element_type=F32).astype(o_ref.dtype)


def _dil_qkv(x, qkv_w, layer, norm_g, perms, *, tm=1024):
    t, k = x.shape
    n_groups = len(perms)
    width = DIL_HEADS * DIL_HEAD_DIM
    assert t % tm == 0 and qkv_w.shape[-1] == 3 * n_groups * width and qkv_w.dtype == BF16
    assert all(tm % pb == 0 and pb % d == 0 for pb, d in perms)
    return pl.pallas_call(
        functools.partial(_dil_qkv_kernel, perms=perms, blocks_per_group=3),
        out_shape=jax.ShapeDtypeStruct((n_groups, t, 3 * width), BF16),
        grid=(t // tm, 3 * n_groups),
        in_specs=[pl.BlockSpec((tm, k), lambda i, j: (i, 0)),
                  _layer_spec(qkv_w, layer, (k, width), lambda i, j: (0, (j % 3) * n_groups + j // 3)),
                  pl.BlockSpec((1, k), lambda i, j: (0, 0))],
        out_specs=pl.BlockSpec((None, tm, width), lambda i, j: (j // 3, i, j % 3)),
        scratch_shapes=[pltpu.VMEM((n_groups, tm, k), BF16)],
        compiler_params=_params("parallel", "arbitrary"),
        name="dil_qkv",
    )(x, qkv_w, norm_g.reshape(1, k))


def _dil_attn_kernel(q_ref, kp_ref, kc_ref, kn_ref, vp_ref, vc_ref, vn_ref, bias_ref,
                     o_ref, lse_ref, *, n_steps, blk, per_step, split_rows):
    j = pl.program_id(1)
    hd = DIL_HEAD_DIM
    first = (j == 0).astype(jnp.int32)
    last = (j == n_steps - 1).astype(jnp.int32)
    piece = lambda n: slice(n * blk, (n + 1) * blk)
    subs = []
    for n in range(per_step):
        if split_rows:
            kv = tuple(divmod(per_step + n + t - 1, per_step) for t in range(3))
            kv = tuple((ri, piece(pi)) for ri, pi in kv)
            variant = (first if n == 0 else 0) + (2 * last if n == per_step - 1 else 0)
        else:
            kv = tuple((ri, piece(n)) for ri in range(3))
            variant = first + 2 * last
        subs.append((piece(n), kv, variant))
    k_refs, v_refs = (kp_ref, kc_ref, kn_ref), (vp_ref, vc_ref, vn_ref)
    lane = lax.broadcasted_iota(jnp.int32, (blk, 2 * hd), 1)
    scale = 1.0 / math.sqrt(hd)
    pairs = DIL_HEADS // 2
    tiles = [(si, a, slice(a * 2 * hd, (a + 1) * 2 * hd))
             for si in range(per_step) for a in range(pairs)]
    scores = []
    for si, a, sl in tiles:
        q_rows, kv, variant = subs[si]
        q = (q_ref[q_rows, sl].astype(F32) * scale).astype(BF16)
        zero = jnp.zeros_like(q)
        qbd = jnp.concatenate([jnp.where(lane < hd, q, zero), jnp.where(lane >= hd, q, zero)], axis=0)
        k = jnp.concatenate([k_refs[ri][rows, sl] for ri, rows in kv], axis=0)
        scores.append(lax.dot_general(k, qbd, NT_DIMS, preferred_element_type=F32)
                      + bias_ref[variant, a])
    probs, lse_rows = [], [[] for _ in range(per_step)]
    for (si, a, sl), s in zip(tiles, scores):
        m = jnp.max(s, axis=0, keepdims=True)
        p = jnp.exp(s - m)
        den = jnp.sum(p, axis=0, keepdims=True)
        probs.append((p * (1.0 / den)).astype(BF16))
        lse_rows[si].append(m + jnp.log(den))
    for si, rows_ in enumerate(lse_rows):
        t = jnp.concatenate(rows_, axis=0).T
        lse_ref[subs[si][0], :] = jnp.concatenate([t[0:blk], t[blk:2 * blk]], axis=1)
    for (si, a, sl), pn in zip(tiles, probs):
        q_rows, kv, _ = subs[si]
        v = jnp.concatenate([v_refs[ri][rows, sl] for ri, rows in kv], axis=0)
        r = lax.dot_general(pn, v, TN_DIMS, preferred_element_type=F32)
        o_ref[q_rows, sl] = jnp.where(lane < hd, r[0:blk], r[blk:2 * blk]).astype(o_ref.dtype)


def _dil_bias(table, window, dilation):
    blk = window // (2 * dilation)
    delta = np.arange(3 * blk)[:, None] - blk - np.arange(blk)[None, :]
    bucket = _rel_bucket_np(delta * dilation)
    band = np.abs(delta) <= blk
    key_blk = np.arange(3 * blk)[:, None] // blk
    idx = []
    for variant in range(4):
        ok = band & ~((key_blk == 0) & bool(variant & 1)) & ~((key_blk == 2) & bool(variant & 2))
        idx.append(np.where(ok, bucket, MASKED_BUCKET))
    return _bias_pair_tiles(table, np.stack(idx).astype(np.int32))


def _dil_group(qkv, gi, bias, window, dilation, bsz, s):
    width = DIL_HEADS * DIL_HEAD_DIM
    d = dilation
    n = s // d
    blk = window // (2 * d)
    assert n % blk == 0 and blk % 8 == 0
    nb = n // blk
    pairs = DIL_HEADS // 2
    split_rows = d == 1
    per_step = min(DIL_TILES_PER_STEP, nb if split_rows else d)
    n_steps, n_res = (nb // per_step, 1) if split_rows else (nb, d // per_step)
    assert (nb if split_rows else d) % per_step == 0

    def spec(which, shift):
        def index(b, jj, r):
            jn = jnp.clip(jj + shift, 0, n_steps - 1)
            return (gi, (b * n_steps + jn) * n_res + r, which)
        return pl.BlockSpec((None, per_step * blk, width), index)

    o, lse = pl.pallas_call(
        functools.partial(_dil_attn_kernel, n_steps=n_steps, blk=blk, per_step=per_step,
                          split_rows=split_rows),
        out_shape=(jax.ShapeDtypeStruct((bsz * s, width), BF16),
                   jax.ShapeDtypeStruct((bsz * s, DIL_HEADS), F32)),
        grid=(bsz, n_steps, n_res),
        in_specs=[spec(0, 0), spec(1, -1), spec(1, 0), spec(1, 1),
                  spec(2, -1), spec(2, 0), spec(2, 1),
                  pl.BlockSpec((4, pairs, 3 * blk, 2 * blk), lambda b, jj, r: (0, 0, 0, 0))],
        out_specs=(pl.BlockSpec((per_step * blk, width),
                                lambda b, jj, r: ((b * n_steps + jj) * n_res + r, 0)),
                   pl.BlockSpec((per_step * blk, DIL_HEADS),
                                lambda b, jj, r: ((b * n_steps + jj) * n_res + r, 0))),
        compiler_params=_params("parallel", "parallel", "parallel"),
        name=f"dil_attn_d{d}",
    )(qkv, qkv, qkv, qkv, qkv, qkv, qkv, bias)
    return o, lse


def _dil_out_kernel(o0_ref, o1_ref, o2_ref, l0_ref, l1_ref, l2_ref, w_ref, x_ref, o_ref, *, perms):
    width = o0_ref.shape[1]
    nh = DIL_HEADS
    os_nat, lses = [], []
    for o_ref_g, l_ref, (pb, d) in zip((o0_ref, o1_ref, o2_ref), (l0_ref, l1_ref, l2_ref), perms):
        if d == 1:
            os_nat.append(o_ref_g[...].astype(F32))
            lses.append(l_ref[...])
            continue
        hi, mid, _ = _split3(l_ref[...])
        filler = jnp.zeros((hi.shape[0], LANE_TILE - 2 * nh), BF16)
        both = _permute_rows(jnp.concatenate([o_ref_g[...], hi, mid, filler], axis=1), pb, d,
                             inverse=True, out_dtype=F32)
        os_nat.append(both[:, 0:width])
        lses.append(both[:, width:width + nh] + both[:, width + nh:width + 2 * nh])
    m = jnp.maximum(jnp.maximum(lses[0], lses[1]), lses[2])
    es = [jnp.exp(l - m) for l in lses]
    inv = 1.0 / (es[0] + es[1] + es[2])
    col = lax.broadcasted_iota(jnp.int32, (DIL_HEADS, width), 0)
    head_of_col = 2 * (col % (DIL_HEADS // 2)) + col // (DIL_HEADS // 2)
    head_of_lane = lax.broadcasted_iota(jnp.int32, (DIL_HEADS, width), 1) // DIL_HEAD_DIM
    expand = (head_of_col == head_of_lane).astype(BF16)
    y = None
    for e, o_nat in zip(es, os_nat):
        w_g = e * inv
        w_hi = w_g.astype(BF16)
        w_lo = (w_g - w_hi.astype(F32)).astype(BF16)
        wts = (jnp.dot(w_hi, expand, preferred_element_type=F32)
               + jnp.dot(w_lo, expand, preferred_element_type=F32))
        term = wts * o_nat
        y = term if y is None else y + term
    o_ref[...] = x_ref[...] + jnp.dot(y.astype(BF16), w_ref[...], preferred_element_type=F32)


def _dil_out(os_, lses, perms, out_w, layer, x, *, tm=1024):
    t, width = os_[0].shape
    d = out_w.shape[-1]
    assert t % tm == 0 and all(tm % pb == 0 for pb, _ in perms) and out_w.dtype == BF16
    row = lambda i: (i, 0)
    return pl.pallas_call(
        functools.partial(_dil_out_kernel, perms=perms),
        out_shape=jax.ShapeDtypeStruct((t, d), F32),
        grid=(t // tm,),
        in_specs=[pl.BlockSpec((tm, width), row)] * 3 + [pl.BlockSpec((tm, DIL_HEADS), row)] * 3
                 + [_layer_spec(out_w, layer, (width, d), lambda i: (0, 0)), pl.BlockSpec((tm, d), row)],
        out_specs=pl.BlockSpec((tm, d), row),
        compiler_params=_params("parallel"),
        name="dil_out",
    )(*os_, *lses, out_w, x)


def _dil_layer(x, norm_g, qkv_w, out_w, layer, table, bsz, s):
    perms = tuple(((win // (2 * dil)) * dil, dil) for win, dil in DIL_CONFIGS)
    qkv = _dil_qkv(x, qkv_w, layer, norm_g, perms)
    os_, lses = [], []
    for gi, (win, dil) in enumerate(DIL_CONFIGS):
        o, lse = _dil_group(qkv, gi, _dil_bias(table, win, dil), win, dil, bsz, s)
        os_.append(o)
        lses.append(lse)
    return _dil_out(os_, lses, perms, out_w, layer, x)


def _diff_attn_kernel(q_ref, k_ref, v_ref, bank_ref, lam_ref, g_ref, o_ref,
                      vx_ref, q2_ref, s_ref, m_ref, acc_ref, *, tq, tk, seq, lam_init):
    qi = pl.program_id(2)
    tile = DIFF_TILE
    hd = DIFF_HEAD_DIM
    nv = DIFF_V_DIM

    @pl.when(qi == 0)
    def _():
        lane = lax.broadcasted_iota(jnp.int32, (seq, nv), 1)
        vx_ref[:, 0:nv] = v_ref[...]
        vx_ref[:, nv:2 * nv] = jnp.where(lane == 0, 1.0, 0.0).astype(BF16)

    q = (q_ref[...].astype(F32) * (LOG2_E / math.sqrt(hd))).astype(BF16)
    lane = lax.broadcasted_iota(jnp.int32, q.shape, 1)
    zero = jnp.zeros_like(q)
    q2_ref[0:tq, :] = jnp.where(lane < hd, q, zero)
    q2_ref[tq:2 * tq, :] = jnp.where(lane >= hd, q, zero)
    m_ref[...] = jnp.full(m_ref.shape, -jnp.inf, F32)
    acc_ref[...] = jnp.zeros(acc_ref.shape, F32)

    n_chunks = seq // tk
    kt = tk // tile

    def scores(kc, slot):
        start = pl.multiple_of(kc * tk, tk)
        s_ref[slot] = lax.dot_general(q2_ref[...], k_ref[pl.ds(start, tk), :], NT_DIMS,
                                      preferred_element_type=F32)

    def consume(kc, slot):
        start = pl.multiple_of(kc * tk, tk)
        vx = vx_ref[pl.ds(start, tk), :]
        for i in range(2):
            rows = []
            for a in range(tq // tile):
                cols = []
                for b in range(kt):
                    dlt = (kc * kt + b) - (qi * (tq // tile) + a)
                    idx = jnp.clip(dlt, -DIFF_BANK_REACH, DIFF_BANK_REACH) + DIFF_BANK_REACH
                    r0 = i * tq + a * tile
                    cols.append(s_ref[slot, r0:r0 + tile, b * tile:(b + 1) * tile]
                                + bank_ref[i, idx])
                rows.append(cols)
            part = jnp.concatenate([functools.reduce(jnp.maximum, cols) for cols in rows], axis=0)
            row_max = jnp.broadcast_to(jnp.max(part, axis=-1, keepdims=True), part.shape)
            m_old = m_ref[i]
            m_new = jnp.maximum(m_old, row_max)
            alpha = jnp.exp2(m_old - m_new)
            p = jnp.concatenate(
                [jnp.concatenate([jnp.exp2(c - m_new[a * tile:(a + 1) * tile]).astype(BF16)
                                  for c in cols], axis=1) for a, cols in enumerate(rows)], axis=0)
            pv = jnp.dot(p, vx, preferred_element_type=F32)
            acc_ref[i] = jnp.concatenate([alpha, alpha], axis=1) * acc_ref[i] + pv
            m_ref[i] = m_new

    scores(0, 0)

    def body(j, carry):
        for slot in range(2):
            kc = 2 * j + slot
            scores(kc + 1, 1 - slot)
            consume(kc, slot)
        return carry

    lax.fori_loop(0, n_chunks // 2 - 1, body, 0)
    scores(n_chunks - 1, 1)
    consume(n_chunks - 2, 0)
    consume(n_chunks - 1, 1)

    lf = lam_ref[...]
    lam_full = (jnp.exp(jnp.sum(lf[0:1] * lf[1:2], axis=-1, keepdims=True))
                - jnp.exp(jnp.sum(lf[2:3] * lf[3:4], axis=-1, keepdims=True)) + lam_init)
    o = (acc_ref[0, :, 0:nv] / acc_ref[0, :, nv:nv + 1]
         - lam_full * (acc_ref[1, :, 0:nv] / acc_ref[1, :, nv:nv + 1]))
    o = _rms(o, g_ref[...]) * (1.0 - lam_init)
    o_ref[...] = o.astype(o_ref.dtype)


def _diff_attn(qkv, bank, lam, subln_g, lam_init, bsz, s, *, tq=1024, tk=1024):
    hw = 2 * DIFF_HEAD_DIM
    H = DIFF_HEADS
    assert s % tq == 0 and s % (2 * tk) == 0 and tq % DIFF_TILE == 0 and tk % DIFF_TILE == 0
    n_bank = bank.shape[1]
    return pl.pallas_call(
        functools.partial(_diff_attn_kernel, tq=tq, tk=tk, seq=s, lam_init=lam_init),
        out_shape=jax.ShapeDtypeStruct((bsz, s, H * DIFF_V_DIM), BF16),
        grid=(bsz, H, s // tq),
        in_specs=[pl.BlockSpec((None, tq, hw), lambda b, h, i: (b, i, h)),
                  pl.BlockSpec((None, s, hw), lambda b, h, i: (b, 0, H + h)),
                  pl.BlockSpec((None, s, DIFF_V_DIM), lambda b, h, i: (b, 0, 2 * H + h)),
                  pl.BlockSpec((2, n_bank, DIFF_TILE, DIFF_TILE), lambda b, h, i: (h, 0, 0, 0)),
                  pl.BlockSpec(lam.shape, lambda b, h, i: (0, 0)),
                  pl.BlockSpec((1, DIFF_V_DIM), lambda b, h, i: (0, 0))],
        out_specs=pl.BlockSpec((None, tq, DIFF_V_DIM), lambda b, h, i: (b, i, h)),
        scratch_shapes=[pltpu.VMEM((s, 2 * DIFF_V_DIM), BF16),
                        pltpu.VMEM((2 * tq, hw), BF16),
                        pltpu.VMEM((2, 2 * tq, tk), F32),
                        pltpu.VMEM((2, tq, DIFF_TILE), F32),
                        pltpu.VMEM((2, tq, 2 * DIFF_V_DIM), F32)],
        compiler_params=_params("parallel", "parallel", "arbitrary"),
        name="diff_attn",
    )(qkv, qkv, qkv, bank, lam.astype(F32), subln_g.reshape(1, DIFF_V_DIM).astype(F32))


def _diff_bank_buckets():
    reach = DIFF_BANK_REACH
    local = np.arange(DIFF_TILE)[None, :] - np.arange(DIFF_TILE)[:, None]
    tiles = np.stack([_rel_bucket_np(local + (t - reach) * DIFF_TILE) for t in range(2 * reach + 1)])
    assert (tiles[0] == tiles[0, 0, 0]).all() and (tiles[-1] == tiles[-1, 0, 0]).all()
    return tiles


def _diff_layer(x, norm_g, qkv_w, lam, subln_g, out_w, layer, table, layer_idx, bsz, s):
    d = x.shape[1]
    lam_init = 0.8 - 0.6 * math.exp(-0.3 * layer_idx)
    qkv = _mm(x, qkv_w, layer=layer, g=norm_g, out_dtype=BF16, tn=qkv_w.shape[-1], name="diff_qkv")
    bank = _bias_tiles(table, _diff_bank_buckets(), scale=LOG2_E)
    o = _diff_attn(qkv.reshape(bsz, s, 3 * d), bank, lam, subln_g, lam_init, bsz, s)
    return _mm(o.reshape(bsz * s, d), out_w, layer=layer, res=x, name="diff_out")


def kernel(x, rel_bias, norm_mix_g, norm_mlp_g, mlp_w1, mlp_w2, ssd_in_w, ssd_conv_w, ssd_conv_b, ssd_dt_bias, ssd_a_log, ssd_d, ssd_norm_g, ssd_out_w, dil_qkv_w, dil_out_w, diff_qkv_w, diff_lambda, diff_subln_g, diff_out_w, final_norm_g):
    bsz, s, d = x.shape
    depth = norm_mix_g.shape[0]
    (mlp_w1, mlp_w2, ssd_in_w, ssd_out_w, dil_qkv_w, dil_out_w, diff_qkv_w, diff_out_w) = (
        w.astype(BF16) for w in (mlp_w1, mlp_w2, ssd_in_w, ssd_out_w, dil_qkv_w, dil_out_w,
                                 diff_qkv_w, diff_out_w))
    h = x.reshape(bsz * s, d)
    for i in range(depth):
        kind, j = i % 3, i // 3
        if kind == 0:
            h = _ssd_layer(h, norm_mix_g[i], ssd_in_w, j, ssd_conv_w[j], ssd_conv_b[j],
                           ssd_dt_bias[j], ssd_a_log[j], ssd_d[j], ssd_norm_g[j], ssd_out_w, bsz, s)
        elif kind == 1:
            h = _dil_layer(h, norm_mix_g[i], dil_qkv_w, dil_out_w, j, rel_bias, bsz, s)
        else:
            h = _diff_layer(h, norm_mix_g[i], diff_qkv_w, diff_lambda[j], diff_subln_g[j],
                            diff_out_w, j, rel_bias, i, bsz, s)
        h = _mlp(h, norm_mlp_g[i], mlp_w1, mlp_w2, i,
                 final_norm_g if i == depth - 1 else None)
    return h.reshape(bsz, s, d)
```
